```python
import jax
import jax.numpy as jnp
from jax import lax
import numpy as np

D_MODEL = 1024
BATCH = 32
SEQ = 2048
DEPTH = 4

GRID_W = 64
CTX_LEN = 256
EPS = 1e-6
N_MOD = 6

MIX_WIDTH = D_MODEL
LRU_WIDTH = 3 * MIX_WIDTH // 8
LRU_HEADS = 6
LRU_HEAD_DIM = LRU_WIDTH // LRU_HEADS
LRU_CONV = 4
LRU_C = 8.0
RET_WIDTH = 3 * MIX_WIDTH // 8
RET_HEADS = 6
RET_HEAD_DIM = RET_WIDTH // RET_HEADS
RET_CHUNK = 128
ROPE_BASE = 100.0
ROPE_PAIRS = RET_HEAD_DIM // 4
CM_WIDTH = MIX_WIDTH - LRU_WIDTH - RET_WIDTH
CM_GROUPS = 4
CM_GROUP_DIM = CM_WIDTH // CM_GROUPS
CM_CHUNK = 128
IN_WIDTH = 2 * LRU_WIDTH + 4 * RET_WIDTH + 2 * CM_WIDTH
SPLIT_POINTS = (LRU_WIDTH, 2 * LRU_WIDTH, 2 * LRU_WIDTH + RET_WIDTH, 2 * LRU_WIDTH + 2 * RET_WIDTH,
                2 * LRU_WIDTH + 3 * RET_WIDTH, 2 * LRU_WIDTH + 4 * RET_WIDTH)
D_FF = -(-(8 * D_MODEL // 3) // 128) * 128
N_EXPERTS = 8
TOP_K = 2

kernel_name = "hybrid_lru_retention_gmlp_moe_dit"


def rms_norm(x, g):
    xf = x.astype(jnp.float32)
    y = xf * lax.rsqrt(jnp.mean(xf * xf, axis=-1, keepdims=True) + EPS)
    return (y * g.astype(jnp.float32)).astype(x.dtype)


def layer_norm(x, g, b):
    xf = x.astype(jnp.float32)
    mu = jnp.mean(xf, axis=-1, keepdims=True)
    xc = xf - mu
    y = xc * lax.rsqrt(jnp.mean(xc * xc, axis=-1, keepdims=True) + EPS)
    return (y * g.astype(jnp.float32) + b.astype(jnp.float32)).astype(x.dtype)


def head_rms(o):
    return o * lax.rsqrt(jnp.mean(o * o, axis=-1, keepdims=True) + EPS)


def modulate(h, shift, scale):
    return h * (1 + scale) + shift


def apply_rope(x, cos, sin):
    xp = x.reshape(x.shape[:-1] + (-1, 2))
    x0, x1 = xp[..., 0], xp[..., 1]
    c = cos[None, :, None, :]
    s = sin[None, :, None, :]
    return jnp.stack([x0 * c - x1 * s, x0 * s + x1 * c], axis=-1).reshape(x.shape)


def dwconv_centred(x, w, b):
    y = lax.conv_general_dilated(
        x, w[:, None, :].astype(x.dtype), window_strides=(1,),
        padding=[(LRU_CONV // 2, LRU_CONV - 1 - LRU_CONV // 2)],
        dimension_numbers=('NWC', 'WIO', 'NWC'), feature_group_count=x.shape[-1])
    return y + b


def _linear_combine(left, right):
    a_l, b_l = left
    a_r, b_r = right
    return a_l * a_r, a_r * b_l + b_r


def rglru_scan(x, w_a, b_a, w_x, b_x, lam, h0):
    r = jax.nn.sigmoid(jnp.einsum('blhi,hij->blhj', x, w_a) + b_a)
    i = jax.nn.sigmoid(jnp.einsum('blhi,hij->blhj', x, w_x) + b_x)
    log_a = -LRU_C * r * jax.nn.softplus(-lam)
    a = jnp.exp(log_a)
    u = jnp.sqrt(-jnp.expm1(2.0 * log_a)) * (i * x)
    a_cum, h = lax.associative_scan(_linear_combine, (a, u), axis=1)
    h = h + a_cum * h0[:, None]
    return h, h[:, -1]


def rglru_bidir(x_ctx, x_lat, wa, ba, wx, bx, lam, want_ctx):
    zeros = jnp.zeros((x_ctx.shape[0], LRU_HEADS, LRU_HEAD_DIM), jnp.float32)
    hc_f, s_f = rglru_scan(x_ctx, wa[0], ba[0], wx[0], bx[0], lam[0], zeros)
    hc_b, s_b = rglru_scan(x_ctx[:, ::-1], wa[1], ba[1], wx[1], bx[1], lam[1], zeros)
    hl_f, _ = rglru_scan(x_lat, wa[0], ba[0], wx[0], bx[0], lam[0], s_f)
    hl_b, _ = rglru_scan(x_lat[:, ::-1], wa[1], ba[1], wx[1], bx[1], lam[1], s_b)
    lat = hl_f + hl_b[:, ::-1]
    ctx = hc_f + hc_b[:, ::-1] if want_ctx else None
    return lat, ctx


def retention_chunkwise(q, k, v, log_gamma, s0):
    B, L, H, d = q.shape
    n = L // RET_CHUNK
    qc = q.reshape(B, n, RET_CHUNK, H, d)
    kc = k.reshape(B, n, RET_CHUNK, H, d)
    vc = v.reshape(B, n, RET_CHUNK, H, d)
    pos = jnp.arange(RET_CHUNK, dtype=jnp.float32)
    diff = pos[:, None] - pos[None, :]
    decay_intra = jnp.where(diff >= 0, jnp.exp(log_gamma[:, None, None] * jnp.maximum(diff, 0.0)), 0.0)
    scores = jnp.einsum('bnihd,bnjhd->bnhij', qc, kc) * decay_intra
    intra = jnp.einsum('bnhij,bnjhd->bnihd', scores, vc)
    k_decay = jnp.exp(log_gamma[None, :] * (RET_CHUNK - 1 - pos)[:, None])
    chunk_kv = jnp.einsum('bnjhd,jh,bnjhe->bnhde', kc, k_decay, vc)
    chunk_decay = jnp.exp(log_gamma * RET_CHUNK)[None, :, None, None]

    def step(s, kv):
        return chunk_decay * s + kv, s

    s_final, s_before = lax.scan(step, s0, jnp.moveaxis(chunk_kv, 1, 0))
    s_before = jnp.moveaxis(s_before, 0, 1)
    q_decay = jnp.exp(log_gamma[None, :] * (pos + 1)[:, None])
    inter = jnp.einsum('bnihd,ih,bnhde->bnihe', qc, q_decay, s_before)
    return (intra + inter).reshape(B, L, H, d), s_final


def retention_final_state(k, v, log_gamma):
    L = k.shape[1]
    w = jnp.exp(log_gamma[None, :] * (L - 1 - jnp.arange(L, dtype=jnp.float32))[:, None])
    return jnp.einsum('blhd,lh,blhe->bhde', k, w, v)


def retention_bidir(q_c, k_c, v_c, q_l, k_l, v_l, theta, want_ctx):
    lg_f = jax.nn.log_sigmoid(theta[0].astype(jnp.float32))
    lg_b = jax.nn.log_sigmoid(theta[1].astype(jnp.float32))
    flip = lambda t: t[:, ::-1]
    if want_ctx:
        zeros = jnp.zeros((q_c.shape[0], RET_HEADS, RET_HEAD_DIM, RET_HEAD_DIM), jnp.float32)
        oc_f, s_f = retention_chunkwise(q_c, k_c, v_c, lg_f, zeros)
        oc_b, s_b = retention_chunkwise(flip(q_c), flip(k_c), flip(v_c), lg_b, zeros)
        o_ctx = oc_f + flip(oc_b)
    else:
        s_f = retention_final_state(k_c, v_c, lg_f)
        s_b = retention_final_state(flip(k_c), flip(v_c), lg_b)
        o_ctx = None
    ol_f, _ = retention_chunkwise(q_l, k_l, v_l, lg_f, s_f)
    ol_b, _ = retention_chunkwise(flip(q_l), flip(k_l), flip(v_l), lg_b, s_b)
    return ol_f + flip(ol_b), o_ctx


def chunk_mlp(z, ln_g, ln_b, ws, bs):
    z = jax.nn.gelu(z)
    u, v = jnp.split(z, 2, axis=-1)
    B, L, _ = v.shape
    v = v.reshape(B, L // CM_CHUNK, CM_CHUNK, CM_GROUPS, CM_GROUP_DIM)
    v = layer_norm(v, ln_g, ln_b)
    s = jnp.einsum('gij,bnjgc->bnigc', ws, v) + jnp.transpose(bs)[None, None, :, :, None]
    return u * s.reshape(B, L, CM_WIDTH)


def project(h, w_in, conv_w, conv_b):
    B, L, _ = h.shape
    p = h @ w_in
    lx, ly, q, k, v, g, z = jnp.split(p, SPLIT_POINTS, axis=-1)
    lx = dwconv_centred(lx, conv_w, conv_b).reshape(B, L, LRU_HEADS, LRU_HEAD_DIM).astype(jnp.float32)
    heads = lambda t: t.reshape(B, L, RET_HEADS, RET_HEAD_DIM).astype(jnp.float32)
    return lx, ly, heads(q) * (RET_HEAD_DIM ** -0.5), heads(k), heads(v), g, z


def merge(ly, lru, g, ret, z, w_out, ln_g, ln_b, ws, bs):
    B, L, _ = ly.shape
    lru_out = jax.nn.gelu(ly) * lru.reshape(B, L, LRU_WIDTH).astype(ly.dtype)
    ret_out = jax.nn.silu(g) * head_rms(ret).reshape(B, L, RET_WIDTH).astype(g.dtype)
    cm_out = chunk_mlp(z, ln_g, ln_b, ws, bs)
    return jnp.concatenate([lru_out, ret_out, cm_out], axis=-1) @ w_out


def hybrid_mixer(h_lat, h_ctx, cos, sin, w_in, w_out, conv_w, conv_b, wa, ba, wx, bx, lam, theta,
                 ln_g, ln_b, ws, bs, want_ctx):
    lx_c, ly_c, q_c, k_c, v_c, g_c, z_c = project(h_ctx, w_in, conv_w, conv_b)
    lx_l, ly_l, q_l, k_l, v_l, g_l, z_l = project(h_lat, w_in, conv_w, conv_b)
    q_l = apply_rope(q_l, cos, sin)
    k_l = apply_rope(k_l, cos, sin)
    lru_l, lru_c = rglru_bidir(lx_c, lx_l, wa, ba, wx, bx, lam, want_ctx)
    ret_l, ret_c = retention_bidir(q_c, k_c, v_c, q_l, k_l, v_l, theta, want_ctx)
    y_lat = merge(ly_l, lru_l, g_l, ret_l, z_l, w_out, ln_g, ln_b, ws, bs)
    y_ctx = merge(ly_c, lru_c, g_c, ret_c, z_c, w_out, ln_g, ln_b, ws, bs) if want_ctx else None
    return y_lat, y_ctx


def swiglu(h, w1, w3, w2):
    return (jax.nn.silu(h @ w1) * (h @ w3)) @ w2


def moe_swiglu(h, router_w, w1, w3, w2):
    shp = h.shape
    t = h.reshape(-1, shp[-1])
    logits = (t @ router_w).astype(jnp.float32)
    top_logit, top_idx = lax.top_k(logits, TOP_K)
    top_w = jax.nn.softmax(top_logit, axis=-1)
    combine = jnp.sum(jax.nn.one_hot(top_idx, N_EXPERTS, dtype=jnp.float32) * top_w[..., None], axis=1)
    y = jnp.zeros_like(t)
    for e in range(N_EXPERTS):
        y = y + combine[:, e:e + 1].astype(t.dtype) * swiglu(t, w1[e], w3[e], w2[e])
    return y.reshape(shp)


def setup_inputs(seed: int = 0) -> dict:
    key = jax.random.key(seed)
    ks = jax.random.split(key, 32)
    f32 = jnp.float32
    n_dense = (DEPTH + 1) // 2
    n_moe = DEPTH // 2

    def nrm(k, shape, scale):
        return jax.random.normal(k, shape, f32) * scale

    lam_u = jax.random.uniform(ks[18], (DEPTH, 2, LRU_HEADS, LRU_HEAD_DIM), f32, 0.9, 0.999)
    lam_s = lam_u ** (1.0 / LRU_C)
    gam = 1.0 - jnp.exp2(-5.0 - jnp.arange(RET_HEADS, dtype=f32))
    theta0 = jnp.log(gam) - jnp.log1p(-gam)
    return {
        "x": nrm(ks[0], (BATCH, SEQ, D_MODEL), 1.0),
        "c": nrm(ks[1], (BATCH, D_MODEL), 1.0),
        "ctx": nrm(ks[2], (BATCH, CTX_LEN, D_MODEL), 1.0),
        "c_ctx": nrm(ks[3], (D_MODEL,), 1.0),
        "w_mod": nrm(ks[4], (DEPTH, D_MODEL, N_MOD * D_MODEL), 0.5 * D_MODEL ** -0.5),
        "b_mod": nrm(ks[5], (DEPTH, N_MOD * D_MODEL), 0.02),
        "g_mix_pre": 1.0 + nrm(ks[6], (DEPTH, D_MODEL), 0.05),
        "g_mix_post": 1.0 + nrm(ks[7], (DEPTH, D_MODEL), 0.05),
        "g_ffn_pre": 1.0 + nrm(ks[8], (DEPTH, D_MODEL), 0.05),
        "g_ffn_post": 1.0 + nrm(ks[9], (DEPTH, D_MODEL), 0.05),
        "w_in": nrm(ks[10], (DEPTH, D_MODEL, IN_WIDTH), D_MODEL ** -0.5),
        "w_out": nrm(ks[11], (DEPTH, MIX_WIDTH, D_MODEL), MIX_WIDTH ** -0.5),
        "lru_conv_w": nrm(ks[12], (DEPTH, LRU_CONV, LRU_WIDTH), LRU_CONV ** -0.5),
        "lru_conv_b": nrm(ks[13], (DEPTH, LRU_WIDTH), 0.02),
        "lru_wa": nrm(ks[14], (DEPTH, 2, LRU_HEADS, LRU_HEAD_DIM, LRU_HEAD_DIM), LRU_HEAD_DIM ** -0.5),
        "lru_ba": nrm(ks[15], (DEPTH, 2, LRU_HEADS, LRU_HEAD_DIM), 0.02),
        "lru_wx": nrm(ks[16], (DEPTH, 2, LRU_HEADS, LRU_HEAD_DIM, LRU_HEAD_DIM), LRU_HEAD_DIM ** -0.5),
        "lru_bx": nrm(ks[17], (DEPTH, 2, LRU_HEADS, LRU_HEAD_DIM), 0.02),
        "lru_lam": jnp.log(lam_s) - jnp.log1p(-lam_s),
        "ret_theta": theta0 + nrm(ks[19], (DEPTH, 2, RET_HEADS), 0.05),
        "cm_ln_g": 1.0 + nrm(ks[20], (DEPTH, CM_GROUPS, CM_GROUP_DIM), 0.05),
        "cm_ln_b": nrm(ks[21], (DEPTH, CM_GROUPS, CM_GROUP_DIM), 0.02),
        "cm_ws": nrm(ks[22], (DEPTH, CM_GROUPS, CM_CHUNK, CM_CHUNK), CM_CHUNK ** -0.5),
        "cm_bs": 1.0 + nrm(ks[23], (DEPTH, CM_GROUPS, CM_CHUNK), 0.05),
        "ffn_w1": nrm(ks[24], (n_dense, D_MODEL, D_FF), D_MODEL ** -0.5),
        "ffn_w3": nrm(ks[25], (n_dense, D_MODEL, D_FF), D_MODEL ** -0.5),
        "ffn_w2": nrm(ks[26], (n_dense, D_FF, D_MODEL), D_FF ** -0.5),
        "router_w": nrm(ks[27], (n_moe, D_MODEL, N_EXPERTS), D_MODEL ** -0.5),
        "moe_w1": nrm(ks[28], (n_moe, N_EXPERTS, D_MODEL, D_FF), D_MODEL ** -0.5),
        "moe_w3": nrm(ks[29], (n_moe, N_EXPERTS, D_MODEL, D_FF), D_MODEL ** -0.5),
        "moe_w2": nrm(ks[30], (n_moe, N_EXPERTS, D_FF, D_MODEL), D_FF ** -0.5),
    }


def reference(x, c, ctx, c_ctx, w_mod, b_mod, g_mix_pre, g_mix_post, g_ffn_pre, g_ffn_post, w_in, w_out,
              lru_conv_w, lru_conv_b, lru_wa, lru_ba, lru_wx, lru_bx, lru_lam, ret_theta,
              cm_ln_g, cm_ln_b, cm_ws, cm_bs, ffn_w1, ffn_w3, ffn_w2, router_w, moe_w1, moe_w3, moe_w2):
    L = x.shape[1]
    n_rows = L // GRID_W
    rows = jnp.repeat(jnp.arange(n_rows, dtype=jnp.float32), GRID_W)
    cols = jnp.tile(jnp.arange(GRID_W, dtype=jnp.float32), n_rows)
    freqs = ROPE_BASE ** (-jnp.arange(ROPE_PAIRS, dtype=jnp.float32) / ROPE_PAIRS)
    ang = jnp.concatenate([rows[:, None] * freqs, cols[:, None] * freqs], axis=-1)
    cos, sin = jnp.cos(ang), jnp.sin(ang)
    s_lat = jax.nn.silu(c)
    s_ctx = jax.nn.silu(c_ctx)
    for l in range(DEPTH):
        want_ctx = l < DEPTH - 1
        mod_l = (s_lat @ w_mod[l] + b_mod[l])[:, None, :]
        mod_c = s_ctx @ w_mod[l] + b_mod[l]
        sh_m, sc_m, gt_m, sh_f, sc_f, gt_f = jnp.split(mod_l, N_MOD, axis=-1)
        csh_m, csc_m, cgt_m, csh_f, csc_f, cgt_f = jnp.split(mod_c, N_MOD, axis=-1)
        h_lat = modulate(rms_norm(x, g_mix_pre[l]), sh_m, sc_m)
        h_ctx = modulate(rms_norm(ctx, g_mix_pre[l]), csh_m, csc_m)
        y_lat, y_ctx = hybrid_mixer(h_lat, h_ctx, cos, sin, w_in[l], w_out[l], lru_conv_w[l], lru_conv_b[l],
                                    lru_wa[l], lru_ba[l], lru_wx[l], lru_bx[l], lru_lam[l], ret_theta[l],
                                    cm_ln_g[l], cm_ln_b[l], cm_ws[l], cm_bs[l], want_ctx)
        x = x + gt_m * rms_norm(y_lat, g_mix_post[l])
        if want_ctx:
            ctx = ctx + cgt_m * rms_norm(y_ctx, g_mix_post[l])
        j = l // 2
        if l % 2 == 0:
            ffn = lambda h, j=j: swiglu(h, ffn_w1[j], ffn_w3[j], ffn_w2[j])
        else:
            ffn = lambda h, j=j: moe_swiglu(h, router_w[j], moe_w1[j], moe_w3[j], moe_w2[j])
        h_lat = modulate(rms_norm(x, g_ffn_pre[l]), sh_f, sc_f)
        x = x + gt_f * rms_norm(ffn(h_lat), g_ffn_post[l])
        if want_ctx:
            h_ctx = modulate(rms_norm(ctx, g_ffn_pre[l]), csh_f, csc_f)
            ctx = ctx + cgt_f * rms_norm(ffn(h_ctx), g_ffn_post[l])
    return x
```

```python
import functools

import jax
import jax.numpy as jnp
from jax import lax
from jax.experimental import pallas as pl
from jax.experimental.pallas import tpu as pltpu

F32 = jnp.float32
BF16 = jnp.bfloat16

EPS = 1e-6
N_MOD = 6
LRU_HEADS = 6
LRU_CONV = 4
LRU_C = 8.0
RET_HEADS = 6
ROPE_BASE = 100.0
GRID_W = 64
CM_GROUPS = 4
CM_CHUNK = 128
TOP_K = 2

V7X_LANES = 128
V7X_SUBLANES = 8
V7X_VMEM_LIMIT = 56 * 1024 * 1024

ROW_BLOCK = 768
RET_CHUNK = 256
FF_CHUNK = 256
MOE_TILE = 512
SCAN_ROWS = 32


def _cparams(*sem):
    return pltpu.CompilerParams(dimension_semantics=sem, vmem_limit_bytes=V7X_VMEM_LIMIT)


def _const_spec(shape):
    nd = len(shape)
    return pl.BlockSpec(shape, lambda *_: (0,) * nd, pipeline_mode=pl.Buffered(1))


def _rms(x):
    return x * lax.rsqrt(jnp.mean(x * x, axis=-1, keepdims=True) + EPS)


def _gelu_tanh(x):
    return 0.5 * x * (1.0 + jnp.tanh(0.7978845608028654 * (x + 0.044715 * (x * x * x))))


def _sigmoid(x):
    return 1.0 / (1.0 + jnp.exp(-x))


def _silu(x):
    return x * _sigmoid(x)


def _mod_kernel(c_ref, w_ref, b_ref, o_ref):
    s = _silu(c_ref[...])
    o_ref[...] = jnp.dot(s, w_ref[...], preferred_element_type=F32) + b_ref[...]


def _modulation(cpad, w_mod, b_mod):
    depth, d, nd = w_mod.shape
    r = cpad.shape[0]
    return pl.pallas_call(
        _mod_kernel,
        grid=(depth, nd // d),
        in_specs=[
            pl.BlockSpec((r, d), lambda l, n: (0, 0)),
            pl.BlockSpec((None, d, d), lambda l, n: (l, 0, n)),
            pl.BlockSpec((None, 1, d), lambda l, n: (l, 0, n)),
        ],
        out_specs=pl.BlockSpec((None, r, d), lambda l, n: (l, 0, n)),
        out_shape=jax.ShapeDtypeStruct((depth, r, nd), F32),
        compiler_params=_cparams("parallel", "parallel"),
        name="modulation",
    )(cpad, w_mod, b_mod.reshape(depth, 1, nd))


def _row_mod(ml_ref, mc_ref, idx, first_block):
    lat = ml_ref[0, idx:idx + 1, :]
    top = jnp.where(first_block, mc_ref[0, idx:idx + 1, :], lat)
    return top, lat


def _prenorm_modulate(x_ref, g_ref, ml_ref, mc_ref, h_ref, ctx_len, shift_idx):
    first = pl.program_id(1) == 0
    sh_top, sh_lat = _row_mod(ml_ref, mc_ref, shift_idx, first)
    sc_top, sc_lat = _row_mod(ml_ref, mc_ref, shift_idx + 1, first)
    y = _rms(x_ref[0]) * g_ref[...]
    h_ref[:ctx_len, :] = (y[:ctx_len] * (1.0 + sc_top) + sh_top).astype(h_ref.dtype)
    h_ref[ctx_len:, :] = (y[ctx_len:] * (1.0 + sc_lat) + sh_lat).astype(h_ref.dtype)


def _postnorm_residual(x_ref, y, g_ref, ml_ref, mc_ref, o_ref, ctx_len, gate_idx):
    first = pl.program_id(1) == 0
    gt_top, gt_lat = _row_mod(ml_ref, mc_ref, gate_idx, first)
    r = _rms(y) * g_ref[...]
    x = x_ref[0]
    o_ref[0, :ctx_len, :] = x[:ctx_len] + gt_top * r[:ctx_len]
    o_ref[0, ctx_len:, :] = x[ctx_len:] + gt_lat * r[ctx_len:]


def _row_specs(rb, d, batch):
    x_spec = pl.BlockSpec((1, rb, d), lambda b, j: (b, j, 0))
    g_spec = _const_spec((1, d))
    ml_spec = pl.BlockSpec((1, N_MOD, d), lambda b, j: (b, 0, 0))
    mc_spec = pl.BlockSpec((1, N_MOD, d), lambda b, j: (batch, 0, 0))
    return x_spec, g_spec, ml_spec, mc_spec


def _inproj_kernel(x_ref, g_ref, ml_ref, mc_ref, wl_ref, wr_ref, wc_ref,
                   ol_ref, or_ref, oc_ref, h_ref, *, ctx_len):
    _prenorm_modulate(x_ref, g_ref, ml_ref, mc_ref, h_ref, ctx_len, 0)
    h = h_ref[...]
    ol_ref[0] = jnp.dot(h, wl_ref[...], preferred_element_type=F32).astype(ol_ref.dtype)
    or_ref[0] = jnp.dot(h, wr_ref[...], preferred_element_type=F32).astype(or_ref.dtype)
    oc_ref[0] = jnp.dot(h, wc_ref[...], preferred_element_type=F32).astype(oc_ref.dtype)


def _inproj(xs, g, mod, w_lru, w_ret, w_cm, ctx_len):
    b, s, d = xs.shape
    rb = ROW_BLOCK
    x_spec, g_spec, ml_spec, mc_spec = _row_specs(rb, d, b)
    outs = [w.shape[1] for w in (w_lru, w_ret, w_cm)]
    return pl.pallas_call(
        functools.partial(_inproj_kernel, ctx_len=ctx_len),
        grid=(b, s // rb),
        in_specs=[x_spec, g_spec, ml_spec, mc_spec] + [_const_spec(w.shape) for w in (w_lru, w_ret, w_cm)],
        out_specs=[pl.BlockSpec((1, rb, n), lambda b_, j: (b_, j, 0)) for n in outs],
        out_shape=[jax.ShapeDtypeStruct((b, s, n), BF16) for n in outs],
        scratch_shapes=[pltpu.VMEM((rb, d), BF16)],
        compiler_params=_cparams("parallel", "parallel"),
        name="mixer_inproj",
    )(xs, g, mod, mod, w_lru, w_ret, w_cm)


def _tile_scan(a, u, carry, row, reverse):
    n = V7X_SUBLANES
    for dist in (1, 2, 4):
        if reverse:
            keep = row < (n - dist)
            shift = n - dist
        else:
            keep = row >= dist
            shift = dist
        a_s = jnp.where(keep, pltpu.roll(a, shift, 0), 1.0)
        u_s = jnp.where(keep, pltpu.roll(u, shift, 0), 0.0)
        u = a * u_s + u
        a = a * a_s
    h = a * carry + u
    new_carry = h[0:1] if reverse else h[n - 1:n]
    return h, new_carry


def _lru_kernel(p_ref, cw_ref, cb_ref, wg_ref, bg_ref, lam_ref, o_ref,
                xpad_ref, a_ref, u_ref, *, ctx_len, seq):
    w = cw_ref.shape[1]
    s = seq
    pad = V7X_SUBLANES
    chunk = ctx_len
    zeros = jnp.zeros((pad, w), F32)
    xpad_ref[0:pad, :] = zeros
    xpad_ref[pad + ctx_len:2 * pad + ctx_len, :] = zeros
    xpad_ref[2 * pad + s:3 * pad + s, :] = zeros
    xpad_ref[pad:pad + ctx_len, :] = p_ref[0, 0:ctx_len, 0:w].astype(F32)
    xpad_ref[2 * pad + ctx_len:2 * pad + s, :] = p_ref[0, ctx_len:s, 0:w].astype(F32)

    neg = -lam_ref[...]
    softplus = jnp.maximum(neg, 0.0) + jnp.log(1.0 + jnp.exp(-jnp.abs(neg)))
    log_base = -LRU_C * softplus

    for c in range(s // chunk):
        r0 = c * chunk
        base = r0 + (pad if r0 < ctx_len else 2 * pad)
        xc = jnp.zeros((chunk, w), F32) + cb_ref[...]
        for k in range(LRU_CONV):
            off = k - LRU_CONV // 2
            xc = xc + xpad_ref[base + off:base + off + chunk, :] * cw_ref[k:k + 1, :]
        gates = jnp.dot(xc.astype(BF16), wg_ref[...], preferred_element_type=F32) + bg_ref[...]
        for d in range(2):
            r = _sigmoid(gates[:, (2 * d) * w:(2 * d + 1) * w])
            i = _sigmoid(gates[:, (2 * d + 1) * w:(2 * d + 2) * w])
            log_a = log_base[d:d + 1, :] * r
            a = jnp.exp(log_a)
            a_ref[d, r0:r0 + chunk, :] = a
            u_ref[d, r0:r0 + chunk, :] = jnp.sqrt(1.0 - a * a) * (i * xc)

    row = lax.broadcasted_iota(jnp.int32, (V7X_SUBLANES, w), 0)
    tiles = SCAN_ROWS // V7X_SUBLANES

    def fwd_body(it, carry):
        r0 = pl.multiple_of(it * SCAN_ROWS, SCAN_ROWS)
        for t in range(tiles):
            sl = pl.ds(r0 + t * V7X_SUBLANES, V7X_SUBLANES)
            h, carry = _tile_scan(a_ref[0, sl, :], u_ref[0, sl, :], carry, row, False)
            u_ref[0, sl, :] = h
        return carry

    lax.fori_loop(0, s // SCAN_ROWS, fwd_body, jnp.zeros((1, w), F32))

    def bwd_body(hi):
        def body(it, carry):
            r0 = pl.multiple_of(hi - (it + 1) * SCAN_ROWS, SCAN_ROWS)
            for t in reversed(range(tiles)):
                sl = pl.ds(r0 + t * V7X_SUBLANES, V7X_SUBLANES)
                h, carry = _tile_scan(a_ref[1, sl, :], u_ref[1, sl, :], carry, row, True)
                u_ref[1, sl, :] = h
            return carry
        return body

    carry = lax.fori_loop(0, ctx_len // SCAN_ROWS, bwd_body(ctx_len), jnp.zeros((1, w), F32))
    lax.fori_loop(0, (s - ctx_len) // SCAN_ROWS, bwd_body(s), carry)

    for c in range(s // chunk):
        r0 = c * chunk
        ly = p_ref[0, r0:r0 + chunk, w:2 * w].astype(F32)
        h = u_ref[0, r0:r0 + chunk, :] + u_ref[1, r0:r0 + chunk, :]
        o_ref[0, r0:r0 + chunk, :] = (_gelu_tanh(ly) * h).astype(o_ref.dtype)


def _lru(p_lru, conv_w, conv_b, wg, bg, lam, ctx_len):
    b, s, w2 = p_lru.shape
    w = w2 // 2
    return pl.pallas_call(
        functools.partial(_lru_kernel, ctx_len=ctx_len, seq=s),
        grid=(b,),
        in_specs=[
            pl.BlockSpec((1, s, w2), lambda i: (i, 0, 0)),
            _const_spec(conv_w.shape), _const_spec(conv_b.shape),
            _const_spec(wg.shape), _const_spec(bg.shape), _const_spec(lam.shape),
        ],
        out_specs=pl.BlockSpec((1, s, w), lambda i: (i, 0, 0)),
        out_shape=jax.ShapeDtypeStruct((b, s, w), BF16),
        scratch_shapes=[
            pltpu.VMEM((s + 3 * V7X_SUBLANES, w), F32),
            pltpu.VMEM((2, s, w), F32),
            pltpu.VMEM((2, s, w), F32),
        ],
        compiler_params=_cparams("parallel"),
        name="rglru",
    )(p_lru, conv_w, conv_b, wg, bg, lam)


def _rope(x, cos, sin_signed, lane_even):
    partner = jnp.where(lane_even, pltpu.roll(x, V7X_LANES - 1, 1), pltpu.roll(x, 1, 1))
    return x * cos + partner * sin_signed


def _ret_kernel(lg_ref, p_ref, cos_ref, sin_ref, lgl_ref, o_ref,
                q_ref, k_ref, dlt_ref, st_ref, *, ctx_len, seq, head_dim):
    s = seq
    c = RET_CHUNK
    lanes = V7X_LANES
    w = o_ref.shape[2]
    n_pairs = w // lanes
    n_chunks = s // c
    n_ctx = ctx_len // c
    scale = head_dim ** -0.5

    rowi = lax.broadcasted_iota(jnp.int32, (c, lanes), 0).astype(F32)
    lane = lax.broadcasted_iota(jnp.int32, (c, lanes), 1)
    lane_even = (lane % 2) == 0
    low_half = lane < head_dim
    dif = (lax.broadcasted_iota(jnp.int32, (c, c), 0)
           - lax.broadcasted_iota(jnp.int32, (c, c), 1)).astype(F32)
    blk_row_low = lax.broadcasted_iota(jnp.int32, (lanes, lanes), 0) < head_dim
    blk_col_low = lax.broadcasted_iota(jnp.int32, (lanes, lanes), 1) < head_dim
    blk_diag = blk_row_low == blk_col_low

    order_b = list(range(n_ctx - 1, -1, -1)) + list(range(n_chunks - 1, n_ctx - 1, -1))

    for p in range(n_pairs):
        lsl = slice(p * lanes, (p + 1) * lanes)
        lg_f = lgl_ref[0:1, lsl]
        lg_b = lgl_ref[1:2, lsl]
        kd_f = jnp.exp(lg_f * (c - 1.0 - rowi))
        kd_b = jnp.exp(lg_b * rowi)
        qd_f = jnp.exp(lg_f * (rowi + 1.0))
        qd_b = jnp.exp(lg_b * (c - rowi))

        for ch in range(n_chunks):
            r0 = ch * c
            q = p_ref[0, r0:r0 + c, p * lanes:(p + 1) * lanes].astype(F32)
            k = p_ref[0, r0:r0 + c, w + p * lanes:w + (p + 1) * lanes].astype(F32)
            v = p_ref[0, r0:r0 + c, 2 * w + p * lanes:2 * w + (p + 1) * lanes]
            if ch >= n_ctx:
                t0 = r0 - ctx_len
                cos = cos_ref[t0:t0 + c, :]
                sin = sin_ref[t0:t0 + c, :]
                q = _rope(q, cos, sin, lane_even)
                k = _rope(k, cos, sin, lane_even)
            q_ref[r0:r0 + c, lsl] = (q * scale).astype(BF16)
            k_ref[r0:r0 + c, lsl] = k.astype(BF16)
            d_f = pl.dot((k * kd_f).astype(BF16), v, trans_a=True)
            d_b = pl.dot((k * kd_b).astype(BF16), v, trans_a=True)
            dlt_ref[0, ch] = jnp.where(blk_diag, d_f, 0.0)
            dlt_ref[1, ch] = jnp.where(blk_diag, d_b, 0.0)

        gc_f = jnp.exp(jnp.where(blk_row_low, lg_ref[0, 2 * p], lg_ref[0, 2 * p + 1]) * float(c))
        gc_b = jnp.exp(jnp.where(blk_row_low, lg_ref[1, 2 * p], lg_ref[1, 2 * p + 1]) * float(c))
        state = jnp.zeros((lanes, lanes), F32)
        for ch in range(n_chunks):
            st_ref[0, ch] = state.astype(BF16)
            state = gc_f * state + dlt_ref[0, ch]
        state = jnp.zeros((lanes, lanes), F32)
        for ch in order_b:
            st_ref[1, ch] = state.astype(BF16)
            state = gc_b * state + dlt_ref[1, ch]

        dmats = []
        for h in range(2):
            lgf_s = lg_ref[0, 2 * p + h]
            lgb_s = lg_ref[1, 2 * p + h]
            dmats.append(jnp.where(dif > 0, jnp.exp(lgf_s * jnp.maximum(dif, 0.0)),
                                   jnp.where(dif < 0, jnp.exp(lgb_s * jnp.maximum(-dif, 0.0)), 2.0)))

        def out_chunk(ch, _):
            r0 = pl.multiple_of(ch * c, c)
            rows = pl.ds(r0, c)
            q = q_ref[rows, lsl]
            k = k_ref[rows, lsl]
            v = p_ref[0, rows, 2 * w + p * lanes:2 * w + (p + 1) * lanes]
            g = p_ref[0, rows, 3 * w + p * lanes:3 * w + (p + 1) * lanes].astype(F32)
            zero = jnp.zeros_like(q)
            s0 = pl.dot(jnp.where(low_half, q, zero), k, trans_b=True)
            s1 = pl.dot(jnp.where(low_half, zero, q), k, trans_b=True)
            o0 = jnp.dot((s0 * dmats[0]).astype(BF16), v, preferred_element_type=F32)
            o1 = jnp.dot((s1 * dmats[1]).astype(BF16), v, preferred_element_type=F32)
            o = jnp.where(low_half, o0, o1)
            qf = q.astype(F32)
            o = o + jnp.dot((qf * qd_f).astype(BF16), st_ref[0, ch], preferred_element_type=F32)
            o = o + jnp.dot((qf * qd_b).astype(BF16), st_ref[1, ch], preferred_element_type=F32)
            oo = o * o
            ss0 = jnp.sum(jnp.where(low_half, oo, 0.0), axis=-1, keepdims=True)
            ss1 = jnp.sum(jnp.where(low_half, 0.0, oo), axis=-1, keepdims=True)
            inv = jnp.where(low_half, lax.rsqrt(ss0 / head_dim + EPS), lax.rsqrt(ss1 / head_dim + EPS))
            o_ref[0, rows, lsl] = (_silu(g) * (o * inv)).astype(o_ref.dtype)
            return 0

        lax.fori_loop(0, n_chunks, out_chunk, 0)


def _retention(p_ret, lg, lg_lane, cos, sin_signed, ctx_len, head_dim):
    b, s, w4 = p_ret.shape
    w = w4 // 4
    n_chunks = s // RET_CHUNK
    return pl.pallas_call(
        functools.partial(_ret_kernel, ctx_len=ctx_len, seq=s, head_dim=head_dim),
        grid=(b,),
        in_specs=[
            pl.BlockSpec(memory_space=pltpu.SMEM),
            pl.BlockSpec((1, s, w4), lambda i: (i, 0, 0)),
            _const_spec(cos.shape), _const_spec(sin_signed.shape), _const_spec(lg_lane.shape),
        ],
        out_specs=pl.BlockSpec((1, s, w), lambda i: (i, 0, 0)),
        out_shape=jax.ShapeDtypeStruct((b, s, w), BF16),
        scratch_shapes=[
            pltpu.VMEM((s, w), BF16),
            pltpu.VMEM((s, w), BF16),
            pltpu.VMEM((2, n_chunks, V7X_LANES, V7X_LANES), F32),
            pltpu.VMEM((2, n_chunks, V7X_LANES, V7X_LANES), BF16),
        ],
        compiler_params=_cparams("parallel"),
        name="retention",
    )(lg, p_ret, cos, sin_signed, lg_lane)


def _cm_kernel(z_ref, lng_ref, lnb_ref, ws_ref, bs_ref, o_ref, *, seq, group_dim):
    c = CM_CHUNK
    lanes = V7X_LANES
    w = o_ref.shape[2]
    n_slabs = w // lanes
    low_half = lax.broadcasted_iota(jnp.int32, (c, lanes), 1) < group_dim
    lane_w = lax.broadcasted_iota(jnp.int32, (c, w), 1)

    def body(ch, _):
        r0 = pl.multiple_of(ch * c, c)
        rows = pl.ds(r0, c)
        z = _gelu_tanh(z_ref[0, rows, :].astype(F32))
        u = z[:, :w]
        vn = []
        for sl in range(n_slabs):
            v = z[:, w + sl * lanes:w + (sl + 1) * lanes]
            m0 = jnp.sum(jnp.where(low_half, v, 0.0), axis=-1, keepdims=True) / group_dim
            m1 = jnp.sum(jnp.where(low_half, 0.0, v), axis=-1, keepdims=True) / group_dim
            xc = v - jnp.where(low_half, m0, m1)
            xx = xc * xc
            v0 = jnp.sum(jnp.where(low_half, xx, 0.0), axis=-1, keepdims=True) / group_dim
            v1 = jnp.sum(jnp.where(low_half, 0.0, xx), axis=-1, keepdims=True) / group_dim
            vn.append(xc * jnp.where(low_half, lax.rsqrt(v0 + EPS), lax.rsqrt(v1 + EPS)))
        vn = jnp.concatenate(vn, axis=-1) * lng_ref[...] + lnb_ref[...]
        s_all = jnp.dot(ws_ref[...], vn.astype(BF16), preferred_element_type=F32)
        sp = s_all[0:c]
        for gi in range(1, CM_GROUPS):
            sp = jnp.where(lane_w >= gi * group_dim, s_all[gi * c:(gi + 1) * c], sp)
        o_ref[0, rows, :] = (u * (sp + bs_ref[...])).astype(o_ref.dtype)
        return 0

    lax.fori_loop(0, seq // c, body, 0)


def _chunk_mlp(p_cm, ln_g, ln_b, ws, bs, group_dim):
    b, s, w2 = p_cm.shape
    w = w2 // 2
    return pl.pallas_call(
        functools.partial(_cm_kernel, seq=s, group_dim=group_dim),
        grid=(b,),
        in_specs=[pl.BlockSpec((1, s, w2), lambda i: (i, 0, 0))]
        + [_const_spec(a.shape) for a in (ln_g, ln_b, ws, bs)],
        out_specs=pl.BlockSpec((1, s, w), lambda i: (i, 0, 0)),
        out_shape=jax.ShapeDtypeStruct((b, s, w), BF16),
        compiler_params=_cparams("parallel"),
        name="chunk_gmlp",
    )(p_cm, ln_g, ln_b, ws, bs)


def _outproj_kernel(x_ref, g_ref, ml_ref, mc_ref, a_ref, b_ref, c_ref, wa_ref, wb_ref, wc_ref,
                    o_ref, *, ctx_len):
    y = jnp.dot(a_ref[0], wa_ref[...], preferred_element_type=F32)
    y = y + jnp.dot(b_ref[0], wb_ref[...], preferred_element_type=F32)
    y = y + jnp.dot(c_ref[0], wc_ref[...], preferred_element_type=F32)
    _postnorm_residual(x_ref, y, g_ref, ml_ref, mc_ref, o_ref, ctx_len, 2)


def _outproj(xs, g, mod, parts, weights, ctx_len):
    b, s, d = xs.shape
    rb = ROW_BLOCK
    x_spec, g_spec, ml_spec, mc_spec = _row_specs(rb, d, b)
    return pl.pallas_call(
        functools.partial(_outproj_kernel, ctx_len=ctx_len),
        grid=(b, s // rb),
        in_specs=[x_spec, g_spec, ml_spec, mc_spec]
        + [pl.BlockSpec((1, rb, a.shape[2]), lambda b_, j: (b_, j, 0)) for a in parts]
        + [_const_spec(w.shape) for w in weights],
        out_specs=x_spec,
        out_shape=jax.ShapeDtypeStruct(xs.shape, F32),
        input_output_aliases={0: 0},
        compiler_params=_cparams("parallel", "parallel"),
        name="mixer_outproj",
    )(xs, g, mod, mod, *parts, *weights)


def _swiglu_rows(h, w1_ref, w3_ref, w2_ref):
    f = w1_ref.shape[1]
    acc = None
    for c0 in range(0, f, FF_CHUNK):
        a = jnp.dot(h, w1_ref[:, c0:c0 + FF_CHUNK], preferred_element_type=F32)
        g = jnp.dot(h, w3_ref[:, c0:c0 + FF_CHUNK], preferred_element_type=F32)
        act = (_silu(a) * g).astype(BF16)
        part = jnp.dot(act, w2_ref[c0:c0 + FF_CHUNK, :], preferred_element_type=F32)
        acc = part if acc is None else acc + part
    return acc


def _ffn_kernel(x_ref, gpre_ref, gpost_ref, ml_ref, mc_ref, w1_ref, w3_ref, w2_ref, o_ref, h_ref,
                *, ctx_len):
    _prenorm_modulate(x_ref, gpre_ref, ml_ref, mc_ref, h_ref, ctx_len, 3)
    y = _swiglu_rows(h_ref[...], w1_ref, w3_ref, w2_ref)
    _postnorm_residual(x_ref, y, gpost_ref, ml_ref, mc_ref, o_ref, ctx_len, 5)


def _dense_ffn(xs, g_pre, g_post, mod, w1, w3, w2, ctx_len):
    b, s, d = xs.shape
    rb = ROW_BLOCK
    x_spec, g_spec, ml_spec, mc_spec = _row_specs(rb, d, b)
    return pl.pallas_call(
        functools.partial(_ffn_kernel, ctx_len=ctx_len),
        grid=(b, s // rb),
        in_specs=[x_spec, g_spec, g_spec, ml_spec, mc_spec]
        + [_const_spec(w.shape) for w in (w1, w3, w2)],
        out_specs=x_spec,
        out_shape=jax.ShapeDtypeStruct(xs.shape, F32),
        scratch_shapes=[pltpu.VMEM((rb, d), BF16)],
        input_output_aliases={0: 0},
        compiler_params=_cparams("parallel", "parallel"),
        name="dense_swiglu",
    )(xs, g_pre, g_post, mod, mod, w1, w3, w2)


def _router_kernel(x_ref, g_ref, ml_ref, mc_ref, wr_ref, h_ref, idx_ref, wgt_ref, hf_ref,
                   *, ctx_len, n_experts):
    _prenorm_modulate(x_ref, g_ref, ml_ref, mc_ref, hf_ref, ctx_len, 3)
    h = hf_ref[...]
    h_ref[0] = h.astype(h_ref.dtype)
    logits = jnp.dot(h, wr_ref[...], preferred_element_type=F32)
    lane = lax.broadcasted_iota(jnp.int32, logits.shape, 1).astype(F32)
    neg_inf = jnp.float32(-jnp.inf)
    logits = jnp.where(lane < n_experts, logits, neg_inf)
    big = jnp.float32(V7X_LANES)
    m1 = jnp.max(logits, axis=-1, keepdims=True)
    i1 = jnp.min(jnp.where(logits == m1, lane, big), axis=-1, keepdims=True)
    rest = jnp.where(lane == i1, neg_inf, logits)
    m2 = jnp.max(rest, axis=-1, keepdims=True)
    i2 = jnp.min(jnp.where(rest == m2, lane, big), axis=-1, keepdims=True)
    e2 = jnp.exp(m2 - m1)
    w1 = 1.0 / (1.0 + e2)
    w2 = e2 / (1.0 + e2)
    out_lane = lax.broadcasted_iota(jnp.int32, idx_ref.shape[1:], 1)
    idx_ref[0] = jnp.where(out_lane == 0, i1, i2).astype(jnp.int32)
    wgt_ref[0] = jnp.where(out_lane == 0, w1, w2)


def _router(xs, g, mod, wr_pad, ctx_len, n_experts):
    b, s, d = xs.shape
    rb = ROW_BLOCK
    x_spec, g_spec, ml_spec, mc_spec = _row_specs(rb, d, b)
    small = pl.BlockSpec((1, rb, V7X_SUBLANES), lambda b_, j: (b_, j, 0))
    return pl.pallas_call(
        functools.partial(_router_kernel, ctx_len=ctx_len, n_experts=n_experts),
        grid=(b, s // rb),
        in_specs=[x_spec, g_spec, ml_spec, mc_spec, _const_spec(wr_pad.shape)],
        out_specs=[x_spec, small, small],
        out_shape=[jax.ShapeDtypeStruct((b, s, d), BF16),
                   jax.ShapeDtypeStruct((b, s, V7X_SUBLANES), jnp.int32),
                   jax.ShapeDtypeStruct((b, s, V7X_SUBLANES), F32)],
        scratch_shapes=[pltpu.VMEM((rb, d), F32)],
        compiler_params=_cparams("parallel", "parallel"),
        name="moe_router",
    )(xs, g, mod, mod, wr_pad)


def _grouped_kernel(te_ref, tv_ref, h_ref, w1_ref, w3_ref, w2_ref, o_ref):
    i = pl.program_id(0)

    @pl.when(tv_ref[i] != 0)
    def _():
        o_ref[...] = _swiglu_rows(h_ref[...], w1_ref, w3_ref, w2_ref).astype(o_ref.dtype)

    @pl.when(tv_ref[i] == 0)
    def _():
        o_ref[...] = jnp.zeros_like(o_ref)


def _grouped_swiglu(tile_expert, tile_valid, hs, w1, w3, w2):
    p, d = hs.shape
    f = w1.shape[2]
    tm = MOE_TILE
    grid_spec = pltpu.PrefetchScalarGridSpec(
        num_scalar_prefetch=2,
        grid=(p // tm,),
        in_specs=[
            pl.BlockSpec((tm, d), lambda i, te, tv: (i, 0)),
            pl.BlockSpec((None, d, f), lambda i, te, tv: (te[i], 0, 0)),
            pl.BlockSpec((None, d, f), lambda i, te, tv: (te[i], 0, 0)),
            pl.BlockSpec((None, f, d), lambda i, te, tv: (te[i], 0, 0)),
        ],
        out_specs=pl.BlockSpec((tm, d), lambda i, te, tv: (i, 0)),
    )
    return pl.pallas_call(
        _grouped_kernel,
        grid_spec=grid_spec,
        out_shape=jax.ShapeDtypeStruct((p, d), BF16),
        compiler_params=_cparams("arbitrary"),
        name="moe_grouped_swiglu",
    )(tile_expert, tile_valid, hs, w1, w3, w2)


def _combine_kernel(x_ref, g_ref, ml_ref, mc_ref, y_ref, wgt_ref, o_ref, *, ctx_len):
    wgt = wgt_ref[0]
    d = x_ref.shape[2]
    y = wgt[:, 0:1] * y_ref[0, :, 0:d].astype(F32) + wgt[:, 1:2] * y_ref[0, :, d:2 * d].astype(F32)
    _postnorm_residual(x_ref, y, g_ref, ml_ref, mc_ref, o_ref, ctx_len, 5)


def _combine(xs, g, mod, ys2, wgt, ctx_len):
    b, s, d = xs.shape
    rb = ROW_BLOCK
    x_spec, g_spec, ml_spec, mc_spec = _row_specs(rb, d, b)
    return pl.pallas_call(
        functools.partial(_combine_kernel, ctx_len=ctx_len),
        grid=(b, s // rb),
        in_specs=[x_spec, g_spec, ml_spec, mc_spec,
                  pl.BlockSpec((1, rb, TOP_K * d), lambda b_, j: (b_, j, 0)),
                  pl.BlockSpec((1, rb, V7X_SUBLANES), lambda b_, j: (b_, j, 0))],
        out_specs=x_spec,
        out_shape=jax.ShapeDtypeStruct(xs.shape, F32),
        input_output_aliases={0: 0},
        compiler_params=_cparams("parallel", "parallel"),
        name="moe_combine",
    )(xs, g, mod, mod, ys2, wgt)


def _gather_rows(src, idx):
    return jnp.take(src, idx, axis=0)


def _moe_ffn(xs, g_pre, g_post, mod, router_w, w1, w3, w2, ctx_len):
    b, s, d = xs.shape
    n_exp = router_w.shape[1]
    t = b * s
    tm = MOE_TILE
    wr_pad = jnp.zeros((d, V7X_LANES), F32).at[:, :n_exp].set(router_w)
    h, idx, wgt = _router(xs, g_pre, mod, wr_pad, ctx_len, n_exp)

    e_flat = idx[:, :, :TOP_K].reshape(t * TOP_K)
    onehot = (e_flat[:, None] == jnp.arange(n_exp, dtype=jnp.int32)[None, :]).astype(jnp.int32)
    csum = jnp.cumsum(onehot, axis=0)
    counts = csum[-1]
    rank = jnp.sum((csum - onehot) * onehot, axis=1)
    padded = ((counts + tm - 1) // tm) * tm
    ends = jnp.cumsum(padded)
    starts = ends - padded
    pos = starts[e_flat] + rank
    n_rows = t * TOP_K + n_exp * tm
    n_tiles = n_rows // tm
    src_tok = jnp.zeros((n_rows,), jnp.int32).at[pos].set(
        jnp.arange(t * TOP_K, dtype=jnp.int32) // TOP_K)
    tile_start = jnp.arange(n_tiles, dtype=jnp.int32) * tm
    tile_expert = jnp.minimum(jnp.searchsorted(ends, tile_start, side="right"), n_exp - 1).astype(jnp.int32)
    tile_valid = (tile_start < ends[-1]).astype(jnp.int32)

    hs = _gather_rows(h.reshape(t, d), src_tok)
    ys = _grouped_swiglu(tile_expert, tile_valid, hs, w1, w3, w2)
    ys2 = _gather_rows(ys, pos).reshape(b, s, TOP_K * d)
    return _combine(xs, g_post, mod, ys2, wgt, ctx_len)


def _block_diag(wh):
    h, d, _ = wh.shape
    eye = jnp.eye(h, dtype=wh.dtype)
    return jnp.einsum("hij,hg->higj", wh, eye).reshape(h * d, h * d)


def kernel(x, c, ctx, c_ctx, w_mod, b_mod, g_mix_pre, g_mix_post, g_ffn_pre, g_ffn_post, w_in, w_out,
           lru_conv_w, lru_conv_b, lru_wa, lru_ba, lru_wx, lru_bx, lru_lam, ret_theta,
           cm_ln_g, cm_ln_b, cm_ws, cm_bs, ffn_w1, ffn_w3, ffn_w2, router_w, moe_w1, moe_w3, moe_w2):
    b, l, d = x.shape
    lc = ctx.shape[1]
    s = lc + l
    depth = w_mod.shape[0]
    lru_w = lru_conv_w.shape[2]
    cm_w = cm_ln_g.shape[1] * cm_ln_g.shape[2]
    ret_w = (w_in.shape[2] - 2 * lru_w - 2 * cm_w) // 4
    head_dim = ret_w // RET_HEADS
    group_dim = cm_ln_g.shape[2]
    assert s % ROW_BLOCK == 0 and lc < ROW_BLOCK and lc % RET_CHUNK == 0 and l % RET_CHUNK == 0
    assert ret_w % V7X_LANES == 0 and cm_w % V7X_LANES == 0 and 2 * head_dim == V7X_LANES

    n_rows = l // GRID_W
    rows = jnp.repeat(jnp.arange(n_rows, dtype=F32), GRID_W)
    cols = jnp.tile(jnp.arange(GRID_W, dtype=F32), n_rows)
    pairs = head_dim // 4
    freqs = ROPE_BASE ** (-jnp.arange(pairs, dtype=F32) / pairs)
    ang = jnp.concatenate([rows[:, None] * freqs, cols[:, None] * freqs], axis=-1)
    cos = jnp.tile(jnp.repeat(jnp.cos(ang), 2, axis=1), (1, V7X_LANES // head_dim))
    sin = jnp.repeat(jnp.sin(ang), 2, axis=1) * jnp.tile(jnp.array([-1.0, 1.0], F32), head_dim // 2)
    sin_signed = jnp.tile(sin, (1, V7X_LANES // head_dim))

    xs = jnp.concatenate([ctx, x], axis=1)
    n_cond = ((b + 1 + V7X_SUBLANES - 1) // V7X_SUBLANES) * V7X_SUBLANES
    cpad = jnp.zeros((n_cond, d), F32).at[:b].set(c).at[b].set(c_ctx)
    mod_all = _modulation(cpad, w_mod, b_mod).reshape(depth, n_cond, N_MOD, d)

    for layer in range(depth):
        mod = mod_all[layer]
        wi = w_in[layer].astype(BF16)
        w_lru, w_ret, w_cm = wi[:, :2 * lru_w], wi[:, 2 * lru_w:2 * lru_w + 4 * ret_w], wi[:, 2 * lru_w + 4 * ret_w:]
        p_lru, p_ret, p_cm = _inproj(xs, g_mix_pre[layer][None], mod, w_lru, w_ret, w_cm, lc)

        wg = jnp.concatenate([_block_diag(lru_wa[layer, 0]), _block_diag(lru_wx[layer, 0]),
                              _block_diag(lru_wa[layer, 1]), _block_diag(lru_wx[layer, 1])], axis=1).astype(BF16)
        bg = jnp.concatenate([lru_ba[layer, 0].reshape(-1), lru_bx[layer, 0].reshape(-1),
                              lru_ba[layer, 1].reshape(-1), lru_bx[layer, 1].reshape(-1)])[None]
        o_lru = _lru(p_lru, lru_conv_w[layer], lru_conv_b[layer][None], wg, bg,
                     lru_lam[layer].reshape(2, lru_w), lc)

        lg = jax.nn.log_sigmoid(ret_theta[layer].astype(F32))
        o_ret = _retention(p_ret, lg, jnp.repeat(lg, head_dim, axis=1), cos, sin_signed, lc, head_dim)

        o_cm = _chunk_mlp(p_cm, cm_ln_g[layer].reshape(1, cm_w), cm_ln_b[layer].reshape(1, cm_w),
                          cm_ws[layer].reshape(CM_GROUPS * CM_CHUNK, CM_CHUNK).astype(BF16),
                          jnp.repeat(cm_bs[layer].T, group_dim, axis=1), group_dim)

        wo = w_out[layer].astype(BF16)
        xs = _outproj(xs, g_mix_post[layer][None], mod, (o_lru, o_ret, o_cm),
                      (wo[:lru_w], wo[lru_w:lru_w + ret_w], wo[lru_w + ret_w:]), lc)

        j = layer // 2
        if layer % 2 == 0:
            xs = _dense_ffn(xs, g_ffn_pre[layer][None], g_ffn_post[layer][None], mod,
                            ffn_w1[j].astype(BF16), ffn_w3[j].astype(BF16), ffn_w2[j].astype(BF16), lc)
        else:
            xs = _moe_ffn(xs, g_ffn_pre[layer][None], g_ffn_post[layer][None], mod, router_w[j],
                          moe_w1[j].astype(BF16), moe_w3[j].astype(BF16), moe_w2[j].astype(BF16), lc)
    return xs[:, lc:, :]
```

```python
import functools

import jax
import jax.numpy as jnp
from jax import lax
from jax.experimental import pallas as pl
from jax.experimental.pallas import tpu as pltpu

F32 = jnp.float32
BF16 = jnp.bfloat16

EPS = 1e-6
N_MOD = 6
LRU_HEADS = 6
LRU_CONV = 4
LRU_C = 8.0
RET_HEADS = 6
ROPE_BASE = 100.0
GRID_W = 64
CM_GROUPS = 4
CM_CHUNK = 128
TOP_K = 2

V7X_LANES = 128
V7X_SUBLANES = 8
V7X_VMEM_LIMIT = 56 * 1024 * 1024

ROW_BLOCK = 768
RET_CHUNK = 256
FF_CHUNK = 256
MOE_TILE = 512
LOG2_E = 1.4426950408889634
TINY = 1e-30


def _cparams(*sem):
    return pltpu.CompilerParams(dimension_semantics=sem, vmem_limit_bytes=V7X_VMEM_LIMIT)


def _const_spec(shape):
    nd = len(shape)
    return pl.BlockSpec(shape, lambda *_: (0,) * nd, pipeline_mode=pl.Buffered(1))


def _rms(x):
    return x * lax.rsqrt(jnp.mean(x * x, axis=-1, keepdims=True) + EPS)


def _gelu_tanh(x):
    return 0.5 * x * (1.0 + jnp.tanh(0.7978845608028654 * (x + 0.044715 * (x * x * x))))


def _sigmoid(x):
    return 1.0 / (1.0 + jnp.exp(-x))


def _silu(x):
    return x * _sigmoid(x)


def _mod_kernel(c_ref, w_ref, b_ref, o_ref):
    s = _silu(c_ref[...])
    o_ref[...] = jnp.dot(s, w_ref[...], preferred_element_type=F32) + b_ref[...]


def _modulation(cpad, w_mod, b_mod):
    depth, d, nd = w_mod.shape
    r = cpad.shape[0]
    return pl.pallas_call(
        _mod_kernel,
        grid=(depth, nd // d),
        in_specs=[
            pl.BlockSpec((r, d), lambda l, n: (0, 0)),
            pl.BlockSpec((None, d, d), lambda l, n: (l, 0, n)),
            pl.BlockSpec((None, 1, d), lambda l, n: (l, 0, n)),
        ],
        out_specs=pl.BlockSpec((None, r, d), lambda l, n: (l, 0, n)),
        out_shape=jax.ShapeDtypeStruct((depth, r, nd), F32),
        compiler_params=_cparams("parallel", "parallel"),
        name="modulation",
    )(cpad, w_mod, b_mod.reshape(depth, 1, nd))


def _row_mod(ml_ref, mc_ref, idx, first_block):
    lat = ml_ref[0, idx:idx + 1, :]
    top = jnp.where(first_block, mc_ref[0, idx:idx + 1, :], lat)
    return top, lat


def _prenorm_modulate(x_ref, g_ref, ml_ref, mc_ref, h_ref, ctx_len, shift_idx):
    first = pl.program_id(1) == 0
    sh_top, sh_lat = _row_mod(ml_ref, mc_ref, shift_idx, first)
    sc_top, sc_lat = _row_mod(ml_ref, mc_ref, shift_idx + 1, first)
    y = _rms(x_ref[0]) * g_ref[...]
    h_ref[:ctx_len, :] = (y[:ctx_len] * (1.0 + sc_top) + sh_top).astype(h_ref.dtype)
    h_ref[ctx_len:, :] = (y[ctx_len:] * (1.0 + sc_lat) + sh_lat).astype(h_ref.dtype)


def _postnorm_residual(x_ref, y, g_ref, ml_ref, mc_ref, o_ref, ctx_len, gate_idx):
    first = pl.program_id(1) == 0
    gt_top, gt_lat = _row_mod(ml_ref, mc_ref, gate_idx, first)
    r = _rms(y) * g_ref[...]
    x = x_ref[0]
    o_ref[0, :ctx_len, :] = x[:ctx_len] + gt_top * r[:ctx_len]
    o_ref[0, ctx_len:, :] = x[ctx_len:] + gt_lat * r[ctx_len:]


def _row_specs(rb, d, batch):
    x_spec = pl.BlockSpec((1, rb, d), lambda b, j: (b, j, 0))
    g_spec = _const_spec((1, d))
    ml_spec = pl.BlockSpec((1, N_MOD, d), lambda b, j: (b, 0, 0))
    mc_spec = pl.BlockSpec((1, N_MOD, d), lambda b, j: (batch, 0, 0))
    return x_spec, g_spec, ml_spec, mc_spec


def _inproj_kernel(x_ref, g_ref, ml_ref, mc_ref, wl_ref, wr_ref, wc_ref,
                   ol_ref, or_ref, oc_ref, h_ref, *, ctx_len):
    _prenorm_modulate(x_ref, g_ref, ml_ref, mc_ref, h_ref, ctx_len, 0)
    h = h_ref[...]
    ol_ref[0] = jnp.dot(h, wl_ref[...], preferred_element_type=F32).astype(ol_ref.dtype)
    or_ref[0] = jnp.dot(h, wr_ref[...], preferred_element_type=F32).astype(or_ref.dtype)
    oc_ref[0] = jnp.dot(h, wc_ref[...], preferred_element_type=F32).astype(oc_ref.dtype)


def _inproj(xs, g, mod, w_lru, w_ret, w_cm, ctx_len):
    b, s, d = xs.shape
    rb = ROW_BLOCK
    x_spec, g_spec, ml_spec, mc_spec = _row_specs(rb, d, b)
    outs = [w.shape[1] for w in (w_lru, w_ret, w_cm)]
    return pl.pallas_call(
        functools.partial(_inproj_kernel, ctx_len=ctx_len),
        grid=(b, s // rb),
        in_specs=[x_spec, g_spec, ml_spec, mc_spec] + [_const_spec(w.shape) for w in (w_lru, w_ret, w_cm)],
        out_specs=[pl.BlockSpec((1, rb, n), lambda b_, j: (b_, j, 0)) for n in outs],
        out_shape=[jax.ShapeDtypeStruct((b, s, n), BF16) for n in outs],
        scratch_shapes=[pltpu.VMEM((rb, d), BF16)],
        compiler_params=_cparams("parallel", "parallel"),
        name="mixer_inproj",
    )(xs, g, mod, mod, w_lru, w_ret, w_cm)


def _tile_scan(a, u, carry, row, reverse):
    n = V7X_SUBLANES
    for dist in (1, 2, 4):
        if reverse:
            keep = row < (n - dist)
            shift = n - dist
        else:
            keep = row >= dist
            shift = dist
        a_s = jnp.where(keep, pltpu.roll(a, shift, 0), 1.0)
        u_s = jnp.where(keep, pltpu.roll(u, shift, 0), 0.0)
        u = a * u_s + u
        a = a * a_s
    h = a * carry + u
    new_carry = h[0:1] if reverse else h[n - 1:n]
    return h, new_carry


def _lru_kernel(p_ref, cw_ref, cb_ref, wg_ref, bg_ref, lam_ref, o_ref,
                xpad_ref, a_ref, u_ref, h_ref, *, ctx_len, seq):
    w = cw_ref.shape[1]
    s = seq
    pad = V7X_SUBLANES
    chunk = ctx_len
    zeros = jnp.zeros((pad, w), F32)
    xpad_ref[0:pad, :] = zeros
    xpad_ref[pad + ctx_len:2 * pad + ctx_len, :] = zeros
    xpad_ref[2 * pad + s:3 * pad + s, :] = zeros
    xpad_ref[pad:pad + ctx_len, :] = p_ref[0, 0:ctx_len, 0:w].astype(F32)
    xpad_ref[2 * pad + ctx_len:2 * pad + s, :] = p_ref[0, ctx_len:s, 0:w].astype(F32)

    neg = -lam_ref[...]
    softplus = jnp.maximum(neg, 0.0) + jnp.log(1.0 + jnp.exp(-jnp.abs(neg)))
    c2 = (-0.5 * LRU_C * LOG2_E) * softplus

    lanes = V7X_LANES
    n_slabs = w // lanes
    pitch = a_ref.shape[2] // V7X_SUBLANES
    lat_len = s - ctx_len

    def phys(d, r0):
        if d == 0:
            return r0
        return lat_len + r0 if r0 < ctx_len else r0 - ctx_len

    for d in range(2):
        for j in range(n_slabs):
            a_ref[d, j, s:, :] = jnp.ones((a_ref.shape[2] - s, lanes), F32)
            u_ref[d, j, s:, :] = jnp.zeros((a_ref.shape[2] - s, lanes), F32)

    for c in range(s // chunk):
        r0 = c * chunk
        base = r0 + (pad if r0 < ctx_len else 2 * pad)
        xc = jnp.zeros((chunk, w), F32) + cb_ref[...]
        for k in range(LRU_CONV):
            off = k - LRU_CONV // 2
            xc = xc + xpad_ref[base + off:base + off + chunk, :] * cw_ref[k:k + 1, :]
        th = jnp.tanh(jnp.dot(xc.astype(BF16), wg_ref[...], preferred_element_type=F32) + bg_ref[...])
        hx = 0.5 * xc
        for d in range(2):
            th_r = th[:, (2 * d) * w:(2 * d + 1) * w]
            th_i = th[:, (2 * d + 1) * w:(2 * d + 2) * w]
            cd = c2[d:d + 1, :]
            a = jnp.exp2(cd * th_r + cd)
            y = 1.0 - a * a
            u = (y * lax.rsqrt(jnp.maximum(y, TINY))) * (hx * th_i + hx)
            pr = phys(d, r0)
            for j in range(n_slabs):
                a_ref[d, j, pr:pr + chunk, :] = a[:, j * lanes:(j + 1) * lanes]
                u_ref[d, j, pr:pr + chunk, :] = u[:, j * lanes:(j + 1) * lanes]

    row = lax.broadcasted_iota(jnp.int32, (V7X_SUBLANES, lanes), 0)
    chains = [(d, j) for d in range(2) for j in range(n_slabs)]

    def seg_rows(d, t):
        tt = t if d == 0 else pitch - 1 - t
        return pl.ds(tt, V7X_SUBLANES, stride=pitch)

    def pass1(t, carry):
        out = []
        for (d, j), (h, pprod) in zip(chains, carry):
            a = a_ref[d, j, seg_rows(d, t), :]
            u = u_ref[d, j, seg_rows(d, t), :]
            out.append((a * h + u, a * pprod))
        return tuple(out)

    zero = jnp.zeros((V7X_SUBLANES, lanes), F32)
    ends = lax.fori_loop(0, pitch, pass1, tuple((zero, zero + 1.0) for _ in chains), unroll=4)

    starts = []
    for (d, j), (h_end, p_tot) in zip(chains, ends):
        incl, _ = _tile_scan(p_tot, h_end, jnp.zeros((1, lanes), F32), row, d == 1)
        if d == 0:
            starts.append(jnp.where(row >= 1, pltpu.roll(incl, 1, 0), 0.0))
        else:
            starts.append(jnp.where(row < V7X_SUBLANES - 1, pltpu.roll(incl, V7X_SUBLANES - 1, 0), 0.0))

    def pass2(t, carry):
        out = []
        for (d, j), h in zip(chains, carry):
            a = a_ref[d, j, seg_rows(d, t), :]
            u = u_ref[d, j, seg_rows(d, t), :]
            h = a * h + u
            h_ref[d, j, seg_rows(d, t), :] = h
            out.append(h)
        return tuple(out)

    lax.fori_loop(0, pitch, pass2, tuple(starts), unroll=4)

    for c in range(s // chunk):
        r0 = c * chunk
        rb = phys(1, r0)
        for j in range(n_slabs):
            ly = p_ref[0, r0:r0 + chunk, w + j * lanes:w + (j + 1) * lanes].astype(F32)
            h = h_ref[0, j, r0:r0 + chunk, :] + h_ref[1, j, rb:rb + chunk, :]
            o_ref[0, r0:r0 + chunk, j * lanes:(j + 1) * lanes] = (_gelu_tanh(ly) * h).astype(o_ref.dtype)


def _lru(p_lru, conv_w, conv_b, wg, bg, lam, ctx_len):
    b, s, w2 = p_lru.shape
    w = w2 // 2
    assert s % (V7X_SUBLANES * V7X_SUBLANES) == 0
    scan_rows = V7X_SUBLANES * (s // V7X_SUBLANES + 4)
    return pl.pallas_call(
        functools.partial(_lru_kernel, ctx_len=ctx_len, seq=s),
        grid=(b,),
        in_specs=[
            pl.BlockSpec((1, s, w2), lambda i: (i, 0, 0)),
            _const_spec(conv_w.shape), _const_spec(conv_b.shape),
            _const_spec(wg.shape), _const_spec(bg.shape), _const_spec(lam.shape),
        ],
        out_specs=pl.BlockSpec((1, s, w), lambda i: (i, 0, 0)),
        out_shape=jax.ShapeDtypeStruct((b, s, w), BF16),
        scratch_shapes=[
            pltpu.VMEM((s + 3 * V7X_SUBLANES, w), F32),
            pltpu.VMEM((2, w // V7X_LANES, scan_rows, V7X_LANES), F32),
            pltpu.VMEM((2, w // V7X_LANES, scan_rows, V7X_LANES), F32),
            pltpu.VMEM((2, w // V7X_LANES, scan_rows, V7X_LANES), F32),
        ],
        compiler_params=_cparams("parallel"),
        name="rglru",
    )(p_lru, conv_w, conv_b, wg, bg, lam)


def _rope(x, cos, sin_signed, lane_even):
    partner = jnp.where(lane_even, pltpu.roll(x, V7X_LANES - 1, 1), pltpu.roll(x, 1, 1))
    return x * cos + partner * sin_signed


def _ret_kernel(lg_ref, p_ref, cos_ref, sin_ref, lgl_ref, o_ref,
                q_ref, k_ref, dlt_ref, st_ref, *, ctx_len, seq, head_dim):
    s = seq
    c = RET_CHUNK
    lanes = V7X_LANES
    w = o_ref.shape[2]
    n_pairs = w // lanes
    n_chunks = s // c
    n_ctx = ctx_len // c
    scale = head_dim ** -0.5

    rowi = lax.broadcasted_iota(jnp.int32, (c, lanes), 0).astype(F32)
    lane = lax.broadcasted_iota(jnp.int32, (c, lanes), 1)
    lane_even = (lane % 2) == 0
    low_half = lane < head_dim
    dif = (lax.broadcasted_iota(jnp.int32, (c, c), 0)
           - lax.broadcasted_iota(jnp.int32, (c, c), 1)).astype(F32)
    blk_row_low = lax.broadcasted_iota(jnp.int32, (lanes, lanes), 0) < head_dim
    blk_col_low = lax.broadcasted_iota(jnp.int32, (lanes, lanes), 1) < head_dim
    blk_diag = blk_row_low == blk_col_low

    order_b = list(range(n_ctx - 1, -1, -1)) + list(range(n_chunks - 1, n_ctx - 1, -1))

    for p in range(n_pairs):
        lsl = slice(p * lanes, (p + 1) * lanes)
        lg_f = lgl_ref[0:1, lsl]
        lg_b = lgl_ref[1:2, lsl]
        kd_f = jnp.exp(lg_f * (c - 1.0 - rowi))
        kd_b = jnp.exp(lg_b * rowi)
        qd_f = jnp.exp(lg_f * (rowi + 1.0))
        qd_b = jnp.exp(lg_b * (c - rowi))

        for ch in range(n_chunks):
            r0 = ch * c
            q = p_ref[0, r0:r0 + c, p * lanes:(p + 1) * lanes].astype(F32)
            k = p_ref[0, r0:r0 + c, w + p * lanes:w + (p + 1) * lanes].astype(F32)
            v = p_ref[0, r0:r0 + c, 2 * w + p * lanes:2 * w + (p + 1) * lanes]
            if ch >= n_ctx:
                t0 = r0 - ctx_len
                cos = cos_ref[t0:t0 + c, :]
                sin = sin_ref[t0:t0 + c, :]
                q = _rope(q, cos, sin, lane_even)
                k = _rope(k, cos, sin, lane_even)
            q_ref[r0:r0 + c, lsl] = (q * scale).astype(BF16)
            k_ref[r0:r0 + c, lsl] = k.astype(BF16)
            d_f = pl.dot((k * kd_f).astype(BF16), v, trans_a=True)
            d_b = pl.dot((k * kd_b).astype(BF16), v, trans_a=True)
            dlt_ref[0, ch] = jnp.where(blk_diag, d_f, 0.0)
            dlt_ref[1, ch] = jnp.where(blk_diag, d_b, 0.0)

        gc_f = jnp.exp(jnp.where(blk_row_low, lg_ref[0, 2 * p], lg_ref[0, 2 * p + 1]) * float(c))
        gc_b = jnp.exp(jnp.where(blk_row_low, lg_ref[1, 2 * p], lg_ref[1, 2 * p + 1]) * float(c))
        state = jnp.zeros((lanes, lanes), F32)
        for ch in range(n_chunks):
            st_ref[0, ch] = state.astype(BF16)
            state = gc_f * state + dlt_ref[0, ch]
        state = jnp.zeros((lanes, lanes), F32)
        for ch in order_b:
            st_ref[1, ch] = state.astype(BF16)
            state = gc_b * state + dlt_ref[1, ch]

        dmats = []
        for h in range(2):
            lgf_s = lg_ref[0, 2 * p + h]
            lgb_s = lg_ref[1, 2 * p + h]
            dmats.append(jnp.where(dif > 0, jnp.exp(lgf_s * jnp.maximum(dif, 0.0)),
                                   jnp.where(dif < 0, jnp.exp(lgb_s * jnp.maximum(-dif, 0.0)), 2.0)))

        def out_chunk(ch, _):
            r0 = pl.multiple_of(ch * c, c)
            rows = pl.ds(r0, c)
            q = q_ref[rows, lsl]
            k = k_ref[rows, lsl]
            v = p_ref[0, rows, 2 * w + p * lanes:2 * w + (p + 1) * lanes]
            g = p_ref[0, rows, 3 * w + p * lanes:3 * w + (p + 1) * lanes].astype(F32)
            zero = jnp.zeros_like(q)
            s0 = pl.dot(jnp.where(low_half, q, zero), k, trans_b=True)
            s1 = pl.dot(jnp.where(low_half, zero, q), k, trans_b=True)
            o0 = jnp.dot((s0 * dmats[0]).astype(BF16), v, preferred_element_type=F32)
            o1 = jnp.dot((s1 * dmats[1]).astype(BF16), v, preferred_element_type=F32)
            o = jnp.where(low_half, o0, o1)
            qf = q.astype(F32)
            o = o + jnp.dot((qf * qd_f).astype(BF16), st_ref[0, ch], preferred_element_type=F32)
            o = o + jnp.dot((qf * qd_b).astype(BF16), st_ref[1, ch], preferred_element_type=F32)
            oo = o * o
            ss0 = jnp.sum(jnp.where(low_half, oo, 0.0), axis=-1, keepdims=True)
            ss1 = jnp.sum(jnp.where(low_half, 0.0, oo), axis=-1, keepdims=True)
            inv = jnp.where(low_half, lax.rsqrt(ss0 / head_dim + EPS), lax.rsqrt(ss1 / head_dim + EPS))
            o_ref[0, rows, lsl] = (_silu(g) * (o * inv)).astype(o_ref.dtype)
            return 0

        lax.fori_loop(0, n_chunks, out_chunk, 0, unroll=3 if n_chunks % 3 == 0 else 1)


def _retention(p_ret, lg, lg_lane, cos, sin_signed, ctx_len, head_dim):
    b, s, w4 = p_ret.shape
    w = w4 // 4
    n_chunks = s // RET_CHUNK
    return pl.pallas_call(
        functools.partial(_ret_kernel, ctx_len=ctx_len, seq=s, head_dim=head_dim),
        grid=(b,),
        in_specs=[
            pl.BlockSpec(memory_space=pltpu.SMEM),
            pl.BlockSpec((1, s, w4), lambda i: (i, 0, 0)),
            _const_spec(cos.shape), _const_spec(sin_signed.shape), _const_spec(lg_lane.shape),
        ],
        out_specs=pl.BlockSpec((1, s, w), lambda i: (i, 0, 0)),
        out_shape=jax.ShapeDtypeStruct((b, s, w), BF16),
        scratch_shapes=[
            pltpu.VMEM((s, w), BF16),
            pltpu.VMEM((s, w), BF16),
            pltpu.VMEM((2, n_chunks, V7X_LANES, V7X_LANES), F32),
            pltpu.VMEM((2, n_chunks, V7X_LANES, V7X_LANES), BF16),
        ],
        compiler_params=_cparams("parallel"),
        name="retention",
    )(lg, p_ret, cos, sin_signed, lg_lane)


def _cm_kernel(z_ref, lng_ref, lnb_ref, ws_ref, bs_ref, o_ref, *, seq, group_dim):
    c = CM_CHUNK
    lanes = V7X_LANES
    w = o_ref.shape[2]
    n_slabs = w // lanes
    low_half = lax.broadcasted_iota(jnp.int32, (c, lanes), 1) < group_dim
    lane_w = lax.broadcasted_iota(jnp.int32, (c, w), 1)

    def body(ch, _):
        r0 = pl.multiple_of(ch * c, c)
        rows = pl.ds(r0, c)
        z = _gelu_tanh(z_ref[0, rows, :].astype(F32))
        u = z[:, :w]
        vn = []
        for sl in range(n_slabs):
            v = z[:, w + sl * lanes:w + (sl + 1) * lanes]
            m0 = jnp.sum(jnp.where(low_half, v, 0.0), axis=-1, keepdims=True) / group_dim
            m1 = jnp.sum(jnp.where(low_half, 0.0, v), axis=-1, keepdims=True) / group_dim
            xc = v - jnp.where(low_half, m0, m1)
            xx = xc * xc
            v0 = jnp.sum(jnp.where(low_half, xx, 0.0), axis=-1, keepdims=True) / group_dim
            v1 = jnp.sum(jnp.where(low_half, 0.0, xx), axis=-1, keepdims=True) / group_dim
            vn.append(xc * jnp.where(low_half, lax.rsqrt(v0 + EPS), lax.rsqrt(v1 + EPS)))
        vn = jnp.concatenate(vn, axis=-1) * lng_ref[...] + lnb_ref[...]
        s_all = jnp.dot(ws_ref[...], vn.astype(BF16), preferred_element_type=F32)
        sp = s_all[0:c]
        for gi in range(1, CM_GROUPS):
            sp = jnp.where(lane_w >= gi * group_dim, s_all[gi * c:(gi + 1) * c], sp)
        o_ref[0, rows, :] = (u * (sp + bs_ref[...])).astype(o_ref.dtype)
        return 0

    lax.fori_loop(0, seq // c, body, 0, unroll=3 if (seq // c) % 3 == 0 else 1)


def _chunk_mlp(p_cm, ln_g, ln_b, ws, bs, group_dim):
    b, s, w2 = p_cm.shape
    w = w2 // 2
    return pl.pallas_call(
        functools.partial(_cm_kernel, seq=s, group_dim=group_dim),
        grid=(b,),
        in_specs=[pl.BlockSpec((1, s, w2), lambda i: (i, 0, 0))]
        + [_const_spec(a.shape) for a in (ln_g, ln_b, ws, bs)],
        out_specs=pl.BlockSpec((1, s, w), lambda i: (i, 0, 0)),
        out_shape=jax.ShapeDtypeStruct((b, s, w), BF16),
        compiler_params=_cparams("parallel"),
        name="chunk_gmlp",
    )(p_cm, ln_g, ln_b, ws, bs)


def _outproj_kernel(x_ref, g_ref, ml_ref, mc_ref, a_ref, b_ref, c_ref, wa_ref, wb_ref, wc_ref,
                    o_ref, *, ctx_len):
    y = jnp.dot(a_ref[0], wa_ref[...], preferred_element_type=F32)
    y = y + jnp.dot(b_ref[0], wb_ref[...], preferred_element_type=F32)
    y = y + jnp.dot(c_ref[0], wc_ref[...], preferred_element_type=F32)
    _postnorm_residual(x_ref, y, g_ref, ml_ref, mc_ref, o_ref, ctx_len, 2)


def _outproj(xs, g, mod, parts, weights, ctx_len):
    b, s, d = xs.shape
    rb = ROW_BLOCK
    x_spec, g_spec, ml_spec, mc_spec = _row_specs(rb, d, b)
    return pl.pallas_call(
        functools.partial(_outproj_kernel, ctx_len=ctx_len),
        grid=(b, s // rb),
        in_specs=[x_spec, g_spec, ml_spec, mc_spec]
        + [pl.BlockSpec((1, rb, a.shape[2]), lambda b_, j: (b_, j, 0)) for a in parts]
        + [_const_spec(w.shape) for w in weights],
        out_specs=x_spec,
        out_shape=jax.ShapeDtypeStruct(xs.shape, F32),
        input_output_aliases={0: 0},
        compiler_params=_cparams("parallel", "parallel"),
        name="mixer_outproj",
    )(xs, g, mod, mod, *parts, *weights)


def _swiglu_rows(h, w1_ref, w3_ref, w2_ref):
    f = w1_ref.shape[1]
    acc = None
    for c0 in range(0, f, FF_CHUNK):
        a = jnp.dot(h, w1_ref[:, c0:c0 + FF_CHUNK], preferred_element_type=F32)
        g = jnp.dot(h, w3_ref[:, c0:c0 + FF_CHUNK], preferred_element_type=F32)
        act = (_silu(a) * g).astype(BF16)
        part = jnp.dot(act, w2_ref[c0:c0 + FF_CHUNK, :], preferred_element_type=F32)
        acc = part if acc is None else acc + part
    return acc


def _ffn_kernel(x_ref, gpre_ref, gpost_ref, ml_ref, mc_ref, w1_ref, w3_ref, w2_ref, o_ref, h_ref,
                *, ctx_len):
    _prenorm_modulate(x_ref, gpre_ref, ml_ref, mc_ref, h_ref, ctx_len, 3)
    y = _swiglu_rows(h_ref[...], w1_ref, w3_ref, w2_ref)
    _postnorm_residual(x_ref, y, gpost_ref, ml_ref, mc_ref, o_ref, ctx_len, 5)


def _dense_ffn(xs, g_pre, g_post, mod, w1, w3, w2, ctx_len):
    b, s, d = xs.shape
    rb = ROW_BLOCK
    x_spec, g_spec, ml_spec, mc_spec = _row_specs(rb, d, b)
    return pl.pallas_call(
        functools.partial(_ffn_kernel, ctx_len=ctx_len),
        grid=(b, s // rb),
        in_specs=[x_spec, g_spec, g_spec, ml_spec, mc_spec]
        + [_const_spec(w.shape) for w in (w1, w3, w2)],
        out_specs=x_spec,
        out_shape=jax.ShapeDtypeStruct(xs.shape, F32),
        scratch_shapes=[pltpu.VMEM((rb, d), BF16)],
        input_output_aliases={0: 0},
        compiler_params=_cparams("parallel", "parallel"),
        name="dense_swiglu",
    )(xs, g_pre, g_post, mod, mod, w1, w3, w2)


def _router_kernel(x_ref, g_ref, ml_ref, mc_ref, wr_ref, h_ref, idx_ref, wgt_ref, hf_ref,
                   *, ctx_len, n_experts):
    _prenorm_modulate(x_ref, g_ref, ml_ref, mc_ref, hf_ref, ctx_len, 3)
    h = hf_ref[...]
    h_ref[0] = h.astype(h_ref.dtype)
    logits = jnp.dot(h, wr_ref[...], preferred_element_type=F32)
    lane = lax.broadcasted_iota(jnp.int32, logits.shape, 1).astype(F32)
    neg_inf = jnp.float32(-jnp.inf)
    logits = jnp.where(lane < n_experts, logits, neg_inf)
    big = jnp.float32(V7X_LANES)
    m1 = jnp.max(logits, axis=-1, keepdims=True)
    i1 = jnp.min(jnp.where(logits == m1, lane, big), axis=-1, keepdims=True)
    rest = jnp.where(lane == i1, neg_inf, logits)
    m2 = jnp.max(rest, axis=-1, keepdims=True)
    i2 = jnp.min(jnp.where(rest == m2, lane, big), axis=-1, keepdims=True)
    e2 = jnp.exp(m2 - m1)
    w1 = 1.0 / (1.0 + e2)
    w2 = e2 / (1.0 + e2)
    out_lane = lax.broadcasted_iota(jnp.int32, idx_ref.shape[1:], 1)
    idx_ref[0] = jnp.where(out_lane == 0, i1, i2).astype(jnp.int32)
    wgt_ref[0] = jnp.where(out_lane == 0, w1, w2)


def _router(xs, g, mod, wr_pad, ctx_len, n_experts):
    b, s, d = xs.shape
    rb = ROW_BLOCK
    x_spec, g_spec, ml_spec, mc_spec = _row_specs(rb, d, b)
    small = pl.BlockSpec((1, rb, V7X_SUBLANES), lambda b_, j: (b_, j, 0))
    return pl.pallas_call(
        functools.partial(_router_kernel, ctx_len=ctx_len, n_experts=n_experts),
        grid=(b, s // rb),
        in_specs=[x_spec, g_spec, ml_spec, mc_spec, _const_spec(wr_pad.shape)],
        out_specs=[x_spec, small, small],
        out_shape=[jax.ShapeDtypeStruct((b, s, d), BF16),
                   jax.ShapeDtypeStruct((b, s, V7X_SUBLANES), jnp.int32),
                   jax.ShapeDtypeStruct((b, s, V7X_SUBLANES), F32)],
        scratch_shapes=[pltpu.VMEM((rb, d), F32)],
        compiler_params=_cparams("parallel", "parallel"),
        name="moe_router",
    )(xs, g, mod, mod, wr_pad)


def _grouped_kernel(te_ref, tv_ref, h_ref, w1_ref, w3_ref, w2_ref, o_ref):
    i = pl.program_id(0)

    @pl.when(tv_ref[i] != 0)
    def _():
        o_ref[...] = _swiglu_rows(h_ref[...], w1_ref, w3_ref, w2_ref).astype(o_ref.dtype)

    @pl.when(tv_ref[i] == 0)
    def _():
        o_ref[...] = jnp.zeros_like(o_ref)


def _grouped_swiglu(layer_idx, tile_expert, tile_valid, hs, w1, w3, w2):
    p, d = hs.shape
    f = w1.shape[3]
    tm = MOE_TILE
    grid_spec = pltpu.PrefetchScalarGridSpec(
        num_scalar_prefetch=2,
        grid=(p // tm,),
        in_specs=[
            pl.BlockSpec((tm, d), lambda i, te, tv: (i, 0)),
            pl.BlockSpec((None, None, d, f), lambda i, te, tv: (layer_idx, te[i], 0, 0)),
            pl.BlockSpec((None, None, d, f), lambda i, te, tv: (layer_idx, te[i], 0, 0)),
            pl.BlockSpec((None, None, f, d), lambda i, te, tv: (layer_idx, te[i], 0, 0)),
        ],
        out_specs=pl.BlockSpec((tm, d), lambda i, te, tv: (i, 0)),
    )
    return pl.pallas_call(
        _grouped_kernel,
        grid_spec=grid_spec,
        out_shape=jax.ShapeDtypeStruct((p, d), BF16),
        compiler_params=_cparams("arbitrary"),
        name="moe_grouped_swiglu",
    )(tile_expert, tile_valid, hs, w1, w3, w2)


def _combine_kernel(x_ref, g_ref, ml_ref, mc_ref, y0_ref, y1_ref, wgt_ref, o_ref, *, ctx_len):
    wgt = wgt_ref[0]
    y = wgt[:, 0:1] * y0_ref[0].astype(F32) + wgt[:, 1:2] * y1_ref[0].astype(F32)
    _postnorm_residual(x_ref, y, g_ref, ml_ref, mc_ref, o_ref, ctx_len, 5)


def _combine(xs, g, mod, y0, y1, wgt, ctx_len):
    b, s, d = xs.shape
    rb = ROW_BLOCK
    x_spec, g_spec, ml_spec, mc_spec = _row_specs(rb, d, b)
    return pl.pallas_call(
        functools.partial(_combine_kernel, ctx_len=ctx_len),
        grid=(b, s // rb),
        in_specs=[x_spec, g_spec, ml_spec, mc_spec, x_spec, x_spec,
                  pl.BlockSpec((1, rb, V7X_SUBLANES), lambda b_, j: (b_, j, 0))],
        out_specs=x_spec,
        out_shape=jax.ShapeDtypeStruct(xs.shape, F32),
        input_output_aliases={0: 0},
        compiler_params=_cparams("parallel", "parallel"),
        name="moe_combine",
    )(xs, g, mod, mod, y0, y1, wgt)


def _gather_rows(src, idx):
    return src.at[idx].get(mode="promise_in_bounds")


def _moe_ffn(xs, g_pre, g_post, mod, router_w, layer_idx, w1, w3, w2, ctx_len):
    b, s, d = xs.shape
    n_exp = router_w.shape[1]
    t = b * s
    tm = MOE_TILE
    wr_pad = jnp.zeros((d, V7X_LANES), F32).at[:, :n_exp].set(router_w)
    h, idx, wgt = _router(xs, g_pre, mod, wr_pad, ctx_len, n_exp)

    e_flat = idx[:, :, :TOP_K].reshape(t * TOP_K)
    onehot = (e_flat[:, None] == jnp.arange(n_exp, dtype=jnp.int32)[None, :]).astype(jnp.int32)
    csum = jnp.cumsum(onehot, axis=0)
    counts = csum[-1]
    rank = jnp.sum((csum - onehot) * onehot, axis=1)
    padded = ((counts + tm - 1) // tm) * tm
    ends = jnp.cumsum(padded)
    starts = ends - padded
    pos = starts[e_flat] + rank
    n_rows = t * TOP_K + n_exp * tm
    n_tiles = n_rows // tm
    src_tok = jnp.zeros((n_rows,), jnp.int32).at[pos].set(
        jnp.arange(t * TOP_K, dtype=jnp.int32) // TOP_K)
    tile_start = jnp.arange(n_tiles, dtype=jnp.int32) * tm
    tile_expert = jnp.minimum(jnp.sum((tile_start[:, None] >= ends[None, :]).astype(jnp.int32), axis=1),
                              n_exp - 1)
    tile_valid = (tile_start < ends[-1]).astype(jnp.int32)

    hs = _gather_rows(h.reshape(t, d), src_tok)
    ys = _grouped_swiglu(layer_idx, tile_expert, tile_valid, hs, w1, w3, w2)
    pos = pos.reshape(t, TOP_K)
    y0 = _gather_rows(ys, pos[:, 0]).reshape(b, s, d)
    y1 = _gather_rows(ys, pos[:, 1]).reshape(b, s, d)
    return _combine(xs, g_post, mod, y0, y1, wgt, ctx_len)


def _block_diag(wh):
    h, d, _ = wh.shape
    eye = jnp.eye(h, dtype=wh.dtype)
    return jnp.einsum("hij,hg->higj", wh, eye).reshape(h * d, h * d)


def kernel(x, c, ctx, c_ctx, w_mod, b_mod, g_mix_pre, g_mix_post, g_ffn_pre, g_ffn_post, w_in, w_out,
           lru_conv_w, lru_conv_b, lru_wa, lru_ba, lru_wx, lru_bx, lru_lam, ret_theta,
           cm_ln_g, cm_ln_b, cm_ws, cm_bs, ffn_w1, ffn_w3, ffn_w2, router_w, moe_w1, moe_w3, moe_w2):
    b, l, d = x.shape
    lc = ctx.shape[1]
    s = lc + l
    depth = w_mod.shape[0]
    lru_w = lru_conv_w.shape[2]
    cm_w = cm_ln_g.shape[1] * cm_ln_g.shape[2]
    ret_w = (w_in.shape[2] - 2 * lru_w - 2 * cm_w) // 4
    head_dim = ret_w // RET_HEADS
    group_dim = cm_ln_g.shape[2]
    assert s % ROW_BLOCK == 0 and lc < ROW_BLOCK and lc % RET_CHUNK == 0 and l % RET_CHUNK == 0
    assert ret_w % V7X_LANES == 0 and cm_w % V7X_LANES == 0 and 2 * head_dim == V7X_LANES

    n_rows = l // GRID_W
    rows = jnp.repeat(jnp.arange(n_rows, dtype=F32), GRID_W)
    cols = jnp.tile(jnp.arange(GRID_W, dtype=F32), n_rows)
    pairs = head_dim // 4
    freqs = ROPE_BASE ** (-jnp.arange(pairs, dtype=F32) / pairs)
    ang = jnp.concatenate([rows[:, None] * freqs, cols[:, None] * freqs], axis=-1)
    cos = jnp.tile(jnp.repeat(jnp.cos(ang), 2, axis=1), (1, V7X_LANES // head_dim))
    sin = jnp.repeat(jnp.sin(ang), 2, axis=1) * jnp.tile(jnp.array([-1.0, 1.0], F32), head_dim // 2)
    sin_signed = jnp.tile(sin, (1, V7X_LANES // head_dim))

    xs = jnp.concatenate([ctx, x], axis=1)
    n_cond = ((b + 1 + V7X_SUBLANES - 1) // V7X_SUBLANES) * V7X_SUBLANES
    cpad = jnp.zeros((n_cond, d), F32).at[:b].set(c).at[b].set(c_ctx)
    mod_all = _modulation(cpad, w_mod, b_mod).reshape(depth, n_cond, N_MOD, d)

    moe_b1, moe_b3, moe_b2 = moe_w1.astype(BF16), moe_w3.astype(BF16), moe_w2.astype(BF16)

    for layer in range(depth):
        mod = mod_all[layer]
        wi = w_in[layer].astype(BF16)
        w_lru, w_ret, w_cm = wi[:, :2 * lru_w], wi[:, 2 * lru_w:2 * lru_w + 4 * ret_w], wi[:, 2 * lru_w + 4 * ret_w:]
        p_lru, p_ret, p_cm = _inproj(xs, g_mix_pre[layer][None], mod, w_lru, w_ret, w_cm, lc)

        wg = (0.5 * jnp.concatenate([_block_diag(lru_wa[layer, 0]), _block_diag(lru_wx[layer, 0]),
                                     _block_diag(lru_wa[layer, 1]), _block_diag(lru_wx[layer, 1])],
                                    axis=1)).astype(BF16)
        bg = 0.5 * jnp.concatenate([lru_ba[layer, 0].reshape(-1), lru_bx[layer, 0].reshape(-1),
                                    lru_ba[layer, 1].reshape(-1), lru_bx[layer, 1].reshape(-1)])[None]
        o_lru = _lru(p_lru, lru_conv_w[layer], lru_conv_b[layer][None], wg, bg,
                     lru_lam[layer].reshape(2, lru_w), lc)

        lg = jax.nn.log_sigmoid(ret_theta[layer].astype(F32))
        o_ret = _retention(p_ret, lg, jnp.repeat(lg, head_dim, axis=1), cos, sin_signed, lc, head_dim)

        o_cm = _chunk_mlp(p_cm, cm_ln_g[layer].reshape(1, cm_w), cm_ln_b[layer].reshape(1, cm_w),
                          cm_ws[layer].reshape(CM_GROUPS * CM_CHUNK, CM_CHUNK).astype(BF16),
                          jnp.repeat(cm_bs[layer].T, group_dim, axis=1), group_dim)

        wo = w_out[layer].astype(BF16)
        xs = _outproj(xs, g_mix_post[layer][None], mod, (o_lru, o_ret, o_cm),
                      (wo[:lru_w], wo[lru_w:lru_w + ret_w], wo[lru_w + ret_w:]), lc)

        j = layer // 2
        if layer % 2 == 0:
            xs = _dense_ffn(xs, g_ffn_pre[layer][None], g_ffn_post[layer][None], mod,
                            ffn_w1[j].astype(BF16), ffn_w3[j].astype(BF16), ffn_w2[j].astype(BF16), lc)
        else:
            xs = _moe_ffn(xs, g_ffn_pre[layer][None], g_ffn_post[layer][None], mod, router_w[j],
                          j, moe_b1, moe_b3, moe_b2, lc)
    return xs[:, lc:, :]
```

```python
import functools

import jax
import jax.numpy as jnp
from jax import lax
from jax.experimental import pallas as pl
from jax.experimental.pallas import tpu as pltpu

F32 = jnp.float32
BF16 = jnp.bfloat16

EPS = 1e-6
N_MOD = 6
LRU_CONV = 4
LRU_C = 8.0
RET_HEADS = 6
ROPE_BASE = 100.0
GRID_W = 64
CM_GROUPS = 4
CM_CHUNK = 128
TOP_K = 2

V7X_LANES = 128
V7X_SUBLANES = 8
V7X_VMEM_LIMIT = 56 * 1024 * 1024

ROW_BLOCK = 768
RET_CHUNK = 256
FF_CHUNK = 256
MOE_TILE = 512
MOE_SPLITS = 2
LOG2_E = 1.4426950408889634
TINY = 1e-30


def _cparams(*sem):
    return pltpu.CompilerParams(dimension_semantics=sem, vmem_limit_bytes=V7X_VMEM_LIMIT)


def _layer_spec(arr, layer):
    nd = arr.ndim - 1
    return pl.BlockSpec((None,) + arr.shape[1:], lambda *_: (layer,) + (0,) * nd,
                        pipeline_mode=pl.Buffered(1))


def _const_spec(shape):
    nd = len(shape)
    return pl.BlockSpec(shape, lambda *_: (0,) * nd, pipeline_mode=pl.Buffered(1))


def _rms(x):
    return x * lax.rsqrt(jnp.mean(x * x, axis=-1, keepdims=True) + EPS)


def _gelu_tanh(x):
    return 0.5 * x * (1.0 + jnp.tanh(0.7978845608028654 * (x + 0.044715 * (x * x * x))))


def _sigmoid(x):
    return 1.0 / (1.0 + jnp.exp(-x))


def _silu(x):
    return x * _sigmoid(x)


def _mod_kernel(c_ref, w_ref, b_ref, o_ref):
    s = _silu(c_ref[...])
    o_ref[...] = jnp.dot(s, w_ref[...], preferred_element_type=F32) + b_ref[...]


def _modulation(cpad, w_mod, b_mod):
    depth, d, nd = w_mod.shape
    r = cpad.shape[0]
    return pl.pallas_call(
        _mod_kernel,
        grid=(depth, nd // d),
        in_specs=[
            pl.BlockSpec((r, d), lambda l, n: (0, 0)),
            pl.BlockSpec((None, d, d), lambda l, n: (l, 0, n)),
            pl.BlockSpec((None, 1, d), lambda l, n: (l, 0, n)),
        ],
        out_specs=pl.BlockSpec((None, r, d), lambda l, n: (l, 0, n)),
        out_shape=jax.ShapeDtypeStruct((depth, r, nd), F32),
        compiler_params=_cparams("parallel", "parallel"),
        name="modulation",
    )(cpad, w_mod, b_mod.reshape(depth, 1, nd))


class _Rows:
    def __init__(self, batch, seq, ctx_len, latent_only=False, batch0=0, n_batch=None):
        self.batch0 = batch0
        self.n_batch = batch if n_batch is None else n_batch
        self.batch = batch
        if latent_only:
            self.rb = ctx_len
            self.blk0 = 1
            self.n_blk = (seq - ctx_len) // ctx_len
            self.ctx_rows = 0
        else:
            self.rb = ROW_BLOCK
            self.blk0 = 0
            self.n_blk = seq // ROW_BLOCK
            self.ctx_rows = ctx_len
        self.rows = self.rb * self.n_blk

    @property
    def grid(self):
        return (self.n_batch, self.n_blk)

    def slab(self, width):
        return pl.BlockSpec((1, self.rb, width), lambda b, j: (b + self.batch0, j + self.blk0, 0))

    def walked(self, width):
        return pl.BlockSpec((1, self.rb, width), lambda b, j: (b + self.batch0, j, 0))

    def local(self, width):
        return pl.BlockSpec((1, self.rb, width), lambda b, j: (b, j, 0))

    def mod_specs(self, mod_all, layer):
        d = mod_all.shape[-1]
        lat = pl.BlockSpec((None, 1, N_MOD, d), lambda b, j: (layer, b + self.batch0, 0, 0))
        ctx = pl.BlockSpec((None, 1, N_MOD, d), lambda b, j: (layer, self.batch, 0, 0))
        return lat, ctx


def _row_mod(ml_ref, mc_ref, idx, first_block):
    lat = ml_ref[0, idx:idx + 1, :]
    top = jnp.where(first_block, mc_ref[0, idx:idx + 1, :], lat)
    return top, lat


def _prenorm_modulate(x_ref, g_ref, ml_ref, mc_ref, h_ref, ctx_rows, shift_idx):
    first = pl.program_id(1) == 0
    sh_top, sh_lat = _row_mod(ml_ref, mc_ref, shift_idx, first)
    sc_top, sc_lat = _row_mod(ml_ref, mc_ref, shift_idx + 1, first)
    y = _rms(x_ref[0]) * g_ref[...]
    if ctx_rows:
        h_ref[:ctx_rows, :] = (y[:ctx_rows] * (1.0 + sc_top) + sh_top).astype(h_ref.dtype)
    h_ref[ctx_rows:, :] = (y[ctx_rows:] * (1.0 + sc_lat) + sh_lat).astype(h_ref.dtype)


def _postnorm_residual(x, y, g_ref, ml_ref, mc_ref, o_ref, ctx_rows, gate_idx):
    first = pl.program_id(1) == 0
    gt_top, gt_lat = _row_mod(ml_ref, mc_ref, gate_idx, first)
    r = _rms(y) * g_ref[...]
    if ctx_rows:
        o_ref[0, :ctx_rows, :] = x[:ctx_rows] + gt_top * r[:ctx_rows]
    o_ref[0, ctx_rows:, :] = x[ctx_rows:] + gt_lat * r[ctx_rows:]


def _inproj_kernel(x_ref, g_ref, ml_ref, mc_ref, w_ref, ol_ref, or_ref, oc_ref, h_ref, *, ctx_rows):
    _prenorm_modulate(x_ref, g_ref, ml_ref, mc_ref, h_ref, ctx_rows, 0)
    h = h_ref[...]
    c0 = 0
    for o_ref in (ol_ref, or_ref, oc_ref):
        n = o_ref.shape[2]
        o_ref[0] = jnp.dot(h, w_ref[:, c0:c0 + n], preferred_element_type=F32).astype(o_ref.dtype)
        c0 += n


def _inproj(xs, g_all, mod_all, w_in_b, layer, widths, ctx_len):
    b, s, d = xs.shape
    rows = _Rows(b, s, ctx_len)
    ml_spec, mc_spec = rows.mod_specs(mod_all, layer)
    return pl.pallas_call(
        functools.partial(_inproj_kernel, ctx_rows=rows.ctx_rows),
        grid=rows.grid,
        in_specs=[rows.slab(d), _layer_spec(g_all, layer), ml_spec, mc_spec, _layer_spec(w_in_b, layer)],
        out_specs=[rows.slab(n) for n in widths],
        out_shape=[jax.ShapeDtypeStruct((b, s, n), BF16) for n in widths],
        scratch_shapes=[pltpu.VMEM((rows.rb, d), BF16)],
        compiler_params=_cparams("parallel", "parallel"),
        name="mixer_inproj",
    )(xs, g_all, mod_all, mod_all, w_in_b)


def _tile_scan(a, u, carry, row, reverse):
    n = V7X_SUBLANES
    for dist in (1, 2, 4):
        if reverse:
            keep = row < (n - dist)
            shift = n - dist
        else:
            keep = row >= dist
            shift = dist
        a_s = jnp.where(keep, pltpu.roll(a, shift, 0), 1.0)
        u_s = jnp.where(keep, pltpu.roll(u, shift, 0), 0.0)
        u = a * u_s + u
        a = a * a_s
    h = a * carry + u
    new_carry = h[0:1] if reverse else h[n - 1:n]
    return h, new_carry


def _lru_kernel(p_ref, cw_ref, cb_ref, wg_ref, bg_ref, lam_ref, o_ref,
                xpad_ref, a_ref, u_ref, h_ref, *, ctx_len, seq):
    w = cw_ref.shape[1]
    s = seq
    pad = V7X_SUBLANES
    chunk = ctx_len
    zeros = jnp.zeros((pad, w), F32)
    xpad_ref[0:pad, :] = zeros
    xpad_ref[pad + ctx_len:2 * pad + ctx_len, :] = zeros
    xpad_ref[2 * pad + s:3 * pad + s, :] = zeros
    xpad_ref[pad:pad + ctx_len, :] = p_ref[0, 0:ctx_len, 0:w].astype(F32)
    xpad_ref[2 * pad + ctx_len:2 * pad + s, :] = p_ref[0, ctx_len:s, 0:w].astype(F32)

    neg = -lam_ref[...]
    softplus = jnp.maximum(neg, 0.0) + jnp.log(1.0 + jnp.exp(-jnp.abs(neg)))
    c2 = (-0.5 * LRU_C * LOG2_E) * softplus

    lanes = V7X_LANES
    n_slabs = w // lanes
    pitch = a_ref.shape[2] // V7X_SUBLANES
    lat_len = s - ctx_len

    def phys(d, r0):
        if d == 0:
            return r0
        return lat_len + r0 if r0 < ctx_len else r0 - ctx_len

    for d in range(2):
        for j in range(n_slabs):
            a_ref[d, j, s:, :] = jnp.ones((a_ref.shape[2] - s, lanes), F32)
            u_ref[d, j, s:, :] = jnp.zeros((a_ref.shape[2] - s, lanes), F32)

    for c in range(s // chunk):
        r0 = c * chunk
        base = r0 + (pad if r0 < ctx_len else 2 * pad)
        xc = jnp.zeros((chunk, w), F32) + cb_ref[...]
        for k in range(LRU_CONV):
            off = k - LRU_CONV // 2
            xc = xc + xpad_ref[base + off:base + off + chunk, :] * cw_ref[k:k + 1, :]
        th = jnp.tanh(jnp.dot(xc.astype(BF16), wg_ref[...], preferred_element_type=F32) + bg_ref[...])
        hx = 0.5 * xc
        for d in range(2):
            th_r = th[:, (2 * d) * w:(2 * d + 1) * w]
            th_i = th[:, (2 * d + 1) * w:(2 * d + 2) * w]
            cd = c2[d:d + 1, :]
            a = jnp.exp2(cd * th_r + cd)
            y = 1.0 - a * a
            u = (y * lax.rsqrt(jnp.maximum(y, TINY))) * (hx * th_i + hx)
            pr = phys(d, r0)
            for j in range(n_slabs):
                a_ref[d, j, pr:pr + chunk, :] = a[:, j * lanes:(j + 1) * lanes]
                u_ref[d, j, pr:pr + chunk, :] = u[:, j * lanes:(j + 1) * lanes]

    row = lax.broadcasted_iota(jnp.int32, (V7X_SUBLANES, lanes), 0)
    chains = [(d, j) for d in range(2) for j in range(n_slabs)]

    def seg_rows(d, t):
        tt = t if d == 0 else pitch - 1 - t
        return pl.ds(tt, V7X_SUBLANES, stride=pitch)

    def pass1(t, carry):
        out = []
        for (d, j), (h, pprod) in zip(chains, carry):
            a = a_ref[d, j, seg_rows(d, t), :]
            u = u_ref[d, j, seg_rows(d, t), :]
            out.append((a * h + u, a * pprod))
        return tuple(out)

    zero = jnp.zeros((V7X_SUBLANES, lanes), F32)
    ends = lax.fori_loop(0, pitch, pass1, tuple((zero, zero + 1.0) for _ in chains), unroll=4)

    starts = []
    for (d, j), (h_end, p_tot) in zip(chains, ends):
        incl, _ = _tile_scan(p_tot, h_end, jnp.zeros((1, lanes), F32), row, d == 1)
        if d == 0:
            starts.append(jnp.where(row >= 1, pltpu.roll(incl, 1, 0), 0.0))
        else:
            starts.append(jnp.where(row < V7X_SUBLANES - 1, pltpu.roll(incl, V7X_SUBLANES - 1, 0), 0.0))

    def pass2(t, carry):
        out = []
        for (d, j), h in zip(chains, carry):
            a = a_ref[d, j, seg_rows(d, t), :]
            u = u_ref[d, j, seg_rows(d, t), :]
            h = a * h + u
            h_ref[d, j, seg_rows(d, t), :] = h
            out.append(h)
        return tuple(out)

    lax.fori_loop(0, pitch, pass2, tuple(starts), unroll=4)

    for c in range(s // chunk):
        r0 = c * chunk
        rb = phys(1, r0)
        for j in range(n_slabs):
            ly = p_ref[0, r0:r0 + chunk, w + j * lanes:w + (j + 1) * lanes].astype(F32)
            h = h_ref[0, j, r0:r0 + chunk, :] + h_ref[1, j, rb:rb + chunk, :]
            o_ref[0, r0:r0 + chunk, j * lanes:(j + 1) * lanes] = (_gelu_tanh(ly) * h).astype(o_ref.dtype)


def _lru(p_lru, conv_w, conv_b, wg, bg, lam, layer, ctx_len):
    b, s, w2 = p_lru.shape
    w = w2 // 2
    assert s % (V7X_SUBLANES * V7X_SUBLANES) == 0
    scan_rows = V7X_SUBLANES * (s // V7X_SUBLANES + 4)
    scan_buf = pltpu.VMEM((2, w // V7X_LANES, scan_rows, V7X_LANES), F32)
    return pl.pallas_call(
        functools.partial(_lru_kernel, ctx_len=ctx_len, seq=s),
        grid=(b,),
        in_specs=[pl.BlockSpec((1, s, w2), lambda i: (i, 0, 0))]
        + [_layer_spec(a, layer) for a in (conv_w, conv_b, wg, bg, lam)],
        out_specs=pl.BlockSpec((1, s, w), lambda i: (i, 0, 0)),
        out_shape=jax.ShapeDtypeStruct((b, s, w), BF16),
        scratch_shapes=[pltpu.VMEM((s + 3 * V7X_SUBLANES, w), F32), scan_buf, scan_buf, scan_buf],
        compiler_params=_cparams("parallel"),
        name="rglru",
    )(p_lru, conv_w, conv_b, wg, bg, lam)


def _rope(x, cos, sin_signed, lane_even):
    partner = jnp.where(lane_even, pltpu.roll(x, V7X_LANES - 1, 1), pltpu.roll(x, 1, 1))
    return x * cos + partner * sin_signed


def _ret_kernel(lg_ref, p_ref, cos_ref, sin_ref, lgl_ref, o_ref,
                q_ref, k_ref, dlt_ref, st_ref, *, layer, ctx_len, seq, head_dim):
    s = seq
    c = RET_CHUNK
    lanes = V7X_LANES
    w = o_ref.shape[2]
    n_pairs = w // lanes
    n_chunks = s // c
    n_ctx = ctx_len // c
    scale = head_dim ** -0.5

    rowi = lax.broadcasted_iota(jnp.int32, (c, lanes), 0).astype(F32)
    lane = lax.broadcasted_iota(jnp.int32, (c, lanes), 1)
    lane_even = (lane % 2) == 0
    low_half = lane < head_dim
    dif = (lax.broadcasted_iota(jnp.int32, (c, c), 0)
           - lax.broadcasted_iota(jnp.int32, (c, c), 1)).astype(F32)
    blk_row_low = lax.broadcasted_iota(jnp.int32, (lanes, lanes), 0) < head_dim
    blk_col_low = lax.broadcasted_iota(jnp.int32, (lanes, lanes), 1) < head_dim
    blk_diag = blk_row_low == blk_col_low

    order_b = list(range(n_ctx - 1, -1, -1)) + list(range(n_chunks - 1, n_ctx - 1, -1))

    for p in range(n_pairs):
        lsl = slice(p * lanes, (p + 1) * lanes)
        lg_f = lgl_ref[0:1, lsl]
        lg_b = lgl_ref[1:2, lsl]
        kd_f = jnp.exp(lg_f * (c - 1.0 - rowi))
        kd_b = jnp.exp(lg_b * rowi)
        qd_f = jnp.exp(lg_f * (rowi + 1.0))
        qd_b = jnp.exp(lg_b * (c - rowi))

        for ch in range(n_chunks):
            r0 = ch * c
            q = p_ref[0, r0:r0 + c, p * lanes:(p + 1) * lanes].astype(F32)
            k = p_ref[0, r0:r0 + c, w + p * lanes:w + (p + 1) * lanes].astype(F32)
            v = p_ref[0, r0:r0 + c, 2 * w + p * lanes:2 * w + (p + 1) * lanes]
            if ch >= n_ctx:
                t0 = r0 - ctx_len
                cos = cos_ref[t0:t0 + c, :]
                sin = sin_ref[t0:t0 + c, :]
                q = _rope(q, cos, sin, lane_even)
                k = _rope(k, cos, sin, lane_even)
            q_ref[r0:r0 + c, lsl] = (q * scale).astype(BF16)
            k_ref[r0:r0 + c, lsl] = k.astype(BF16)
            d_f = pl.dot((k * kd_f).astype(BF16), v, trans_a=True)
            d_b = pl.dot((k * kd_b).astype(BF16), v, trans_a=True)
            dlt_ref[0, ch] = jnp.where(blk_diag, d_f, 0.0)
            dlt_ref[1, ch] = jnp.where(blk_diag, d_b, 0.0)

        gc_f = jnp.exp(jnp.where(blk_row_low, lg_ref[layer, 0, 2 * p], lg_ref[layer, 0, 2 * p + 1]) * float(c))
        gc_b = jnp.exp(jnp.where(blk_row_low, lg_ref[layer, 1, 2 * p], lg_ref[layer, 1, 2 * p + 1]) * float(c))
        state = jnp.zeros((lanes, lanes), F32)
        for ch in range(n_chunks):
            st_ref[0, ch] = state.astype(BF16)
            state = gc_f * state + dlt_ref[0, ch]
        state = jnp.zeros((lanes, lanes), F32)
        for ch in order_b:
            st_ref[1, ch] = state.astype(BF16)
            state = gc_b * state + dlt_ref[1, ch]

        dmats = []
        for h in range(2):
            lgf_s = lg_ref[layer, 0, 2 * p + h]
            lgb_s = lg_ref[layer, 1, 2 * p + h]
            dmats.append(jnp.where(dif > 0, jnp.exp(lgf_s * jnp.maximum(dif, 0.0)),
                                   jnp.where(dif < 0, jnp.exp(lgb_s * jnp.maximum(-dif, 0.0)), 2.0)))

        def out_chunk(ch, _):
            r0 = pl.multiple_of(ch * c, c)
            rows = pl.ds(r0, c)
            q = q_ref[rows, lsl]
            k = k_ref[rows, lsl]
            v = p_ref[0, rows, 2 * w + p * lanes:2 * w + (p + 1) * lanes]
            g = p_ref[0, rows, 3 * w + p * lanes:3 * w + (p + 1) * lanes].astype(F32)
            zero = jnp.zeros_like(q)
            s0 = pl.dot(jnp.where(low_half, q, zero), k, trans_b=True)
            s1 = pl.dot(jnp.where(low_half, zero, q), k, trans_b=True)
            o0 = jnp.dot((s0 * dmats[0]).astype(BF16), v, preferred_element_type=F32)
            o1 = jnp.dot((s1 * dmats[1]).astype(BF16), v, preferred_element_type=F32)
            o = jnp.where(low_half, o0, o1)
            qf = q.astype(F32)
            o = o + jnp.dot((qf * qd_f).astype(BF16), st_ref[0, ch], preferred_element_type=F32)
            o = o + jnp.dot((qf * qd_b).astype(BF16), st_ref[1, ch], preferred_element_type=F32)
            oo = o * o
            ss0 = jnp.sum(jnp.where(low_half, oo, 0.0), axis=-1, keepdims=True)
            ss1 = jnp.sum(jnp.where(low_half, 0.0, oo), axis=-1, keepdims=True)
            inv = jnp.where(low_half, lax.rsqrt(ss0 / head_dim + EPS), lax.rsqrt(ss1 / head_dim + EPS))
            o_ref[0, rows, lsl] = (_silu(g) * (o * inv)).astype(o_ref.dtype)
            return 0

        lax.fori_loop(0, n_chunks, out_chunk, 0, unroll=3 if n_chunks % 3 == 0 else 1)


def _retention(p_ret, lg, lg_lane, cos, sin_signed, layer, ctx_len, head_dim):
    b, s, w4 = p_ret.shape
    w = w4 // 4
    n_chunks = s // RET_CHUNK
    return pl.pallas_call(
        functools.partial(_ret_kernel, layer=layer, ctx_len=ctx_len, seq=s, head_dim=head_dim),
        grid=(b,),
        in_specs=[
            pl.BlockSpec(memory_space=pltpu.SMEM),
            pl.BlockSpec((1, s, w4), lambda i: (i, 0, 0)),
            _const_spec(cos.shape), _const_spec(sin_signed.shape), _layer_spec(lg_lane, layer),
        ],
        out_specs=pl.BlockSpec((1, s, w), lambda i: (i, 0, 0)),
        out_shape=jax.ShapeDtypeStruct((b, s, w), BF16),
        scratch_shapes=[
            pltpu.VMEM((s, w), BF16),
            pltpu.VMEM((s, w), BF16),
            pltpu.VMEM((2, n_chunks, V7X_LANES, V7X_LANES), F32),
            pltpu.VMEM((2, n_chunks, V7X_LANES, V7X_LANES), BF16),
        ],
        compiler_params=_cparams("parallel"),
        name="retention",
    )(lg, p_ret, cos, sin_signed, lg_lane)


def _cm_kernel(z_ref, lng_ref, lnb_ref, ws_ref, bs_ref, o_ref, *, seq, group_dim):
    c = CM_CHUNK
    lanes = V7X_LANES
    w = o_ref.shape[2]
    n_slabs = w // lanes
    low_half = lax.broadcasted_iota(jnp.int32, (c, lanes), 1) < group_dim
    lane_w = lax.broadcasted_iota(jnp.int32, (c, w), 1)

    def body(ch, _):
        r0 = pl.multiple_of(ch * c, c)
        rows = pl.ds(r0, c)
        z = _gelu_tanh(z_ref[0, rows, :].astype(F32))
        u = z[:, :w]
        vn = []
        for sl in range(n_slabs):
            v = z[:, w + sl * lanes:w + (sl + 1) * lanes]
            m0 = jnp.sum(jnp.where(low_half, v, 0.0), axis=-1, keepdims=True) / group_dim
            m1 = jnp.sum(jnp.where(low_half, 0.0, v), axis=-1, keepdims=True) / group_dim
            xc = v - jnp.where(low_half, m0, m1)
            xx = xc * xc
            v0 = jnp.sum(jnp.where(low_half, xx, 0.0), axis=-1, keepdims=True) / group_dim
            v1 = jnp.sum(jnp.where(low_half, 0.0, xx), axis=-1, keepdims=True) / group_dim
            vn.append(xc * jnp.where(low_half, lax.rsqrt(v0 + EPS), lax.rsqrt(v1 + EPS)))
        vn = jnp.concatenate(vn, axis=-1) * lng_ref[...] + lnb_ref[...]
        s_all = jnp.dot(ws_ref[...], vn.astype(BF16), preferred_element_type=F32)
        sp = s_all[0:c]
        for gi in range(1, CM_GROUPS):
            sp = jnp.where(lane_w >= gi * group_dim, s_all[gi * c:(gi + 1) * c], sp)
        o_ref[0, rows, :] = (u * (sp + bs_ref[...])).astype(o_ref.dtype)
        return 0

    lax.fori_loop(0, seq // c, body, 0, unroll=3 if (seq // c) % 3 == 0 else 1)


def _chunk_mlp(p_cm, ln_g, ln_b, ws, bs, layer, group_dim):
    b, s, w2 = p_cm.shape
    w = w2 // 2
    return pl.pallas_call(
        functools.partial(_cm_kernel, seq=s, group_dim=group_dim),
        grid=(b,),
        in_specs=[pl.BlockSpec((1, s, w2), lambda i: (i, 0, 0))]
        + [_layer_spec(a, layer) for a in (ln_g, ln_b, ws, bs)],
        out_specs=pl.BlockSpec((1, s, w), lambda i: (i, 0, 0)),
        out_shape=jax.ShapeDtypeStruct((b, s, w), BF16),
        compiler_params=_cparams("parallel"),
        name="chunk_gmlp",
    )(p_cm, ln_g, ln_b, ws, bs)


def _outproj_kernel(x_ref, g_ref, ml_ref, mc_ref, a_ref, b_ref, c_ref, w_ref, o_ref, *, ctx_rows):
    y = None
    r0 = 0
    for part in (a_ref, b_ref, c_ref):
        n = part.shape[2]
        t = jnp.dot(part[0], w_ref[r0:r0 + n, :], preferred_element_type=F32)
        y = t if y is None else y + t
        r0 += n
    _postnorm_residual(x_ref[0], y, g_ref, ml_ref, mc_ref, o_ref, ctx_rows, 2)


def _outproj(xs, g_all, mod_all, parts, w_out_b, layer, ctx_len):
    b, s, d = xs.shape
    rows = _Rows(b, s, ctx_len)
    ml_spec, mc_spec = rows.mod_specs(mod_all, layer)
    return pl.pallas_call(
        functools.partial(_outproj_kernel, ctx_rows=rows.ctx_rows),
        grid=rows.grid,
        in_specs=[rows.slab(d), _layer_spec(g_all, layer), ml_spec, mc_spec]
        + [rows.slab(a.shape[2]) for a in parts] + [_layer_spec(w_out_b, layer)],
        out_specs=rows.slab(d),
        out_shape=jax.ShapeDtypeStruct(xs.shape, F32),
        input_output_aliases={0: 0},
        compiler_params=_cparams("parallel", "parallel"),
        name="mixer_outproj",
    )(xs, g_all, mod_all, mod_all, *parts, w_out_b)


def _swiglu_rows(h, w1_ref, w3_ref, w2_ref):
    f = w1_ref.shape[1]
    acc = None
    for c0 in range(0, f, FF_CHUNK):
        a = jnp.dot(h, w1_ref[:, c0:c0 + FF_CHUNK], preferred_element_type=F32)
        g = jnp.dot(h, w3_ref[:, c0:c0 + FF_CHUNK], preferred_element_type=F32)
        act = (_silu(a) * g).astype(BF16)
        part = jnp.dot(act, w2_ref[c0:c0 + FF_CHUNK, :], preferred_element_type=F32)
        acc = part if acc is None else acc + part
    return acc


def _ffn_kernel(x_ref, gpre_ref, gpost_ref, ml_ref, mc_ref, w1_ref, w3_ref, w2_ref, o_ref, h_ref,
                *, ctx_rows):
    _prenorm_modulate(x_ref, gpre_ref, ml_ref, mc_ref, h_ref, ctx_rows, 3)
    y = _swiglu_rows(h_ref[...], w1_ref, w3_ref, w2_ref)
    _postnorm_residual(x_ref[0], y, gpost_ref, ml_ref, mc_ref, o_ref, ctx_rows, 5)


def _dense_ffn(xs, gpre_all, gpost_all, mod_all, w1, w3, w2, layer, j, ctx_len):
    b, s, d = xs.shape
    rows = _Rows(b, s, ctx_len)
    ml_spec, mc_spec = rows.mod_specs(mod_all, layer)
    return pl.pallas_call(
        functools.partial(_ffn_kernel, ctx_rows=rows.ctx_rows),
        grid=rows.grid,
        in_specs=[rows.slab(d), _layer_spec(gpre_all, layer), _layer_spec(gpost_all, layer), ml_spec, mc_spec]
        + [_layer_spec(w, j) for w in (w1, w3, w2)],
        out_specs=rows.slab(d),
        out_shape=jax.ShapeDtypeStruct(xs.shape, F32),
        scratch_shapes=[pltpu.VMEM((rows.rb, d), BF16)],
        input_output_aliases={0: 0},
        compiler_params=_cparams("parallel", "parallel"),
        name="dense_swiglu",
    )(xs, gpre_all, gpost_all, mod_all, mod_all, w1, w3, w2)


def _router_kernel(x_ref, g_ref, ml_ref, mc_ref, wr_ref, h_ref, idx_ref, wgt_ref, hf_ref,
                   *, ctx_rows, n_experts):
    _prenorm_modulate(x_ref, g_ref, ml_ref, mc_ref, hf_ref, ctx_rows, 3)
    h = hf_ref[...]
    h_ref[0] = h.astype(h_ref.dtype)
    logits = jnp.dot(h, wr_ref[...], preferred_element_type=F32)
    lane = lax.broadcasted_iota(jnp.int32, logits.shape, 1).astype(F32)
    neg_inf = jnp.float32(-jnp.inf)
    logits = jnp.where(lane < n_experts, logits, neg_inf)
    big = jnp.float32(V7X_LANES)
    m1 = jnp.max(logits, axis=-1, keepdims=True)
    i1 = jnp.min(jnp.where(logits == m1, lane, big), axis=-1, keepdims=True)
    rest = jnp.where(lane == i1, neg_inf, logits)
    m2 = jnp.max(rest, axis=-1, keepdims=True)
    i2 = jnp.min(jnp.where(rest == m2, lane, big), axis=-1, keepdims=True)
    e2 = jnp.exp(m2 - m1)
    w1 = 1.0 / (1.0 + e2)
    w2 = e2 / (1.0 + e2)
    out_lane = lax.broadcasted_iota(jnp.int32, idx_ref.shape[1:], 1)
    idx_ref[0] = jnp.where(out_lane == 0, i1, i2).astype(jnp.int32)
    wgt_ref[0] = jnp.where(out_lane == 0, w1, w2)


def _router(xs, g_all, mod_all, wr_pad, layer, j, rows, n_experts):
    b, s, d = xs.shape
    ml_spec, mc_spec = rows.mod_specs(mod_all, layer)
    return pl.pallas_call(
        functools.partial(_router_kernel, ctx_rows=rows.ctx_rows, n_experts=n_experts),
        grid=rows.grid,
        in_specs=[rows.slab(d), _layer_spec(g_all, layer), ml_spec, mc_spec, _layer_spec(wr_pad, j)],
        out_specs=[rows.walked(d), rows.walked(V7X_SUBLANES), rows.walked(V7X_SUBLANES)],
        out_shape=[jax.ShapeDtypeStruct((b, rows.rows, d), BF16),
                   jax.ShapeDtypeStruct((b, rows.rows, V7X_SUBLANES), jnp.int32),
                   jax.ShapeDtypeStruct((b, rows.rows, V7X_SUBLANES), F32)],
        scratch_shapes=[pltpu.VMEM((rows.rb, d), F32)],
        compiler_params=_cparams("parallel", "parallel"),
        name="moe_router",
    )(xs, g_all, mod_all, mod_all, wr_pad)


def _grouped_kernel(te_ref, tv_ref, h_ref, w1_ref, w3_ref, w2_ref, o_ref):
    i = pl.program_id(0)

    @pl.when(tv_ref[i] != 0)
    def _():
        o_ref[...] = _swiglu_rows(h_ref[...], w1_ref, w3_ref, w2_ref).astype(o_ref.dtype)

    @pl.when(tv_ref[i] == 0)
    def _():
        o_ref[...] = jnp.zeros_like(o_ref)


def _grouped_swiglu(layer_idx, tile_expert, tile_valid, hs, w1, w3, w2):
    p, d = hs.shape
    f = w1.shape[3]
    tm = MOE_TILE
    grid_spec = pltpu.PrefetchScalarGridSpec(
        num_scalar_prefetch=2,
        grid=(p // tm,),
        in_specs=[
            pl.BlockSpec((tm, d), lambda i, te, tv: (i, 0)),
            pl.BlockSpec((None, None, d, f), lambda i, te, tv: (layer_idx, te[i], 0, 0)),
            pl.BlockSpec((None, None, d, f), lambda i, te, tv: (layer_idx, te[i], 0, 0)),
            pl.BlockSpec((None, None, f, d), lambda i, te, tv: (layer_idx, te[i], 0, 0)),
        ],
        out_specs=pl.BlockSpec((tm, d), lambda i, te, tv: (i, 0)),
    )
    return pl.pallas_call(
        _grouped_kernel,
        grid_spec=grid_spec,
        out_shape=jax.ShapeDtypeStruct((p, d), BF16),
        compiler_params=_cparams("arbitrary"),
        name="moe_grouped_swiglu",
    )(tile_expert, tile_valid, hs, w1, w3, w2)


def _combine_kernel(*refs, ctx_rows, has_dest):
    x_ref, g_ref, ml_ref, mc_ref, y0_ref, y1_ref, wgt_ref = refs[:7]
    o_ref = refs[-1]
    wgt = wgt_ref[0]
    y = wgt[:, 0:1] * y0_ref[0].astype(F32) + wgt[:, 1:2] * y1_ref[0].astype(F32)
    _postnorm_residual(x_ref[0], y, g_ref, ml_ref, mc_ref, o_ref, ctx_rows, 5)


def _combine(xs, g_all, mod_all, y0, y1, wgt, layer, rows, dest):
    b, s, d = xs.shape
    ml_spec, mc_spec = rows.mod_specs(mod_all, layer)
    in_specs = [rows.slab(d), _layer_spec(g_all, layer), ml_spec, mc_spec,
                rows.local(d), rows.local(d), rows.walked(V7X_SUBLANES)]
    args = [xs, g_all, mod_all, mod_all, y0, y1, wgt]
    if dest is None:
        out_spec, out_shape, aliases = rows.slab(d), jax.ShapeDtypeStruct(xs.shape, F32), {0: 0}
    else:
        out_spec, out_shape, aliases = rows.walked(d), jax.ShapeDtypeStruct(dest.shape, F32), {}
        if not isinstance(dest, jax.ShapeDtypeStruct):
            in_specs.append(pl.BlockSpec(memory_space=pl.ANY))
            args.append(dest)
            aliases = {len(args) - 1: 0}
    return pl.pallas_call(
        functools.partial(_combine_kernel, ctx_rows=rows.ctx_rows, has_dest=dest is not None),
        grid=rows.grid,
        in_specs=in_specs,
        out_specs=out_spec,
        out_shape=out_shape,
        input_output_aliases=aliases,
        compiler_params=_cparams("parallel", "parallel"),
        name="moe_combine",
    )(*args)


def _gather_rows(src, idx):
    return src.at[idx].get(mode="promise_in_bounds")


def _routing_tables(e_flat, tok_flat, n_exp):
    n_assign = e_flat.shape[0]
    tm = MOE_TILE
    onehot = (e_flat[:, None] == jnp.arange(n_exp, dtype=jnp.int32)[None, :]).astype(jnp.int32)
    csum = jnp.cumsum(onehot, axis=0)
    counts = csum[-1]
    rank = jnp.sum((csum - onehot) * onehot, axis=1)
    padded = ((counts + tm - 1) // tm) * tm
    ends = jnp.cumsum(padded)
    starts = ends - padded
    pos = jnp.sum(onehot * starts[None, :], axis=1) + rank
    n_rows = n_assign + n_exp * tm
    n_tiles = n_rows // tm
    src_tok = jnp.zeros((n_rows,), jnp.int32).at[pos].set(
        tok_flat, unique_indices=True, mode="promise_in_bounds")
    tile_start = jnp.arange(n_tiles, dtype=jnp.int32) * tm
    tile_expert = jnp.minimum(jnp.sum((tile_start[:, None] >= ends[None, :]).astype(jnp.int32), axis=1),
                              n_exp - 1)
    tile_valid = (tile_start < ends[-1]).astype(jnp.int32)
    return pos, src_tok, tile_expert, tile_valid


def _moe_ffn(xs, gpre_all, gpost_all, mod_all, wr_pad, w1, w3, w2, layer, j, ctx_len, n_exp, last):
    b, s, d = xs.shape
    rows = _Rows(b, s, ctx_len, latent_only=last)
    h, idx, wgt = _router(xs, gpre_all, mod_all, wr_pad, layer, j, rows, n_exp)
    r = rows.rows
    h_flat = h.reshape(b * r, d)
    bh = b // MOE_SPLITS
    out = jax.ShapeDtypeStruct((b, r, d), F32) if last else None
    for g in range(MOE_SPLITS):
        e_flat = idx[g * bh:(g + 1) * bh, :, :TOP_K].reshape(bh * r * TOP_K)
        tok_flat = g * bh * r + jnp.arange(bh * r * TOP_K, dtype=jnp.int32) // TOP_K
        pos, src_tok, tile_expert, tile_valid = _routing_tables(e_flat, tok_flat, n_exp)
        hs = _gather_rows(h_flat, src_tok)
        ys = _grouped_swiglu(j, tile_expert, tile_valid, hs, w1, w3, w2)
        pos = pos.reshape(bh * r, TOP_K)
        y0 = _gather_rows(ys, pos[:, 0]).reshape(bh, r, d)
        y1 = _gather_rows(ys, pos[:, 1]).reshape(bh, r, d)
        rows_g = _Rows(b, s, ctx_len, latent_only=last, batch0=g * bh, n_batch=bh)
        res = _combine(xs, gpost_all, mod_all, y0, y1, wgt, layer, rows_g, out)
        if last:
            out = res
        else:
            xs = res
    return out if last else xs


def _block_diag(wh):
    h, d = wh.shape[-3], wh.shape[-1]
    eye = jnp.eye(h, dtype=wh.dtype)
    out = jnp.einsum("...hij,hg->...higj", wh, eye)
    return out.reshape(wh.shape[:-3] + (h * d, h * d))


def kernel(x, c, ctx, c_ctx, w_mod, b_mod, g_mix_pre, g_mix_post, g_ffn_pre, g_ffn_post, w_in, w_out,
           lru_conv_w, lru_conv_b, lru_wa, lru_ba, lru_wx, lru_bx, lru_lam, ret_theta,
           cm_ln_g, cm_ln_b, cm_ws, cm_bs, ffn_w1, ffn_w3, ffn_w2, router_w, moe_w1, moe_w3, moe_w2):
    b, l, d = x.shape
    lc = ctx.shape[1]
    s = lc + l
    depth = w_mod.shape[0]
    lru_w = lru_conv_w.shape[2]
    cm_w = cm_ln_g.shape[1] * cm_ln_g.shape[2]
    ret_w = (w_in.shape[2] - 2 * lru_w - 2 * cm_w) // 4
    head_dim = ret_w // RET_HEADS
    group_dim = cm_ln_g.shape[2]
    n_exp = router_w.shape[2]
    assert s % ROW_BLOCK == 0 and lc < ROW_BLOCK and lc % RET_CHUNK == 0 and l % RET_CHUNK == 0
    assert ret_w % V7X_LANES == 0 and cm_w % V7X_LANES == 0 and 2 * head_dim == V7X_LANES
    assert b % MOE_SPLITS == 0 and l % lc == 0

    n_rows = l // GRID_W
    rows = jnp.repeat(jnp.arange(n_rows, dtype=F32), GRID_W)
    cols = jnp.tile(jnp.arange(GRID_W, dtype=F32), n_rows)
    pairs = head_dim // 4
    freqs = ROPE_BASE ** (-jnp.arange(pairs, dtype=F32) / pairs)
    ang = jnp.concatenate([rows[:, None] * freqs, cols[:, None] * freqs], axis=-1)
    cos = jnp.tile(jnp.repeat(jnp.cos(ang), 2, axis=1), (1, V7X_LANES // head_dim))
    sin = jnp.repeat(jnp.sin(ang), 2, axis=1) * jnp.tile(jnp.array([-1.0, 1.0], F32), head_dim // 2)
    sin_signed = jnp.tile(sin, (1, V7X_LANES // head_dim))

    w_in_b, w_out_b = w_in.astype(BF16), w_out.astype(BF16)
    ffn_b1, ffn_b3, ffn_b2 = ffn_w1.astype(BF16), ffn_w3.astype(BF16), ffn_w2.astype(BF16)
    moe_b1, moe_b3, moe_b2 = moe_w1.astype(BF16), moe_w3.astype(BF16), moe_w2.astype(BF16)
    wg = (0.5 * jnp.concatenate([_block_diag(lru_wa[:, 0]), _block_diag(lru_wx[:, 0]),
                                 _block_diag(lru_wa[:, 1]), _block_diag(lru_wx[:, 1])], axis=2)).astype(BF16)
    bg = 0.5 * jnp.concatenate([lru_ba[:, 0].reshape(depth, 1, lru_w), lru_bx[:, 0].reshape(depth, 1, lru_w),
                                lru_ba[:, 1].reshape(depth, 1, lru_w), lru_bx[:, 1].reshape(depth, 1, lru_w)],
                               axis=2)
    conv_b = lru_conv_b.reshape(depth, 1, lru_w)
    lam = lru_lam.reshape(depth, 2, lru_w)
    lg = jax.nn.log_sigmoid(ret_theta.astype(F32))
    lg_lane = jnp.repeat(lg, head_dim, axis=2)
    ln_g = cm_ln_g.reshape(depth, 1, cm_w)
    ln_b = cm_ln_b.reshape(depth, 1, cm_w)
    ws = cm_ws.reshape(depth, CM_GROUPS * CM_CHUNK, CM_CHUNK).astype(BF16)
    bs = jnp.repeat(jnp.swapaxes(cm_bs, 1, 2), group_dim, axis=2)
    wr_pad = jnp.zeros((router_w.shape[0], d, V7X_LANES), F32).at[:, :, :n_exp].set(router_w)
    gains = [g.reshape(depth, 1, d) for g in (g_mix_pre, g_mix_post, g_ffn_pre, g_ffn_post)]
    widths = (2 * lru_w, 4 * ret_w, 2 * cm_w)

    xs = jnp.concatenate([ctx, x], axis=1)
    n_cond = ((b + 1 + V7X_SUBLANES - 1) // V7X_SUBLANES) * V7X_SUBLANES
    cpad = jnp.zeros((n_cond, d), F32).at[:b].set(c).at[b].set(c_ctx)
    mod_all = _modulation(cpad, w_mod, b_mod).reshape(depth, n_cond, N_MOD, d)

    for layer in range(depth):
        p_lru, p_ret, p_cm = _inproj(xs, gains[0], mod_all, w_in_b, layer, widths, lc)
        o_lru = _lru(p_lru, lru_conv_w, conv_b, wg, bg, lam, layer, lc)
        o_ret = _retention(p_ret, lg, lg_lane, cos, sin_signed, layer, lc, head_dim)
        o_cm = _chunk_mlp(p_cm, ln_g, ln_b, ws, bs, layer, group_dim)
        xs = _outproj(xs, gains[1], mod_all, (o_lru, o_ret, o_cm), w_out_b, layer, lc)

        j = layer // 2
        if layer % 2 == 0:
            xs = _dense_ffn(xs, gains[2], gains[3], mod_all, ffn_b1, ffn_b3, ffn_b2, layer, j, lc)
        else:
            xs = _moe_ffn(xs, gains[2], gains[3], mod_all, wr_pad, moe_b1, moe_b3, moe_b2,
                          layer, j, lc, n_exp, last=layer == depth - 1)
    return xs if depth % 2 == 0 else xs[:, lc:, :]
```

```python
import functools
import math

import jax
import jax.numpy as jnp
from jax import lax
from jax.experimental import pallas as pl
from jax.experimental.pallas import tpu as pltpu

F32 = jnp.float32
BF16 = jnp.bfloat16

EPS = 1e-6
N_MOD = 6
LRU_CONV = 4
LRU_C = 8.0
RET_HEADS = 6
ROPE_BASE = 100.0
GRID_W = 64
CM_GROUPS = 4
CM_CHUNK = 128
TOP_K = 2

V7X_LANES = 128
V7X_SUBLANES = 8
V7X_VMEM_LIMIT = 56 * 1024 * 1024

ROW_BLOCK = 768
RET_CHUNK = 256
FF_CHUNK = 256
MOE_TILE = 512
LAT_BLOCK = 1024
MOE_SPLITS = 2
LOG2_E = 1.4426950408889634
TINY = 1e-30


def _cparams(*sem):
    return pltpu.CompilerParams(dimension_semantics=sem, vmem_limit_bytes=V7X_VMEM_LIMIT)


def _layer_spec(arr, layer):
    nd = arr.ndim - 1
    return pl.BlockSpec((None,) + arr.shape[1:], lambda *_: (layer,) + (0,) * nd,
                        pipeline_mode=pl.Buffered(1))


def _const_spec(shape):
    nd = len(shape)
    return pl.BlockSpec(shape, lambda *_: (0,) * nd, pipeline_mode=pl.Buffered(1))


def _rms(x):
    return x * lax.rsqrt(jnp.mean(x * x, axis=-1, keepdims=True) + EPS)


def _gelu_tanh(x):
    return 0.5 * x * (1.0 + jnp.tanh(0.7978845608028654 * (x + 0.044715 * (x * x * x))))


def _sigmoid(x):
    return 1.0 / (1.0 + jnp.exp(-x))


def _silu(x):
    return x * _sigmoid(x)


def _mod_kernel(c_ref, w_ref, b_ref, o_ref):
    s = _silu(c_ref[...])
    o_ref[...] = jnp.dot(s, w_ref[...], preferred_element_type=F32) + b_ref[...]


def _modulation(cpad, w_mod, b_mod):
    depth, d, nd = w_mod.shape
    r = cpad.shape[0]
    return pl.pallas_call(
        _mod_kernel,
        grid=(depth, nd // d),
        in_specs=[
            pl.BlockSpec((r, d), lambda l, n: (0, 0)),
            pl.BlockSpec((None, d, d), lambda l, n: (l, 0, n)),
            pl.BlockSpec((None, 1, d), lambda l, n: (l, 0, n)),
        ],
        out_specs=pl.BlockSpec((None, r, d), lambda l, n: (l, 0, n)),
        out_shape=jax.ShapeDtypeStruct((depth, r, nd), F32),
        compiler_params=_cparams("parallel", "parallel"),
        name="modulation",
    )(cpad, w_mod, b_mod.reshape(depth, 1, nd))


class _Rows:
    def __init__(self, batch, seq, ctx_len, latent_only=False, batch0=0, n_batch=None):
        self.batch, self.seq, self.ctx_len = batch, seq, ctx_len
        self.lat_len = seq - ctx_len
        self.batch0 = batch0
        self.n_batch = batch if n_batch is None else n_batch
        self.latent_only = latent_only
        if latent_only:
            self.rb = LAT_BLOCK if self.lat_len % LAT_BLOCK == 0 else ctx_len
            self.n_blk = self.lat_len // self.rb
            self.ctx_rows = 0
        else:
            self.rb = ROW_BLOCK
            self.n_blk = seq // ROW_BLOCK
            self.ctx_rows = ctx_len
        self.rows = self.rb * self.n_blk
        self.align = math.gcd(self.rb, ctx_len)

    @property
    def grid(self):
        return (self.n_batch, self.n_blk)

    def slab(self, width):
        if not self.latent_only:
            return pl.BlockSpec((self.rb, width), lambda b, j: ((b + self.batch0) * self.n_blk + j, 0))

        def index(b, j):
            row = (b + self.batch0) * self.seq + self.ctx_len + j * self.rb
            return (pl.multiple_of(row, self.align), 0)
        return pl.BlockSpec((pl.Element(self.rb), pl.Element(width)), index)

    def walked(self, width):
        return pl.BlockSpec((self.rb, width), lambda b, j: ((b + self.batch0) * self.n_blk + j, 0))

    def local(self, width):
        return pl.BlockSpec((self.rb, width), lambda b, j: (b * self.n_blk + j, 0))

    def split_source(self, width):
        assert not self.latent_only and self.lat_len >= self.rb
        ctx = pl.BlockSpec((self.ctx_len, width), lambda b, j: (b + self.batch0, 0))

        def index(b, j):
            row = (b + self.batch0) * self.lat_len + jnp.maximum(j * self.rb - self.ctx_len, 0)
            return (pl.multiple_of(row, self.align), 0)
        return ctx, pl.BlockSpec((pl.Element(self.rb), pl.Element(width)), index)

    def mod_specs(self, mod_all, layer):
        d = mod_all.shape[-1]
        lat = pl.BlockSpec((None, 1, N_MOD, d), lambda b, j: (layer, b + self.batch0, 0, 0))
        ctx = pl.BlockSpec((None, 1, N_MOD, d), lambda b, j: (layer, self.batch, 0, 0))
        return lat, ctx


def _row_mod(ml_ref, mc_ref, idx, first_block):
    lat = ml_ref[0, idx:idx + 1, :]
    top = jnp.where(first_block, mc_ref[0, idx:idx + 1, :], lat)
    return top, lat


def _prenorm_modulate(x, g_ref, ml_ref, mc_ref, h_ref, ctx_rows, shift_idx):
    first = pl.program_id(1) == 0
    sh_top, sh_lat = _row_mod(ml_ref, mc_ref, shift_idx, first)
    sc_top, sc_lat = _row_mod(ml_ref, mc_ref, shift_idx + 1, first)
    y = _rms(x) * g_ref[...]
    if ctx_rows:
        h_ref[:ctx_rows, :] = (y[:ctx_rows] * (1.0 + sc_top) + sh_top).astype(h_ref.dtype)
    h_ref[ctx_rows:, :] = (y[ctx_rows:] * (1.0 + sc_lat) + sh_lat).astype(h_ref.dtype)


def _gated_residual(x, y, g_ref, ml_ref, mc_ref, o_ref, ctx_rows, gate_idx):
    first = pl.program_id(1) == 0
    gt_top, gt_lat = _row_mod(ml_ref, mc_ref, gate_idx, first)
    r = _rms(y) * g_ref[...]
    if ctx_rows:
        o_ref[:ctx_rows, :] = x[:ctx_rows] + gt_top * r[:ctx_rows]
    o_ref[ctx_rows:, :] = x[ctx_rows:] + gt_lat * r[ctx_rows:]


def _residual_rows(src_refs, xbuf_ref, ctx_rows):
    if len(src_refs) == 1:
        return src_refs[0]
    ctx_ref, lat_ref = src_refs
    first = pl.program_id(1) == 0

    @pl.when(first)
    def _():
        xbuf_ref[:ctx_rows, :] = ctx_ref[...]
        xbuf_ref[ctx_rows:, :] = lat_ref[:xbuf_ref.shape[0] - ctx_rows, :]

    @pl.when(jnp.logical_not(first))
    def _():
        xbuf_ref[...] = lat_ref[...]

    return xbuf_ref


def _source_specs(rows, src, d):
    if len(src) == 1:
        return [rows.slab(d)], []
    return list(rows.split_source(d)), [pltpu.VMEM((rows.rb, d), F32)]


def _mix_out(part_refs, w_ref):
    y = None
    r0 = 0
    for part in part_refs:
        n = part.shape[1]
        t = jnp.dot(part[...], w_ref[r0:r0 + n, :], preferred_element_type=F32)
        y = t if y is None else y + t
        r0 += n
    return y


def _inproj_kernel(*refs, n_src, ctx_rows):
    src, (g_ref, ml_ref, mc_ref, w_ref, ol_ref, or_ref, oc_ref, h_ref) = refs[:n_src], refs[n_src:n_src + 8]
    x_ref = _residual_rows(src, refs[-1], ctx_rows)
    _prenorm_modulate(x_ref[...], g_ref, ml_ref, mc_ref, h_ref, ctx_rows, 0)
    h = h_ref[...]
    c0 = 0
    for o_ref in (ol_ref, or_ref, oc_ref):
        n = o_ref.shape[1]
        o_ref[...] = jnp.dot(h, w_ref[:, c0:c0 + n], preferred_element_type=F32).astype(o_ref.dtype)
        c0 += n


def _inproj(src, rows, g_all, mod_all, w_in_b, layer, widths):
    d = src[0].shape[1]
    ml_spec, mc_spec = rows.mod_specs(mod_all, layer)
    src_specs, src_scratch = _source_specs(rows, src, d)
    n_total = rows.batch * rows.seq
    return pl.pallas_call(
        functools.partial(_inproj_kernel, n_src=len(src), ctx_rows=rows.ctx_rows),
        grid=rows.grid,
        in_specs=src_specs + [_layer_spec(g_all, layer), ml_spec, mc_spec, _layer_spec(w_in_b, layer)],
        out_specs=[rows.slab(n) for n in widths],
        out_shape=[jax.ShapeDtypeStruct((n_total, n), BF16) for n in widths],
        scratch_shapes=[pltpu.VMEM((rows.rb, d), BF16)] + src_scratch,
        compiler_params=_cparams("parallel", "parallel"),
        name="mixer_inproj",
    )(*src, g_all, mod_all, mod_all, w_in_b)


def _tile_scan(a, u, carry, row, reverse):
    n = V7X_SUBLANES
    for dist in (1, 2, 4):
        if reverse:
            keep = row < (n - dist)
            shift = n - dist
        else:
            keep = row >= dist
            shift = dist
        a_s = jnp.where(keep, pltpu.roll(a, shift, 0), 1.0)
        u_s = jnp.where(keep, pltpu.roll(u, shift, 0), 0.0)
        u = a * u_s + u
        a = a * a_s
    h = a * carry + u
    new_carry = h[0:1] if reverse else h[n - 1:n]
    return h, new_carry


def _lru_kernel(p_ref, cw_ref, cb_ref, wg_ref, bg_ref, lam_ref, o_ref,
                xpad_ref, a_ref, u_ref, h_ref, *, ctx_len, seq):
    w = cw_ref.shape[1]
    s = seq
    pad = V7X_SUBLANES
    chunk = ctx_len
    zeros = jnp.zeros((pad, w), F32)
    xpad_ref[0:pad, :] = zeros
    xpad_ref[pad + ctx_len:2 * pad + ctx_len, :] = zeros
    xpad_ref[2 * pad + s:3 * pad + s, :] = zeros
    xpad_ref[pad:pad + ctx_len, :] = p_ref[0, 0:ctx_len, 0:w].astype(F32)
    xpad_ref[2 * pad + ctx_len:2 * pad + s, :] = p_ref[0, ctx_len:s, 0:w].astype(F32)

    neg = -lam_ref[...]
    softplus = jnp.maximum(neg, 0.0) + jnp.log(1.0 + jnp.exp(-jnp.abs(neg)))
    c2 = (-0.5 * LRU_C * LOG2_E) * softplus

    lanes = V7X_LANES
    n_slabs = w // lanes
    pitch = a_ref.shape[2] // V7X_SUBLANES
    lat_len = s - ctx_len

    def phys(d, r0):
        if d == 0:
            return r0
        return lat_len + r0 if r0 < ctx_len else r0 - ctx_len

    for d in range(2):
        for j in range(n_slabs):
            a_ref[d, j, s:, :] = jnp.ones((a_ref.shape[2] - s, lanes), F32)
            u_ref[d, j, s:, :] = jnp.zeros((a_ref.shape[2] - s, lanes), F32)

    for c in range(s // chunk):
        r0 = c * chunk
        base = r0 + (pad if r0 < ctx_len else 2 * pad)
        xc = jnp.zeros((chunk, w), F32) + cb_ref[...]
        for k in range(LRU_CONV):
            off = k - LRU_CONV // 2
            xc = xc + xpad_ref[base + off:base + off + chunk, :] * cw_ref[k:k + 1, :]
        th = jnp.tanh(jnp.dot(xc.astype(BF16), wg_ref[...], preferred_element_type=F32) + bg_ref[...])
        hx = 0.5 * xc
        for d in range(2):
            th_r = th[:, (2 * d) * w:(2 * d + 1) * w]
            th_i = th[:, (2 * d + 1) * w:(2 * d + 2) * w]
            cd = c2[d:d + 1, :]
            a = jnp.exp2(cd * th_r + cd)
            y = 1.0 - a * a
            u = (y * lax.rsqrt(jnp.maximum(y, TINY))) * (hx * th_i + hx)
            pr = phys(d, r0)
            for j in range(n_slabs):
                a_ref[d, j, pr:pr + chunk, :] = a[:, j * lanes:(j + 1) * lanes]
                u_ref[d, j, pr:pr + chunk, :] = u[:, j * lanes:(j + 1) * lanes]

    row = lax.broadcasted_iota(jnp.int32, (V7X_SUBLANES, lanes), 0)
    chains = [(d, j) for d in range(2) for j in range(n_slabs)]

    def seg_rows(d, t):
        tt = t if d == 0 else pitch - 1 - t
        return pl.ds(tt, V7X_SUBLANES, stride=pitch)

    def pass1(t, carry):
        out = []
        for (d, j), (h, pprod) in zip(chains, carry):
            a = a_ref[d, j, seg_rows(d, t), :]
            u = u_ref[d, j, seg_rows(d, t), :]
            out.append((a * h + u, a * pprod))
        return tuple(out)

    zero = jnp.zeros((V7X_SUBLANES, lanes), F32)
    ends = lax.fori_loop(0, pitch, pass1, tuple((zero, zero + 1.0) for _ in chains), unroll=4)

    starts = []
    for (d, j), (h_end, p_tot) in zip(chains, ends):
        incl, _ = _tile_scan(p_tot, h_end, jnp.zeros((1, lanes), F32), row, d == 1)
        if d == 0:
            starts.append(jnp.where(row >= 1, pltpu.roll(incl, 1, 0), 0.0))
        else:
            starts.append(jnp.where(row < V7X_SUBLANES - 1, pltpu.roll(incl, V7X_SUBLANES - 1, 0), 0.0))

    def pass2(t, carry):
        out = []
        for (d, j), h in zip(chains, carry):
            a = a_ref[d, j, seg_rows(d, t), :]
            u = u_ref[d, j, seg_rows(d, t), :]
            h = a * h + u
            h_ref[d, j, seg_rows(d, t), :] = h
            out.append(h)
        return tuple(out)

    lax.fori_loop(0, pitch, pass2, tuple(starts), unroll=4)

    for c in range(s // chunk):
        r0 = c * chunk
        rb = phys(1, r0)
        for j in range(n_slabs):
            ly = p_ref[0, r0:r0 + chunk, w + j * lanes:w + (j + 1) * lanes].astype(F32)
            h = h_ref[0, j, r0:r0 + chunk, :] + h_ref[1, j, rb:rb + chunk, :]
            o_ref[0, r0:r0 + chunk, j * lanes:(j + 1) * lanes] = (_gelu_tanh(ly) * h).astype(o_ref.dtype)


def _lru(p_lru, conv_w, conv_b, wg, bg, lam, layer, ctx_len):
    b, s, w2 = p_lru.shape
    w = w2 // 2
    assert s % (V7X_SUBLANES * V7X_SUBLANES) == 0
    scan_rows = V7X_SUBLANES * (s // V7X_SUBLANES + 4)
    scan_buf = pltpu.VMEM((2, w // V7X_LANES, scan_rows, V7X_LANES), F32)
    return pl.pallas_call(
        functools.partial(_lru_kernel, ctx_len=ctx_len, seq=s),
        grid=(b,),
        in_specs=[pl.BlockSpec((1, s, w2), lambda i: (i, 0, 0))]
        + [_layer_spec(a, layer) for a in (conv_w, conv_b, wg, bg, lam)],
        out_specs=pl.BlockSpec((1, s, w), lambda i: (i, 0, 0)),
        out_shape=jax.ShapeDtypeStruct((b, s, w), BF16),
        scratch_shapes=[pltpu.VMEM((s + 3 * V7X_SUBLANES, w), F32), scan_buf, scan_buf, scan_buf],
        compiler_params=_cparams("parallel"),
        name="rglru",
    )(p_lru, conv_w, conv_b, wg, bg, lam)


def _rope(x, cos, sin_signed, lane_even):
    partner = jnp.where(lane_even, pltpu.roll(x, V7X_LANES - 1, 1), pltpu.roll(x, 1, 1))
    return x * cos + partner * sin_signed


def _ret_kernel(lg_ref, p_ref, cos_ref, sin_ref, lgl_ref, o_ref,
                q_ref, k_ref, dlt_ref, st_ref, *, layer, ctx_len, seq, head_dim):
    s = seq
    c = RET_CHUNK
    lanes = V7X_LANES
    w = o_ref.shape[2]
    n_pairs = w // lanes
    n_chunks = s // c
    n_ctx = ctx_len // c
    scale = head_dim ** -0.5

    rowi = lax.broadcasted_iota(jnp.int32, (c, lanes), 0).astype(F32)
    lane = lax.broadcasted_iota(jnp.int32, (c, lanes), 1)
    lane_even = (lane % 2) == 0
    low_half = lane < head_dim
    dif = (lax.broadcasted_iota(jnp.int32, (c, c), 0)
           - lax.broadcasted_iota(jnp.int32, (c, c), 1)).astype(F32)
    blk_row_low = lax.broadcasted_iota(jnp.int32, (lanes, lanes), 0) < head_dim
    blk_col_low = lax.broadcasted_iota(jnp.int32, (lanes, lanes), 1) < head_dim
    blk_diag = blk_row_low == blk_col_low

    order_b = list(range(n_ctx - 1, -1, -1)) + list(range(n_chunks - 1, n_ctx - 1, -1))

    for p in range(n_pairs):
        lsl = slice(p * lanes, (p + 1) * lanes)
        lg_f = lgl_ref[0:1, lsl]
        lg_b = lgl_ref[1:2, lsl]
        kd_f = jnp.exp(lg_f * (c - 1.0 - rowi))
        kd_b = jnp.exp(lg_b * rowi)
        qd_f = jnp.exp(lg_f * (rowi + 1.0))
        qd_b = jnp.exp(lg_b * (c - rowi))

        for ch in range(n_chunks):
            r0 = ch * c
            q = p_ref[0, r0:r0 + c, p * lanes:(p + 1) * lanes].astype(F32)
            k = p_ref[0, r0:r0 + c, w + p * lanes:w + (p + 1) * lanes].astype(F32)
            v = p_ref[0, r0:r0 + c, 2 * w + p * lanes:2 * w + (p + 1) * lanes]
            if ch >= n_ctx:
                t0 = r0 - ctx_len
                cos = cos_ref[t0:t0 + c, :]
                sin = sin_ref[t0:t0 + c, :]
                q = _rope(q, cos, sin, lane_even)
                k = _rope(k, cos, sin, lane_even)
            q_ref[r0:r0 + c, lsl] = (q * scale).astype(BF16)
            k_ref[r0:r0 + c, lsl] = k.astype(BF16)
            d_f = pl.dot((k * kd_f).astype(BF16), v, trans_a=True)
            d_b = pl.dot((k * kd_b).astype(BF16), v, trans_a=True)
            dlt_ref[0, ch] = jnp.where(blk_diag, d_f, 0.0)
            dlt_ref[1, ch] = jnp.where(blk_diag, d_b, 0.0)

        gc_f = jnp.exp(jnp.where(blk_row_low, lg_ref[layer, 0, 2 * p], lg_ref[layer, 0, 2 * p + 1]) * float(c))
        gc_b = jnp.exp(jnp.where(blk_row_low, lg_ref[layer, 1, 2 * p], lg_ref[layer, 1, 2 * p + 1]) * float(c))
        state = jnp.zeros((lanes, lanes), F32)
        for ch in range(n_chunks):
            st_ref[0, ch] = state.astype(BF16)
            state = gc_f * state + dlt_ref[0, ch]
        state = jnp.zeros((lanes, lanes), F32)
        for ch in order_b:
            st_ref[1, ch] = state.astype(BF16)
            state = gc_b * state + dlt_ref[1, ch]

        dmats = []
        for h in range(2):
            lgf_s = lg_ref[layer, 0, 2 * p + h]
            lgb_s = lg_ref[layer, 1, 2 * p + h]
            dmats.append(jnp.where(dif > 0, jnp.exp(lgf_s * jnp.maximum(dif, 0.0)),
                                   jnp.where(dif < 0, jnp.exp(lgb_s * jnp.maximum(-dif, 0.0)), 2.0)))

        def out_chunk(ch, _):
            r0 = pl.multiple_of(ch * c, c)
            rows = pl.ds(r0, c)
            q = q_ref[rows, lsl]
            k = k_ref[rows, lsl]
            v = p_ref[0, rows, 2 * w + p * lanes:2 * w + (p + 1) * lanes]
            g = p_ref[0, rows, 3 * w + p * lanes:3 * w + (p + 1) * lanes].astype(F32)
            zero = jnp.zeros_like(q)
            s0 = pl.dot(jnp.where(low_half, q, zero), k, trans_b=True)
            s1 = pl.dot(jnp.where(low_half, zero, q), k, trans_b=True)
            o0 = jnp.dot((s0 * dmats[0]).astype(BF16), v, preferred_element_type=F32)
            o1 = jnp.dot((s1 * dmats[1]).astype(BF16), v, preferred_element_type=F32)
            o = jnp.where(low_half, o0, o1)
            qf = q.astype(F32)
            o = o + jnp.dot((qf * qd_f).astype(BF16), st_ref[0, ch], preferred_element_type=F32)
            o = o + jnp.dot((qf * qd_b).astype(BF16), st_ref[1, ch], preferred_element_type=F32)
            oo = o * o
            ss0 = jnp.sum(jnp.where(low_half, oo, 0.0), axis=-1, keepdims=True)
            ss1 = jnp.sum(jnp.where(low_half, 0.0, oo), axis=-1, keepdims=True)
            inv = jnp.where(low_half, lax.rsqrt(ss0 / head_dim + EPS), lax.rsqrt(ss1 / head_dim + EPS))
            o_ref[0, rows, lsl] = (_silu(g) * (o * inv)).astype(o_ref.dtype)
            return 0

        lax.fori_loop(0, n_chunks, out_chunk, 0, unroll=3 if n_chunks % 3 == 0 else 1)


def _retention(p_ret, lg, lg_lane, cos, sin_signed, layer, ctx_len, head_dim):
    b, s, w4 = p_ret.shape
    w = w4 // 4
    n_chunks = s // RET_CHUNK
    return pl.pallas_call(
        functools.partial(_ret_kernel, layer=layer, ctx_len=ctx_len, seq=s, head_dim=head_dim),
        grid=(b,),
        in_specs=[
            pl.BlockSpec(memory_space=pltpu.SMEM),
            pl.BlockSpec((1, s, w4), lambda i: (i, 0, 0)),
            _const_spec(cos.shape), _const_spec(sin_signed.shape), _layer_spec(lg_lane, layer),
        ],
        out_specs=pl.BlockSpec((1, s, w), lambda i: (i, 0, 0)),
        out_shape=jax.ShapeDtypeStruct((b, s, w), BF16),
        scratch_shapes=[
            pltpu.VMEM((s, w), BF16),
            pltpu.VMEM((s, w), BF16),
            pltpu.VMEM((2, n_chunks, V7X_LANES, V7X_LANES), F32),
            pltpu.VMEM((2, n_chunks, V7X_LANES, V7X_LANES), BF16),
        ],
        compiler_params=_cparams("parallel"),
        name="retention",
    )(lg, p_ret, cos, sin_signed, lg_lane)


def _cm_kernel(z_ref, lng_ref, lnb_ref, ws_ref, bs_ref, o_ref, *, seq, group_dim):
    c = CM_CHUNK
    lanes = V7X_LANES
    w = o_ref.shape[2]
    n_slabs = w // lanes
    low_half = lax.broadcasted_iota(jnp.int32, (c, lanes), 1) < group_dim
    lane_w = lax.broadcasted_iota(jnp.int32, (c, w), 1)

    def body(ch, _):
        r0 = pl.multiple_of(ch * c, c)
        rows = pl.ds(r0, c)
        z = _gelu_tanh(z_ref[0, rows, :].astype(F32))
        u = z[:, :w]
        vn = []
        for sl in range(n_slabs):
            v = z[:, w + sl * lanes:w + (sl + 1) * lanes]
            m0 = jnp.sum(jnp.where(low_half, v, 0.0), axis=-1, keepdims=True) / group_dim
            m1 = jnp.sum(jnp.where(low_half, 0.0, v), axis=-1, keepdims=True) / group_dim
            xc = v - jnp.where(low_half, m0, m1)
            xx = xc * xc
            v0 = jnp.sum(jnp.where(low_half, xx, 0.0), axis=-1, keepdims=True) / group_dim
            v1 = jnp.sum(jnp.where(low_half, 0.0, xx), axis=-1, keepdims=True) / group_dim
            vn.append(xc * jnp.where(low_half, lax.rsqrt(v0 + EPS), lax.rsqrt(v1 + EPS)))
        vn = jnp.concatenate(vn, axis=-1) * lng_ref[...] + lnb_ref[...]
        s_all = jnp.dot(ws_ref[...], vn.astype(BF16), preferred_element_type=F32)
        sp = s_all[0:c]
        for gi in range(1, CM_GROUPS):
            sp = jnp.where(lane_w >= gi * group_dim, s_all[gi * c:(gi + 1) * c], sp)
        o_ref[0, rows, :] = (u * (sp + bs_ref[...])).astype(o_ref.dtype)
        return 0

    lax.fori_loop(0, seq // c, body, 0, unroll=3 if (seq // c) % 3 == 0 else 1)


def _chunk_mlp(p_cm, ln_g, ln_b, ws, bs, layer, group_dim):
    b, s, w2 = p_cm.shape
    w = w2 // 2
    return pl.pallas_call(
        functools.partial(_cm_kernel, seq=s, group_dim=group_dim),
        grid=(b,),
        in_specs=[pl.BlockSpec((1, s, w2), lambda i: (i, 0, 0))]
        + [_layer_spec(a, layer) for a in (ln_g, ln_b, ws, bs)],
        out_specs=pl.BlockSpec((1, s, w), lambda i: (i, 0, 0)),
        out_shape=jax.ShapeDtypeStruct((b, s, w), BF16),
        compiler_params=_cparams("parallel"),
        name="chunk_gmlp",
    )(p_cm, ln_g, ln_b, ws, bs)


def _swiglu_rows(h, w1_ref, w3_ref, w2_ref):
    f = w1_ref.shape[1]
    acc = None
    for c0 in range(0, f, FF_CHUNK):
        a = jnp.dot(h, w1_ref[:, c0:c0 + FF_CHUNK], preferred_element_type=F32)
        g = jnp.dot(h, w3_ref[:, c0:c0 + FF_CHUNK], preferred_element_type=F32)
        act = (_silu(a) * g).astype(BF16)
        part = jnp.dot(act, w2_ref[c0:c0 + FF_CHUNK, :], preferred_element_type=F32)
        acc = part if acc is None else acc + part
    return acc


def _ffn_kernel(*refs, n_src, ctx_rows):
    src = refs[:n_src]
    (gmix_ref, gpre_ref, gpost_ref, ml_ref, mc_ref, pa_ref, pb_ref, pc_ref, wo_ref,
     w1_ref, w3_ref, w2_ref, o_ref, xm_ref, h_ref) = refs[n_src:n_src + 15]
    x_ref = _residual_rows(src, refs[-1], ctx_rows)
    y = _mix_out((pa_ref, pb_ref, pc_ref), wo_ref)
    _gated_residual(x_ref[...], y, gmix_ref, ml_ref, mc_ref, xm_ref, ctx_rows, 2)
    _prenorm_modulate(xm_ref[...], gpre_ref, ml_ref, mc_ref, h_ref, ctx_rows, 3)
    y = _swiglu_rows(h_ref[...], w1_ref, w3_ref, w2_ref)
    _gated_residual(xm_ref[...], y, gpost_ref, ml_ref, mc_ref, o_ref, ctx_rows, 5)


def _mix_dense_ffn(src, rows, gains, mod_all, parts, w_out_b, w1, w3, w2, layer, j):
    d = src[0].shape[1]
    ml_spec, mc_spec = rows.mod_specs(mod_all, layer)
    src_specs, src_scratch = _source_specs(rows, src, d)
    return pl.pallas_call(
        functools.partial(_ffn_kernel, n_src=len(src), ctx_rows=rows.ctx_rows),
        grid=rows.grid,
        in_specs=src_specs + [_layer_spec(g, layer) for g in gains[1:]] + [ml_spec, mc_spec]
        + [rows.slab(a.shape[1]) for a in parts] + [_layer_spec(w_out_b, layer)]
        + [_layer_spec(w, j) for w in (w1, w3, w2)],
        out_specs=rows.slab(d),
        out_shape=jax.ShapeDtypeStruct((rows.batch * rows.seq, d), F32),
        scratch_shapes=[pltpu.VMEM((rows.rb, d), F32), pltpu.VMEM((rows.rb, d), BF16)] + src_scratch,
        input_output_aliases={0: 0} if len(src) == 1 else {},
        compiler_params=_cparams("parallel", "parallel"),
        name="mix_dense_swiglu",
    )(*src, *gains[1:], mod_all, mod_all, *parts, w_out_b, w1, w3, w2)


def _router_kernel(x_ref, gmix_ref, g_ref, ml_ref, mc_ref, pa_ref, pb_ref, pc_ref, wo_ref, wr_ref,
                   xo_ref, h_ref, idx_ref, wgt_ref, hf_ref, *, ctx_rows, n_experts):
    y = _mix_out((pa_ref, pb_ref, pc_ref), wo_ref)
    _gated_residual(x_ref[...], y, gmix_ref, ml_ref, mc_ref, xo_ref, ctx_rows, 2)
    _prenorm_modulate(xo_ref[...], g_ref, ml_ref, mc_ref, hf_ref, ctx_rows, 3)
    h = hf_ref[...]
    h_ref[...] = h.astype(h_ref.dtype)
    logits = jnp.dot(h, wr_ref[...], preferred_element_type=F32)
    lane = lax.broadcasted_iota(jnp.int32, logits.shape, 1).astype(F32)
    neg_inf = jnp.float32(-jnp.inf)
    logits = jnp.where(lane < n_experts, logits, neg_inf)
    big = jnp.float32(V7X_LANES)
    m1 = jnp.max(logits, axis=-1, keepdims=True)
    i1 = jnp.min(jnp.where(logits == m1, lane, big), axis=-1, keepdims=True)
    rest = jnp.where(lane == i1, neg_inf, logits)
    m2 = jnp.max(rest, axis=-1, keepdims=True)
    i2 = jnp.min(jnp.where(rest == m2, lane, big), axis=-1, keepdims=True)
    e2 = jnp.exp(m2 - m1)
    w1 = 1.0 / (1.0 + e2)
    w2 = e2 / (1.0 + e2)
    out_lane = lax.broadcasted_iota(jnp.int32, idx_ref.shape, 1)
    idx_ref[...] = jnp.where(out_lane == 0, i1, i2).astype(jnp.int32)
    wgt_ref[...] = jnp.where(out_lane == 0, w1, w2)


def _mix_router(xs2, rows, gains, mod_all, parts, w_out_b, wr_pad, layer, j, n_experts):
    d = xs2.shape[1]
    n_walk = rows.batch * rows.rows
    ml_spec, mc_spec = rows.mod_specs(mod_all, layer)
    if rows.latent_only:
        x_spec, x_shape, aliases = rows.walked(d), (n_walk, d), {}
    else:
        x_spec, x_shape, aliases = rows.slab(d), xs2.shape, {0: 0}
    return pl.pallas_call(
        functools.partial(_router_kernel, ctx_rows=rows.ctx_rows, n_experts=n_experts),
        grid=rows.grid,
        in_specs=[rows.slab(d), _layer_spec(gains[1], layer), _layer_spec(gains[2], layer), ml_spec, mc_spec]
        + [rows.slab(a.shape[1]) for a in parts]
        + [_layer_spec(w_out_b, layer), _layer_spec(wr_pad, j)],
        out_specs=[x_spec, rows.walked(d), rows.walked(V7X_SUBLANES), rows.walked(V7X_SUBLANES)],
        out_shape=[jax.ShapeDtypeStruct(x_shape, F32),
                   jax.ShapeDtypeStruct((n_walk, d), BF16),
                   jax.ShapeDtypeStruct((n_walk, V7X_SUBLANES), jnp.int32),
                   jax.ShapeDtypeStruct((n_walk, V7X_SUBLANES), F32)],
        scratch_shapes=[pltpu.VMEM((rows.rb, d), F32)],
        input_output_aliases=aliases,
        compiler_params=_cparams("parallel", "parallel"),
        name="mix_moe_router",
    )(xs2, gains[1], gains[2], mod_all, mod_all, *parts, w_out_b, wr_pad)


def _grouped_kernel(te_ref, tv_ref, h_ref, w1_ref, w3_ref, w2_ref, o_ref):
    i = pl.program_id(0)

    @pl.when(tv_ref[i] != 0)
    def _():
        o_ref[...] = _swiglu_rows(h_ref[...], w1_ref, w3_ref, w2_ref).astype(o_ref.dtype)

    @pl.when(tv_ref[i] == 0)
    def _():
        o_ref[...] = jnp.zeros_like(o_ref)


def _grouped_swiglu(layer_idx, tile_expert, tile_valid, hs, w1, w3, w2):
    p, d = hs.shape
    f = w1.shape[3]
    tm = MOE_TILE
    grid_spec = pltpu.PrefetchScalarGridSpec(
        num_scalar_prefetch=2,
        grid=(p // tm,),
        in_specs=[
            pl.BlockSpec((tm, d), lambda i, te, tv: (i, 0)),
            pl.BlockSpec((None, None, d, f), lambda i, te, tv: (layer_idx, te[i], 0, 0)),
            pl.BlockSpec((None, None, d, f), lambda i, te, tv: (layer_idx, te[i], 0, 0)),
            pl.BlockSpec((None, None, f, d), lambda i, te, tv: (layer_idx, te[i], 0, 0)),
        ],
        out_specs=pl.BlockSpec((tm, d), lambda i, te, tv: (i, 0)),
    )
    return pl.pallas_call(
        _grouped_kernel,
        grid_spec=grid_spec,
        out_shape=jax.ShapeDtypeStruct((p, d), BF16),
        compiler_params=_cparams("arbitrary"),
        name="moe_grouped_swiglu",
    )(tile_expert, tile_valid, hs, w1, w3, w2)


def _combine_kernel(*refs, ctx_rows):
    x_ref, g_ref, ml_ref, mc_ref, y0_ref, y1_ref, wgt_ref = refs[:7]
    o_ref = refs[-1]
    wgt = wgt_ref[...]
    y = wgt[:, 0:1] * y0_ref[...].astype(F32) + wgt[:, 1:2] * y1_ref[...].astype(F32)
    _gated_residual(x_ref[...], y, g_ref, ml_ref, mc_ref, o_ref, ctx_rows, 5)


def _combine(xres, g_all, mod_all, y0, y1, wgt, layer, rows, dest):
    d = xres.shape[1]
    ml_spec, mc_spec = rows.mod_specs(mod_all, layer)
    x_spec = rows.slab(d) if dest is None else rows.walked(d)
    in_specs = [x_spec, _layer_spec(g_all, layer), ml_spec, mc_spec,
                rows.local(d), rows.local(d), rows.walked(V7X_SUBLANES)]
    args = [xres, g_all, mod_all, mod_all, y0, y1, wgt]
    if dest is None:
        out_spec, out_shape, aliases = rows.slab(d), jax.ShapeDtypeStruct(xres.shape, F32), {0: 0}
    else:
        out_spec, out_shape, aliases = rows.walked(d), jax.ShapeDtypeStruct(dest.shape, F32), {}
        if not isinstance(dest, jax.ShapeDtypeStruct):
            in_specs.append(pl.BlockSpec(memory_space=pl.ANY))
            args.append(dest)
            aliases = {len(args) - 1: 0}
    return pl.pallas_call(
        functools.partial(_combine_kernel, ctx_rows=rows.ctx_rows),
        grid=rows.grid,
        in_specs=in_specs,
        out_specs=out_spec,
        out_shape=out_shape,
        input_output_aliases=aliases,
        compiler_params=_cparams("parallel", "parallel"),
        name="moe_combine",
    )(*args)


def _gather_rows(src, idx):
    return src.at[idx].get(mode="promise_in_bounds")


def _routing_tables(e_flat, tok_flat, n_exp):
    n_assign = e_flat.shape[0]
    tm = MOE_TILE
    onehot = (e_flat[:, None] == jnp.arange(n_exp, dtype=jnp.int32)[None, :]).astype(jnp.int32)
    csum = jnp.cumsum(onehot, axis=0)
    counts = csum[-1]
    rank = jnp.sum((csum - onehot) * onehot, axis=1)
    padded = ((counts + tm - 1) // tm) * tm
    ends = jnp.cumsum(padded)
    starts = ends - padded
    pos = jnp.sum(onehot * starts[None, :], axis=1) + rank
    n_rows = n_assign + n_exp * tm
    n_tiles = n_rows // tm
    src_tok = jnp.zeros((n_rows,), jnp.int32).at[pos].set(
        tok_flat, unique_indices=True, mode="promise_in_bounds")
    tile_start = jnp.arange(n_tiles, dtype=jnp.int32) * tm
    tile_expert = jnp.minimum(jnp.sum((tile_start[:, None] >= ends[None, :]).astype(jnp.int32), axis=1),
                              n_exp - 1)
    tile_valid = (tile_start < ends[-1]).astype(jnp.int32)
    return pos, src_tok, tile_expert, tile_valid


def _mix_moe_ffn(xs2, shape, gains, mod_all, parts, w_out_b, wr_pad, w1, w3, w2, layer, j, n_exp, last):
    b, s, ctx_len = shape
    d = xs2.shape[1]
    rows = _Rows(b, s, ctx_len, latent_only=last)
    xres, h, idx, wgt = _mix_router(xs2, rows, gains, mod_all, parts, w_out_b, wr_pad, layer, j, n_exp)
    r = rows.rows
    bh = b // MOE_SPLITS
    out = jax.ShapeDtypeStruct((b * r, d), F32) if last else None
    for g in range(MOE_SPLITS):
        e_flat = idx[g * bh * r:(g + 1) * bh * r, :TOP_K].reshape(bh * r * TOP_K)
        tok_flat = g * bh * r + jnp.arange(bh * r * TOP_K, dtype=jnp.int32) // TOP_K
        pos, src_tok, tile_expert, tile_valid = _routing_tables(e_flat, tok_flat, n_exp)
        hs = _gather_rows(h, src_tok)
        ys = _grouped_swiglu(j, tile_expert, tile_valid, hs, w1, w3, w2)
        pos = pos.reshape(bh * r, TOP_K)
        y0 = _gather_rows(ys, pos[:, 0])
        y1 = _gather_rows(ys, pos[:, 1])
        rows_g = _Rows(b, s, ctx_len, latent_only=last, batch0=g * bh, n_batch=bh)
        res = _combine(xres, gains[3], mod_all, y0, y1, wgt, layer, rows_g, out)
        if last:
            out = res
        else:
            xres = res
    return out if last else xres


def _block_diag(wh):
    h, d = wh.shape[-3], wh.shape[-1]
    eye = jnp.eye(h, dtype=wh.dtype)
    out = jnp.einsum("...hij,hg->...higj", wh, eye)
    return out.reshape(wh.shape[:-3] + (h * d, h * d))


def kernel(x, c, ctx, c_ctx, w_mod, b_mod, g_mix_pre, g_mix_post, g_ffn_pre, g_ffn_post, w_in, w_out,
           lru_conv_w, lru_conv_b, lru_wa, lru_ba, lru_wx, lru_bx, lru_lam, ret_theta,
           cm_ln_g, cm_ln_b, cm_ws, cm_bs, ffn_w1, ffn_w3, ffn_w2, router_w, moe_w1, moe_w3, moe_w2):
    b, l, d = x.shape
    lc = ctx.shape[1]
    s = lc + l
    depth = w_mod.shape[0]
    lru_w = lru_conv_w.shape[2]
    cm_w = cm_ln_g.shape[1] * cm_ln_g.shape[2]
    ret_w = (w_in.shape[2] - 2 * lru_w - 2 * cm_w) // 4
    head_dim = ret_w // RET_HEADS
    group_dim = cm_ln_g.shape[2]
    n_exp = router_w.shape[2]
    assert s % ROW_BLOCK == 0 and lc < ROW_BLOCK and lc % RET_CHUNK == 0 and l % RET_CHUNK == 0
    assert ret_w % V7X_LANES == 0 and cm_w % V7X_LANES == 0 and 2 * head_dim == V7X_LANES
    assert b % MOE_SPLITS == 0 and l % lc == 0 and l >= ROW_BLOCK

    n_rows = l // GRID_W
    rows = jnp.repeat(jnp.arange(n_rows, dtype=F32), GRID_W)
    cols = jnp.tile(jnp.arange(GRID_W, dtype=F32), n_rows)
    pairs = head_dim // 4
    freqs = ROPE_BASE ** (-jnp.arange(pairs, dtype=F32) / pairs)
    ang = jnp.concatenate([rows[:, None] * freqs, cols[:, None] * freqs], axis=-1)
    cos = jnp.tile(jnp.repeat(jnp.cos(ang), 2, axis=1), (1, V7X_LANES // head_dim))
    sin = jnp.repeat(jnp.sin(ang), 2, axis=1) * jnp.tile(jnp.array([-1.0, 1.0], F32), head_dim // 2)
    sin_signed = jnp.tile(sin, (1, V7X_LANES // head_dim))

    w_in_b, w_out_b = w_in.astype(BF16), w_out.astype(BF16)
    ffn_b1, ffn_b3, ffn_b2 = ffn_w1.astype(BF16), ffn_w3.astype(BF16), ffn_w2.astype(BF16)
    moe_b1, moe_b3, moe_b2 = moe_w1.astype(BF16), moe_w3.astype(BF16), moe_w2.astype(BF16)
    wg = (0.5 * jnp.concatenate([_block_diag(lru_wa[:, 0]), _block_diag(lru_wx[:, 0]),
                                 _block_diag(lru_wa[:, 1]), _block_diag(lru_wx[:, 1])], axis=2)).astype(BF16)
    bg = 0.5 * jnp.concatenate([lru_ba[:, 0].reshape(depth, 1, lru_w), lru_bx[:, 0].reshape(depth, 1, lru_w),
                                lru_ba[:, 1].reshape(depth, 1, lru_w), lru_bx[:, 1].reshape(depth, 1, lru_w)],
                               axis=2)
    conv_b = lru_conv_b.reshape(depth, 1, lru_w)
    lam = lru_lam.reshape(depth, 2, lru_w)
    lg = jax.nn.log_sigmoid(ret_theta.astype(F32))
    lg_lane = jnp.repeat(lg, head_dim, axis=2)
    ln_g = cm_ln_g.reshape(depth, 1, cm_w)
    ln_b = cm_ln_b.reshape(depth, 1, cm_w)
    ws = cm_ws.reshape(depth, CM_GROUPS * CM_CHUNK, CM_CHUNK).astype(BF16)
    bs = jnp.repeat(jnp.swapaxes(cm_bs, 1, 2), group_dim, axis=2)
    wr_pad = jnp.zeros((router_w.shape[0], d, V7X_LANES), F32).at[:, :, :n_exp].set(router_w)
    gains = [g.reshape(depth, 1, d) for g in (g_mix_pre, g_mix_post, g_ffn_pre, g_ffn_post)]
    widths = (2 * lru_w, 4 * ret_w, 2 * cm_w)

    n_cond = ((b + 1 + V7X_SUBLANES - 1) // V7X_SUBLANES) * V7X_SUBLANES
    cpad = jnp.zeros((n_cond, d), F32).at[:b].set(c).at[b].set(c_ctx)
    mod_all = _modulation(cpad, w_mod, b_mod).reshape(depth, n_cond, N_MOD, d)

    rows = _Rows(b, s, lc)
    src = (ctx.reshape(b * lc, d), x.reshape(b * l, d))
    for layer in range(depth):
        p_lru, p_ret, p_cm = _inproj(src, rows, gains[0], mod_all, w_in_b, layer, widths)
        o_lru = _lru(p_lru.reshape(b, s, -1), lru_conv_w, conv_b, wg, bg, lam, layer, lc)
        o_ret = _retention(p_ret.reshape(b, s, -1), lg, lg_lane, cos, sin_signed, layer, lc, head_dim)
        o_cm = _chunk_mlp(p_cm.reshape(b, s, -1), ln_g, ln_b, ws, bs, layer, group_dim)
        parts = tuple(o.reshape(b * s, -1) for o in (o_lru, o_ret, o_cm))

        j = layer // 2
        last = layer == depth - 1
        if layer % 2 == 0:
            xs2 = _mix_dense_ffn(src, rows, gains, mod_all, parts, w_out_b, ffn_b1, ffn_b3, ffn_b2, layer, j)
        else:
            xs2 = _mix_moe_ffn(src[0], (b, s, lc), gains, mod_all, parts, w_out_b, wr_pad,
                               moe_b1, moe_b3, moe_b2, layer, j, n_exp, last)
            if last:
                return xs2.reshape(b, l, d)
        src = (xs2,)
    return xs2.reshape(b, s, d)[:, lc:, :]
```

```python
import functools
import math

import jax
import jax.numpy as jnp
from jax import lax
from jax.experimental import pallas as pl
from jax.experimental.pallas import tpu as pltpu

F32 = jnp.float32
BF16 = jnp.bfloat16

EPS = 1e-6
N_MOD = 6
LRU_CONV = 4
LRU_C = 8.0
RET_HEADS = 6
ROPE_BASE = 100.0
GRID_W = 64
CM_GROUPS = 4
CM_CHUNK = 128
TOP_K = 2

V7X_LANES = 128
V7X_SUBLANES = 8
V7X_VMEM_LIMIT = 56 * 1024 * 1024

ROW_BLOCK = 768
RET_CHUNK = 256
FF_CHUNK = 256
MOE_TILE = 512
LAT_BLOCK = 1024
MOE_SPLITS = 2
LOG2_E = 1.4426950408889634
TINY = 1e-30


def _cparams(*sem):
    return pltpu.CompilerParams(dimension_semantics=sem, vmem_limit_bytes=V7X_VMEM_LIMIT)


def _layer_spec(arr, layer):
    nd = arr.ndim - 1
    return pl.BlockSpec((None,) + arr.shape[1:], lambda *_: (layer,) + (0,) * nd,
                        pipeline_mode=pl.Buffered(1))


def _const_spec(shape):
    nd = len(shape)
    return pl.BlockSpec(shape, lambda *_: (0,) * nd, pipeline_mode=pl.Buffered(1))


def _rms(x):
    return x * lax.rsqrt(jnp.mean(x * x, axis=-1, keepdims=True) + EPS)


def _gelu_tanh(x):
    return 0.5 * x * (1.0 + jnp.tanh(0.7978845608028654 * (x + 0.044715 * (x * x * x))))


def _sigmoid(x):
    return 1.0 / (1.0 + jnp.exp(-x))


def _silu(x):
    return x * _sigmoid(x)


def _mod_kernel(c_ref, w_ref, b_ref, o_ref):
    s = _silu(c_ref[...])
    o_ref[...] = jnp.dot(s, w_ref[...], preferred_element_type=F32) + b_ref[...]


def _modulation(cpad, w_mod, b_mod):
    depth, d, nd = w_mod.shape
    r = cpad.shape[0]
    return pl.pallas_call(
        _mod_kernel,
        grid=(depth, nd // d),
        in_specs=[
            pl.BlockSpec((r, d), lambda l, n: (0, 0)),
            pl.BlockSpec((None, d, d), lambda l, n: (l, 0, n)),
            pl.BlockSpec((None, 1, d), lambda l, n: (l, 0, n)),
        ],
        out_specs=pl.BlockSpec((None, r, d), lambda l, n: (l, 0, n)),
        out_shape=jax.ShapeDtypeStruct((depth, r, nd), F32),
        compiler_params=_cparams("parallel", "parallel"),
        name="modulation",
    )(cpad, w_mod, b_mod.reshape(depth, 1, nd))


class _Rows:
    def __init__(self, batch, seq, ctx_len, latent_only=False, batch0=0, n_batch=None):
        self.batch, self.seq, self.ctx_len = batch, seq, ctx_len
        self.lat_len = seq - ctx_len
        self.batch0 = batch0
        self.n_batch = batch if n_batch is None else n_batch
        self.latent_only = latent_only
        if latent_only:
            self.rb = LAT_BLOCK if self.lat_len % LAT_BLOCK == 0 else ctx_len
            self.n_blk = self.lat_len // self.rb
            self.ctx_rows = 0
        else:
            self.rb = ROW_BLOCK
            self.n_blk = seq // ROW_BLOCK
            self.ctx_rows = ctx_len
        self.rows = self.rb * self.n_blk
        self.align = math.gcd(self.rb, ctx_len)

    @property
    def grid(self):
        return (self.n_batch, self.n_blk)

    def slab(self, width):
        if not self.latent_only:
            return pl.BlockSpec((self.rb, width), lambda b, j: ((b + self.batch0) * self.n_blk + j, 0))

        def index(b, j):
            row = (b + self.batch0) * self.seq + self.ctx_len + j * self.rb
            return (pl.multiple_of(row, self.align), 0)
        return pl.BlockSpec((pl.Element(self.rb), pl.Element(width)), index)

    def walked(self, width):
        return pl.BlockSpec((self.rb, width), lambda b, j: ((b + self.batch0) * self.n_blk + j, 0))

    def local(self, width):
        return pl.BlockSpec((self.rb, width), lambda b, j: (b * self.n_blk + j, 0))

    def split_source(self, width):
        assert not self.latent_only and self.lat_len >= self.rb
        ctx = pl.BlockSpec((self.ctx_len, width), lambda b, j: (b + self.batch0, 0))

        def index(b, j):
            row = (b + self.batch0) * self.lat_len + jnp.maximum(j * self.rb - self.ctx_len, 0)
            return (pl.multiple_of(row, self.align), 0)
        return ctx, pl.BlockSpec((pl.Element(self.rb), pl.Element(width)), index)

    def mod_specs(self, mod_all, layer):
        d = mod_all.shape[-1]
        lat = pl.BlockSpec((None, 1, N_MOD, d), lambda b, j: (layer, b + self.batch0, 0, 0))
        ctx = pl.BlockSpec((None, 1, N_MOD, d), lambda b, j: (layer, self.batch, 0, 0))
        return lat, ctx


def _row_mod(ml_ref, mc_ref, idx, first_block):
    lat = ml_ref[0, idx:idx + 1, :]
    top = jnp.where(first_block, mc_ref[0, idx:idx + 1, :], lat)
    return top, lat


def _prenorm_modulate(x, g_ref, ml_ref, mc_ref, h_ref, ctx_rows, shift_idx):
    first = pl.program_id(1) == 0
    sh_top, sh_lat = _row_mod(ml_ref, mc_ref, shift_idx, first)
    sc_top, sc_lat = _row_mod(ml_ref, mc_ref, shift_idx + 1, first)
    y = _rms(x) * g_ref[...]
    if ctx_rows:
        h_ref[:ctx_rows, :] = (y[:ctx_rows] * (1.0 + sc_top) + sh_top).astype(h_ref.dtype)
    h_ref[ctx_rows:, :] = (y[ctx_rows:] * (1.0 + sc_lat) + sh_lat).astype(h_ref.dtype)


def _gated_residual(x, y, g_ref, ml_ref, mc_ref, o_ref, ctx_rows, gate_idx):
    first = pl.program_id(1) == 0
    gt_top, gt_lat = _row_mod(ml_ref, mc_ref, gate_idx, first)
    r = _rms(y) * g_ref[...]
    if ctx_rows:
        o_ref[:ctx_rows, :] = x[:ctx_rows] + gt_top * r[:ctx_rows]
    o_ref[ctx_rows:, :] = x[ctx_rows:] + gt_lat * r[ctx_rows:]


def _residual_rows(src_refs, xbuf_ref, ctx_rows):
    if len(src_refs) == 1:
        return src_refs[0]
    ctx_ref, lat_ref = src_refs
    first = pl.program_id(1) == 0

    @pl.when(first)
    def _():
        xbuf_ref[:ctx_rows, :] = ctx_ref[...]
        xbuf_ref[ctx_rows:, :] = lat_ref[:xbuf_ref.shape[0] - ctx_rows, :]

    @pl.when(jnp.logical_not(first))
    def _():
        xbuf_ref[...] = lat_ref[...]

    return xbuf_ref


def _source_specs(rows, src, d):
    if len(src) == 1:
        return [rows.slab(d)], []
    return list(rows.split_source(d)), [pltpu.VMEM((rows.rb, d), F32)]


def _mix_out(part_refs, w_ref, cat_ref):
    c0 = 0
    for part in part_refs:
        n = part.shape[1]
        cat_ref[:, c0:c0 + n] = part[...]
        c0 += n
    return jnp.dot(cat_ref[...], w_ref[...], preferred_element_type=F32)


def _inproj_kernel(*refs, n_src, ctx_rows):
    src, (g_ref, ml_ref, mc_ref, w_ref, ol_ref, or_ref, oc_ref, h_ref) = refs[:n_src], refs[n_src:n_src + 8]
    x_ref = _residual_rows(src, refs[-1], ctx_rows)
    _prenorm_modulate(x_ref[...], g_ref, ml_ref, mc_ref, h_ref, ctx_rows, 0)
    h = h_ref[...]
    c0 = 0
    for o_ref in (ol_ref, or_ref, oc_ref):
        n = o_ref.shape[1]
        o_ref[...] = jnp.dot(h, w_ref[:, c0:c0 + n], preferred_element_type=F32).astype(o_ref.dtype)
        c0 += n


def _inproj(src, rows, g_all, mod_all, w_in_b, layer, widths):
    d = src[0].shape[1]
    ml_spec, mc_spec = rows.mod_specs(mod_all, layer)
    src_specs, src_scratch = _source_specs(rows, src, d)
    n_total = rows.batch * rows.seq
    return pl.pallas_call(
        functools.partial(_inproj_kernel, n_src=len(src), ctx_rows=rows.ctx_rows),
        grid=rows.grid,
        in_specs=src_specs + [_layer_spec(g_all, layer), ml_spec, mc_spec, _layer_spec(w_in_b, layer)],
        out_specs=[rows.slab(n) for n in widths],
        out_shape=[jax.ShapeDtypeStruct((n_total, n), BF16) for n in widths],
        scratch_shapes=[pltpu.VMEM((rows.rb, d), BF16)] + src_scratch,
        compiler_params=_cparams("parallel", "parallel"),
        name="mixer_inproj",
    )(*src, g_all, mod_all, mod_all, w_in_b)


def _tile_scan(a, u, carry, row, reverse):
    n = V7X_SUBLANES
    for dist in (1, 2, 4):
        if reverse:
            keep = row < (n - dist)
            shift = n - dist
        else:
            keep = row >= dist
            shift = dist
        a_s = jnp.where(keep, pltpu.roll(a, shift, 0), 1.0)
        u_s = jnp.where(keep, pltpu.roll(u, shift, 0), 0.0)
        u = a * u_s + u
        a = a * a_s
    h = a * carry + u
    new_carry = h[0:1] if reverse else h[n - 1:n]
    return h, new_carry


def _lru_kernel(p_ref, cw_ref, cb_ref, wg_ref, bg_ref, lam_ref, o_ref,
                xpad_ref, a_ref, u_ref, h_ref, *, ctx_len, seq):
    w = cw_ref.shape[1]
    s = seq
    pad = V7X_SUBLANES
    chunk = ctx_len
    lanes = V7X_LANES
    n_slabs = w // lanes
    zeros = jnp.zeros((pad, lanes), F32)
    for j in range(n_slabs):
        xpad_ref[j, 0:pad, :] = zeros
        xpad_ref[j, pad + ctx_len:2 * pad + ctx_len, :] = zeros
        xpad_ref[j, 2 * pad + s:3 * pad + s, :] = zeros
        xpad_ref[j, pad:pad + ctx_len, :] = p_ref[0, 0:ctx_len, j * lanes:(j + 1) * lanes].astype(F32)
        xpad_ref[j, 2 * pad + ctx_len:2 * pad + s, :] = p_ref[0, ctx_len:s, j * lanes:(j + 1) * lanes].astype(F32)

    neg = -lam_ref[...]
    softplus = jnp.maximum(neg, 0.0) + jnp.log(1.0 + jnp.exp(-jnp.abs(neg)))
    c2 = (-0.5 * LRU_C * LOG2_E) * softplus

    pitch = a_ref.shape[2] // V7X_SUBLANES
    lat_len = s - ctx_len

    def phys(d, r0):
        if d == 0:
            return r0
        return lat_len + r0 if r0 < ctx_len else r0 - ctx_len

    for d in range(2):
        for j in range(n_slabs):
            a_ref[d, j, s:, :] = jnp.ones((a_ref.shape[2] - s, lanes), F32)
            u_ref[d, j, s:, :] = jnp.zeros((a_ref.shape[2] - s, lanes), F32)

    for c in range(s // chunk):
        r0 = c * chunk
        base = r0 + (pad if r0 < ctx_len else 2 * pad)
        xc = []
        for j in range(n_slabs):
            lsl = slice(j * lanes, (j + 1) * lanes)
            acc = jnp.zeros((chunk, lanes), F32) + cb_ref[:, lsl]
            for k in range(LRU_CONV):
                off = k - LRU_CONV // 2
                acc = acc + xpad_ref[j, base + off:base + off + chunk, :] * cw_ref[k:k + 1, lsl]
            xc.append(acc)
        xc = jnp.concatenate(xc, axis=-1)
        th = jnp.tanh(jnp.dot(xc.astype(BF16), wg_ref[...], preferred_element_type=F32) + bg_ref[...])
        hx = 0.5 * xc
        for d in range(2):
            th_r = th[:, (2 * d) * w:(2 * d + 1) * w]
            th_i = th[:, (2 * d + 1) * w:(2 * d + 2) * w]
            cd = c2[d:d + 1, :]
            a = jnp.exp2(cd * th_r + cd)
            y = 1.0 - a * a
            u = (y * lax.rsqrt(jnp.maximum(y, TINY))) * (hx * th_i + hx)
            pr = phys(d, r0)
            for j in range(n_slabs):
                a_ref[d, j, pr:pr + chunk, :] = a[:, j * lanes:(j + 1) * lanes]
                u_ref[d, j, pr:pr + chunk, :] = u[:, j * lanes:(j + 1) * lanes]

    row = lax.broadcasted_iota(jnp.int32, (V7X_SUBLANES, lanes), 0)
    chains = [(d, j) for d in range(2) for j in range(n_slabs)]

    def seg_rows(d, t):
        tt = t if d == 0 else pitch - 1 - t
        return pl.ds(tt, V7X_SUBLANES, stride=pitch)

    def pass1(t, carry):
        out = []
        for (d, j), (h, pprod) in zip(chains, carry):
            a = a_ref[d, j, seg_rows(d, t), :]
            u = u_ref[d, j, seg_rows(d, t), :]
            out.append((a * h + u, a * pprod))
        return tuple(out)

    zero = jnp.zeros((V7X_SUBLANES, lanes), F32)
    ends = lax.fori_loop(0, pitch, pass1, tuple((zero, zero + 1.0) for _ in chains), unroll=4)

    starts = []
    for (d, j), (h_end, p_tot) in zip(chains, ends):
        incl, _ = _tile_scan(p_tot, h_end, jnp.zeros((1, lanes), F32), row, d == 1)
        if d == 0:
            starts.append(jnp.where(row >= 1, pltpu.roll(incl, 1, 0), 0.0))
        else:
            starts.append(jnp.where(row < V7X_SUBLANES - 1, pltpu.roll(incl, V7X_SUBLANES - 1, 0), 0.0))

    def pass2(t, carry):
        out = []
        for (d, j), h in zip(chains, carry):
            a = a_ref[d, j, seg_rows(d, t), :]
            u = u_ref[d, j, seg_rows(d, t), :]
            h = a * h + u
            h_ref[d, j, seg_rows(d, t), :] = h
            out.append(h)
        return tuple(out)

    lax.fori_loop(0, pitch, pass2, tuple(starts), unroll=4)

    for c in range(s // chunk):
        r0 = c * chunk
        rb = phys(1, r0)
        for j in range(n_slabs):
            ly = p_ref[0, r0:r0 + chunk, w + j * lanes:w + (j + 1) * lanes].astype(F32)
            h = h_ref[0, j, r0:r0 + chunk, :] + h_ref[1, j, rb:rb + chunk, :]
            o_ref[0, r0:r0 + chunk, j * lanes:(j + 1) * lanes] = (_gelu_tanh(ly) * h).astype(o_ref.dtype)


def _lru(p_lru, conv_w, conv_b, wg, bg, lam, layer, ctx_len):
    b, s, w2 = p_lru.shape
    w = w2 // 2
    assert s % (V7X_SUBLANES * V7X_SUBLANES) == 0
    scan_rows = V7X_SUBLANES * (s // V7X_SUBLANES + 4)
    scan_buf = pltpu.VMEM((2, w // V7X_LANES, scan_rows, V7X_LANES), F32)
    return pl.pallas_call(
        functools.partial(_lru_kernel, ctx_len=ctx_len, seq=s),
        grid=(b,),
        in_specs=[pl.BlockSpec((1, s, w2), lambda i: (i, 0, 0))]
        + [_layer_spec(a, layer) for a in (conv_w, conv_b, wg, bg, lam)],
        out_specs=pl.BlockSpec((1, s, w), lambda i: (i, 0, 0)),
        out_shape=jax.ShapeDtypeStruct((b, s, w), BF16),
        scratch_shapes=[pltpu.VMEM((w // V7X_LANES, s + 3 * V7X_SUBLANES, V7X_LANES), F32),
                        scan_buf, scan_buf, scan_buf],
        compiler_params=_cparams("parallel"),
        name="rglru",
    )(p_lru, conv_w, conv_b, wg, bg, lam)


def _rope(x, cos, sin_signed, lane_even):
    partner = jnp.where(lane_even, pltpu.roll(x, V7X_LANES - 1, 1), pltpu.roll(x, 1, 1))
    return x * cos + partner * sin_signed


def _ret_kernel(lg_ref, p_ref, cos_ref, sin_ref, lgl_ref, o_ref,
                q_ref, k_ref, dlt_ref, st_ref, *, layer, ctx_len, seq, head_dim):
    s = seq
    c = RET_CHUNK
    lanes = V7X_LANES
    w = o_ref.shape[2]
    n_pairs = w // lanes
    n_chunks = s // c
    n_ctx = ctx_len // c
    scale = head_dim ** -0.5

    rowi = lax.broadcasted_iota(jnp.int32, (c, lanes), 0).astype(F32)
    lane = lax.broadcasted_iota(jnp.int32, (c, lanes), 1)
    lane_even = (lane % 2) == 0
    low_half = lane < head_dim
    dif = (lax.broadcasted_iota(jnp.int32, (c, c), 0)
           - lax.broadcasted_iota(jnp.int32, (c, c), 1)).astype(F32)
    blk_row_low = lax.broadcasted_iota(jnp.int32, (lanes, lanes), 0) < head_dim
    blk_col_low = lax.broadcasted_iota(jnp.int32, (lanes, lanes), 1) < head_dim
    blk_diag = blk_row_low == blk_col_low

    order_b = list(range(n_ctx - 1, -1, -1)) + list(range(n_chunks - 1, n_ctx - 1, -1))

    for p in range(n_pairs):
        lsl = slice(p * lanes, (p + 1) * lanes)
        lg_f = lgl_ref[0:1, lsl]
        lg_b = lgl_ref[1:2, lsl]
        kd_f = jnp.exp(lg_f * (c - 1.0 - rowi))
        kd_b = jnp.exp(lg_b * rowi)
        qd_f = jnp.exp(lg_f * (rowi + 1.0))
        qd_b = jnp.exp(lg_b * (c - rowi))

        for ch in range(n_chunks):
            r0 = ch * c
            q = p_ref[0, r0:r0 + c, p * lanes:(p + 1) * lanes].astype(F32)
            k = p_ref[0, r0:r0 + c, w + p * lanes:w + (p + 1) * lanes].astype(F32)
            v = p_ref[0, r0:r0 + c, 2 * w + p * lanes:2 * w + (p + 1) * lanes]
            if ch >= n_ctx:
                t0 = r0 - ctx_len
                cos = cos_ref[t0:t0 + c, :]
                sin = sin_ref[t0:t0 + c, :]
                q = _rope(q, cos, sin, lane_even)
                k = _rope(k, cos, sin, lane_even)
            q_ref[r0:r0 + c, lsl] = (q * scale).astype(BF16)
            k_ref[r0:r0 + c, lsl] = k.astype(BF16)
            d_f = pl.dot((k * kd_f).astype(BF16), v, trans_a=True)
            d_b = pl.dot((k * kd_b).astype(BF16), v, trans_a=True)
            dlt_ref[0, ch] = jnp.where(blk_diag, d_f, 0.0)
            dlt_ref[1, ch] = jnp.where(blk_diag, d_b, 0.0)

        gc_f = jnp.exp(jnp.where(blk_row_low, lg_ref[layer, 0, 2 * p], lg_ref[layer, 0, 2 * p + 1]) * float(c))
        gc_b = jnp.exp(jnp.where(blk_row_low, lg_ref[layer, 1, 2 * p], lg_ref[layer, 1, 2 * p + 1]) * float(c))
        state = jnp.zeros((lanes, lanes), F32)
        for ch in range(n_chunks):
            st_ref[0, ch] = state.astype(BF16)
            state = gc_f * state + dlt_ref[0, ch]
        state = jnp.zeros((lanes, lanes), F32)
        for ch in order_b:
            st_ref[1, ch] = state.astype(BF16)
            state = gc_b * state + dlt_ref[1, ch]

        dmats = []
        for h in range(2):
            lgf_s = lg_ref[layer, 0, 2 * p + h]
            lgb_s = lg_ref[layer, 1, 2 * p + h]
            dmats.append(jnp.where(dif > 0, jnp.exp(lgf_s * jnp.maximum(dif, 0.0)),
                                   jnp.where(dif < 0, jnp.exp(lgb_s * jnp.maximum(-dif, 0.0)), 2.0)))

        def out_chunk(ch, _):
            r0 = pl.multiple_of(ch * c, c)
            rows = pl.ds(r0, c)
            q = q_ref[rows, lsl]
            k = k_ref[rows, lsl]
            v = p_ref[0, rows, 2 * w + p * lanes:2 * w + (p + 1) * lanes]
            g = p_ref[0, rows, 3 * w + p * lanes:3 * w + (p + 1) * lanes].astype(F32)
            zero = jnp.zeros_like(q)
            s0 = pl.dot(jnp.where(low_half, q, zero), k, trans_b=True)
            s1 = pl.dot(jnp.where(low_half, zero, q), k, trans_b=True)
            o0 = jnp.dot((s0 * dmats[0]).astype(BF16), v, preferred_element_type=F32)
            o1 = jnp.dot((s1 * dmats[1]).astype(BF16), v, preferred_element_type=F32)
            o = jnp.where(low_half, o0, o1)
            qf = q.astype(F32)
            o = o + jnp.dot((qf * qd_f).astype(BF16), st_ref[0, ch], preferred_element_type=F32)
            o = o + jnp.dot((qf * qd_b).astype(BF16), st_ref[1, ch], preferred_element_type=F32)
            oo = o * o
            ss0 = jnp.sum(jnp.where(low_half, oo, 0.0), axis=-1, keepdims=True)
            ss1 = jnp.sum(jnp.where(low_half, 0.0, oo), axis=-1, keepdims=True)
            inv = jnp.where(low_half, lax.rsqrt(ss0 / head_dim + EPS), lax.rsqrt(ss1 / head_dim + EPS))
            o_ref[0, rows, lsl] = (_silu(g) * (o * inv)).astype(o_ref.dtype)
            return 0

        lax.fori_loop(0, n_chunks, out_chunk, 0, unroll=True)


def _retention(p_ret, lg, lg_lane, cos, sin_signed, layer, ctx_len, head_dim):
    b, s, w4 = p_ret.shape
    w = w4 // 4
    n_chunks = s // RET_CHUNK
    return pl.pallas_call(
        functools.partial(_ret_kernel, layer=layer, ctx_len=ctx_len, seq=s, head_dim=head_dim),
        grid=(b,),
        in_specs=[
            pl.BlockSpec(memory_space=pltpu.SMEM),
            pl.BlockSpec((1, s, w4), lambda i: (i, 0, 0)),
            _const_spec(cos.shape), _const_spec(sin_signed.shape), _layer_spec(lg_lane, layer),
        ],
        out_specs=pl.BlockSpec((1, s, w), lambda i: (i, 0, 0)),
        out_shape=jax.ShapeDtypeStruct((b, s, w), BF16),
        scratch_shapes=[
            pltpu.VMEM((s, w), BF16),
            pltpu.VMEM((s, w), BF16),
            pltpu.VMEM((2, n_chunks, V7X_LANES, V7X_LANES), F32),
            pltpu.VMEM((2, n_chunks, V7X_LANES, V7X_LANES), BF16),
        ],
        compiler_params=_cparams("parallel"),
        name="retention",
    )(lg, p_ret, cos, sin_signed, lg_lane)


def _cm_kernel(z_ref, lng_ref, lnb_ref, ws_ref, bs_ref, o_ref, *, seq, group_dim):
    c = CM_CHUNK
    lanes = V7X_LANES
    w = o_ref.shape[2]
    n_slabs = w // lanes
    low_half = lax.broadcasted_iota(jnp.int32, (c, lanes), 1) < group_dim
    lane_w = lax.broadcasted_iota(jnp.int32, (c, w), 1)

    def body(ch, _):
        r0 = pl.multiple_of(ch * c, c)
        rows = pl.ds(r0, c)
        z = _gelu_tanh(z_ref[0, rows, :].astype(F32))
        u = z[:, :w]
        vn = []
        for sl in range(n_slabs):
            v = z[:, w + sl * lanes:w + (sl + 1) * lanes]
            m0 = jnp.sum(jnp.where(low_half, v, 0.0), axis=-1, keepdims=True) / group_dim
            m1 = jnp.sum(jnp.where(low_half, 0.0, v), axis=-1, keepdims=True) / group_dim
            xc = v - jnp.where(low_half, m0, m1)
            xx = xc * xc
            v0 = jnp.sum(jnp.where(low_half, xx, 0.0), axis=-1, keepdims=True) / group_dim
            v1 = jnp.sum(jnp.where(low_half, 0.0, xx), axis=-1, keepdims=True) / group_dim
            vn.append(xc * jnp.where(low_half, lax.rsqrt(v0 + EPS), lax.rsqrt(v1 + EPS)))
        vn = jnp.concatenate(vn, axis=-1) * lng_ref[...] + lnb_ref[...]
        s_all = jnp.dot(ws_ref[...], vn.astype(BF16), preferred_element_type=F32)
        sp = s_all[0:c]
        for gi in range(1, CM_GROUPS):
            sp = jnp.where(lane_w >= gi * group_dim, s_all[gi * c:(gi + 1) * c], sp)
        o_ref[0, rows, :] = (u * (sp + bs_ref[...])).astype(o_ref.dtype)
        return 0

    lax.fori_loop(0, seq // c, body, 0, unroll=3 if (seq // c) % 3 == 0 else 1)


def _chunk_mlp(p_cm, ln_g, ln_b, ws, bs, layer, group_dim):
    b, s, w2 = p_cm.shape
    w = w2 // 2
    return pl.pallas_call(
        functools.partial(_cm_kernel, seq=s, group_dim=group_dim),
        grid=(b,),
        in_specs=[pl.BlockSpec((1, s, w2), lambda i: (i, 0, 0))]
        + [_layer_spec(a, layer) for a in (ln_g, ln_b, ws, bs)],
        out_specs=pl.BlockSpec((1, s, w), lambda i: (i, 0, 0)),
        out_shape=jax.ShapeDtypeStruct((b, s, w), BF16),
        compiler_params=_cparams("parallel"),
        name="chunk_gmlp",
    )(p_cm, ln_g, ln_b, ws, bs)


def _swiglu_rows(h, w1_ref, w3_ref, w2_ref):
    f = w1_ref.shape[1]
    acc = None
    for c0 in range(0, f, FF_CHUNK):
        a = jnp.dot(h, w1_ref[:, c0:c0 + FF_CHUNK], preferred_element_type=F32)
        g = jnp.dot(h, w3_ref[:, c0:c0 + FF_CHUNK], preferred_element_type=F32)
        act = (_silu(a) * g).astype(BF16)
        part = jnp.dot(act, w2_ref[c0:c0 + FF_CHUNK, :], preferred_element_type=F32)
        acc = part if acc is None else acc + part
    return acc


def _ffn_kernel(*refs, n_src, ctx_rows):
    src = refs[:n_src]
    (gmix_ref, gpre_ref, gpost_ref, ml_ref, mc_ref, pa_ref, pb_ref, pc_ref, wo_ref,
     w1_ref, w3_ref, w2_ref, o_ref, xm_ref, h_ref) = refs[n_src:n_src + 15]
    x_ref = _residual_rows(src, refs[-1], ctx_rows)
    y = _mix_out((pa_ref, pb_ref, pc_ref), wo_ref, h_ref)
    _gated_residual(x_ref[...], y, gmix_ref, ml_ref, mc_ref, xm_ref, ctx_rows, 2)
    _prenorm_modulate(xm_ref[...], gpre_ref, ml_ref, mc_ref, h_ref, ctx_rows, 3)
    y = _swiglu_rows(h_ref[...], w1_ref, w3_ref, w2_ref)
    _gated_residual(xm_ref[...], y, gpost_ref, ml_ref, mc_ref, o_ref, ctx_rows, 5)


def _mix_dense_ffn(src, rows, gains, mod_all, parts, w_out_b, w1, w3, w2, layer, j):
    d = src[0].shape[1]
    ml_spec, mc_spec = rows.mod_specs(mod_all, layer)
    src_specs, src_scratch = _source_specs(rows, src, d)
    return pl.pallas_call(
        functools.partial(_ffn_kernel, n_src=len(src), ctx_rows=rows.ctx_rows),
        grid=rows.grid,
        in_specs=src_specs + [_layer_spec(g, layer) for g in gains[1:]] + [ml_spec, mc_spec]
        + [rows.slab(a.shape[1]) for a in parts] + [_layer_spec(w_out_b, layer)]
        + [_layer_spec(w, j) for w in (w1, w3, w2)],
        out_specs=rows.slab(d),
        out_shape=jax.ShapeDtypeStruct((rows.batch * rows.seq, d), F32),
        scratch_shapes=[pltpu.VMEM((rows.rb, d), F32), pltpu.VMEM((rows.rb, d), BF16)] + src_scratch,
        input_output_aliases={0: 0} if len(src) == 1 else {},
        compiler_params=_cparams("parallel", "parallel"),
        name="mix_dense_swiglu",
    )(*src, *gains[1:], mod_all, mod_all, *parts, w_out_b, w1, w3, w2)


def _router_kernel(x_ref, gmix_ref, g_ref, ml_ref, mc_ref, pa_ref, pb_ref, pc_ref, wo_ref, wr_ref,
                   xo_ref, h_ref, idx_ref, wgt_ref, hf_ref, *, ctx_rows, n_experts):
    y = _mix_out((pa_ref, pb_ref, pc_ref), wo_ref, h_ref)
    _gated_residual(x_ref[...], y, gmix_ref, ml_ref, mc_ref, xo_ref, ctx_rows, 2)
    _prenorm_modulate(xo_ref[...], g_ref, ml_ref, mc_ref, hf_ref, ctx_rows, 3)
    h = hf_ref[...]
    h_ref[...] = h.astype(h_ref.dtype)
    logits = jnp.dot(h, wr_ref[...], preferred_element_type=F32)
    lane = lax.broadcasted_iota(jnp.int32, logits.shape, 1).astype(F32)
    neg_inf = jnp.float32(-jnp.inf)
    logits = jnp.where(lane < n_experts, logits, neg_inf)
    big = jnp.float32(V7X_LANES)
    m1 = jnp.max(logits, axis=-1, keepdims=True)
    i1 = jnp.min(jnp.where(logits == m1, lane, big), axis=-1, keepdims=True)
    rest = jnp.where(lane == i1, neg_inf, logits)
    m2 = jnp.max(rest, axis=-1, keepdims=True)
    i2 = jnp.min(jnp.where(rest == m2, lane, big), axis=-1, keepdims=True)
    e2 = jnp.exp(m2 - m1)
    w1 = 1.0 / (1.0 + e2)
    w2 = e2 / (1.0 + e2)
    out_lane = lax.broadcasted_iota(jnp.int32, idx_ref.shape, 1)
    idx_ref[...] = jnp.where(out_lane == 0, i1, i2).astype(jnp.int32)
    wgt_ref[...] = jnp.where(out_lane == 0, w1, w2)


def _mix_router(xs2, rows, gains, mod_all, parts, w_out_b, wr_pad, layer, j, n_experts):
    d = xs2.shape[1]
    n_walk = rows.batch * rows.rows
    ml_spec, mc_spec = rows.mod_specs(mod_all, layer)
    if rows.latent_only:
        x_spec, x_shape, aliases = rows.walked(d), (n_walk, d), {}
    else:
        x_spec, x_shape, aliases = rows.slab(d), xs2.shape, {0: 0}
    return pl.pallas_call(
        functools.partial(_router_kernel, ctx_rows=rows.ctx_rows, n_experts=n_experts),
        grid=rows.grid,
        in_specs=[rows.slab(d), _layer_spec(gains[1], layer), _layer_spec(gains[2], layer), ml_spec, mc_spec]
        + [rows.slab(a.shape[1]) for a in parts]
        + [_layer_spec(w_out_b, layer), _layer_spec(wr_pad, j)],
        out_specs=[x_spec, rows.walked(d), rows.walked(V7X_SUBLANES), rows.walked(V7X_SUBLANES)],
        out_shape=[jax.ShapeDtypeStruct(x_shape, F32),
                   jax.ShapeDtypeStruct((n_walk, d), BF16),
                   jax.ShapeDtypeStruct((n_walk, V7X_SUBLANES), jnp.int32),
                   jax.ShapeDtypeStruct((n_walk, V7X_SUBLANES), F32)],
        scratch_shapes=[pltpu.VMEM((rows.rb, d), F32)],
        input_output_aliases=aliases,
        compiler_params=_cparams("parallel", "parallel"),
        name="mix_moe_router",
    )(xs2, gains[1], gains[2], mod_all, mod_all, *parts, w_out_b, wr_pad)


def _grouped_kernel(te_ref, tv_ref, h_ref, w1_ref, w3_ref, w2_ref, o_ref):
    i = pl.program_id(0)

    @pl.when(tv_ref[i] != 0)
    def _():
        o_ref[...] = _swiglu_rows(h_ref[...], w1_ref, w3_ref, w2_ref).astype(o_ref.dtype)

    @pl.when(tv_ref[i] == 0)
    def _():
        o_ref[...] = jnp.zeros_like(o_ref)


def _grouped_swiglu(layer_idx, tile_expert, tile_valid, hs, w1, w3, w2):
    p, d = hs.shape
    f = w1.shape[3]
    tm = MOE_TILE
    grid_spec = pltpu.PrefetchScalarGridSpec(
        num_scalar_prefetch=2,
        grid=(p // tm,),
        in_specs=[
            pl.BlockSpec((tm, d), lambda i, te, tv: (i, 0)),
            pl.BlockSpec((None, None, d, f), lambda i, te, tv: (layer_idx, te[i], 0, 0)),
            pl.BlockSpec((None, None, d, f), lambda i, te, tv: (layer_idx, te[i], 0, 0)),
            pl.BlockSpec((None, None, f, d), lambda i, te, tv: (layer_idx, te[i], 0, 0)),
        ],
        out_specs=pl.BlockSpec((tm, d), lambda i, te, tv: (i, 0)),
    )
    return pl.pallas_call(
        _grouped_kernel,
        grid_spec=grid_spec,
        out_shape=jax.ShapeDtypeStruct((p, d), BF16),
        compiler_params=_cparams("arbitrary"),
        name="moe_grouped_swiglu",
    )(tile_expert, tile_valid, hs, w1, w3, w2)


def _combine_kernel(*refs, ctx_rows, n_in, with_proj):
    x_ref, g_ref, ml_ref, mc_ref, y0_ref, y1_ref, wgt_ref = refs[:7]
    o_ref = refs[n_in]
    wgt = wgt_ref[...]
    y = wgt[:, 0:1] * y0_ref[...].astype(F32) + wgt[:, 1:2] * y1_ref[...].astype(F32)
    _gated_residual(x_ref[...], y, g_ref, ml_ref, mc_ref, o_ref, ctx_rows, 5)
    if with_proj:
        gn_ref, mln_ref, mcn_ref, w_ref = refs[7:11]
        h_ref = refs[-1]
        _prenorm_modulate(o_ref[...], gn_ref, mln_ref, mcn_ref, h_ref, ctx_rows, 0)
        h = h_ref[...]
        c0 = 0
        for p_ref in refs[n_in + 1:n_in + 4]:
            n = p_ref.shape[1]
            p_ref[...] = jnp.dot(h, w_ref[:, c0:c0 + n], preferred_element_type=F32).astype(p_ref.dtype)
            c0 += n


def _combine(xres, g_all, mod_all, y0, y1, wgt, layer, rows, dest, proj=None, proj_dest=None):
    d = xres.shape[1]
    ml_spec, mc_spec = rows.mod_specs(mod_all, layer)
    x_spec = rows.slab(d) if dest is None else rows.walked(d)
    in_specs = [x_spec, _layer_spec(g_all, layer), ml_spec, mc_spec,
                rows.local(d), rows.local(d), rows.walked(V7X_SUBLANES)]
    args = [xres, g_all, mod_all, mod_all, y0, y1, wgt]
    if proj is not None:
        gn_all, w_in_b, widths = proj
        mln_spec, mcn_spec = rows.mod_specs(mod_all, layer + 1)
        in_specs += [_layer_spec(gn_all, layer + 1), mln_spec, mcn_spec, _layer_spec(w_in_b, layer + 1)]
        args += [gn_all, mod_all, mod_all, w_in_b]
    if dest is None:
        out_specs, out_shapes, aliases = [rows.slab(d)], [jax.ShapeDtypeStruct(xres.shape, F32)], {0: 0}
    else:
        out_specs, out_shapes, aliases = [rows.walked(d)], [jax.ShapeDtypeStruct(dest.shape, F32)], {}
        if not isinstance(dest, jax.ShapeDtypeStruct):
            in_specs.append(pl.BlockSpec(memory_space=pl.ANY))
            args.append(dest)
            aliases = {len(args) - 1: 0}
    scratch = []
    if proj is not None:
        n_total = rows.batch * rows.seq
        out_specs += [rows.slab(n) for n in widths]
        out_shapes += [jax.ShapeDtypeStruct((n_total, n), BF16) for n in widths]
        scratch = [pltpu.VMEM((rows.rb, d), BF16)]
        for k, arr in enumerate(proj_dest or ()):
            in_specs.append(pl.BlockSpec(memory_space=pl.ANY))
            args.append(arr)
            aliases[len(args) - 1] = 1 + k
    res = pl.pallas_call(
        functools.partial(_combine_kernel, ctx_rows=rows.ctx_rows, n_in=len(args), with_proj=proj is not None),
        grid=rows.grid,
        in_specs=in_specs,
        out_specs=out_specs,
        out_shape=out_shapes,
        scratch_shapes=scratch,
        input_output_aliases=aliases,
        compiler_params=_cparams("parallel", "parallel"),
        name="moe_combine",
    )(*args)
    return res if proj is not None else res[0]


def _gather_rows(src, idx):
    return src.at[idx].get(mode="promise_in_bounds")


def _routing_tables(e_flat, tok_flat, n_exp):
    n_assign = e_flat.shape[0]
    tm = MOE_TILE
    onehot = (e_flat[:, None] == jnp.arange(n_exp, dtype=jnp.int32)[None, :]).astype(jnp.int32)
    csum = jnp.cumsum(onehot, axis=0)
    counts = csum[-1]
    rank = jnp.sum((csum - onehot) * onehot, axis=1)
    padded = ((counts + tm - 1) // tm) * tm
    ends = jnp.cumsum(padded)
    starts = ends - padded
    pos = jnp.sum(onehot * starts[None, :], axis=1) + rank
    n_rows = n_assign + n_exp * tm
    n_tiles = n_rows // tm
    src_tok = jnp.zeros((n_rows,), jnp.int32).at[pos].set(
        tok_flat, unique_indices=True, mode="promise_in_bounds")
    tile_start = jnp.arange(n_tiles, dtype=jnp.int32) * tm
    tile_expert = jnp.minimum(jnp.sum((tile_start[:, None] >= ends[None, :]).astype(jnp.int32), axis=1),
                              n_exp - 1)
    tile_valid = (tile_start < ends[-1]).astype(jnp.int32)
    return pos, src_tok, tile_expert, tile_valid


def _mix_moe_ffn(xs2, shape, gains, mod_all, parts, w_out_b, wr_pad, w1, w3, w2, layer, j, n_exp, last,
                 proj=None):
    b, s, ctx_len = shape
    d = xs2.shape[1]
    rows = _Rows(b, s, ctx_len, latent_only=last)
    xres, h, idx, wgt = _mix_router(xs2, rows, gains, mod_all, parts, w_out_b, wr_pad, layer, j, n_exp)
    r = rows.rows
    bh = b // MOE_SPLITS
    out = jax.ShapeDtypeStruct((b * r, d), F32) if last else None
    nxt = None
    for g in range(MOE_SPLITS):
        e_flat = idx[g * bh * r:(g + 1) * bh * r, :TOP_K].reshape(bh * r * TOP_K)
        tok_flat = g * bh * r + jnp.arange(bh * r * TOP_K, dtype=jnp.int32) // TOP_K
        pos, src_tok, tile_expert, tile_valid = _routing_tables(e_flat, tok_flat, n_exp)
        hs = _gather_rows(h, src_tok)
        ys = _grouped_swiglu(j, tile_expert, tile_valid, hs, w1, w3, w2)
        pos = pos.reshape(bh * r, TOP_K)
        y0 = _gather_rows(ys, pos[:, 0])
        y1 = _gather_rows(ys, pos[:, 1])
        rows_g = _Rows(b, s, ctx_len, latent_only=last, batch0=g * bh, n_batch=bh)
        res = _combine(xres, gains[3], mod_all, y0, y1, wgt, layer, rows_g, out, proj, nxt)
        if proj is not None:
            res, nxt = res[0], tuple(res[1:])
        if last:
            out = res
        else:
            xres = res
    return (out if last else xres), nxt


def _block_diag(wh):
    h, d = wh.shape[-3], wh.shape[-1]
    eye = jnp.eye(h, dtype=wh.dtype)
    out = jnp.einsum("...hij,hg->...higj", wh, eye)
    return out.reshape(wh.shape[:-3] + (h * d, h * d))


def kernel(x, c, ctx, c_ctx, w_mod, b_mod, g_mix_pre, g_mix_post, g_ffn_pre, g_ffn_post, w_in, w_out,
           lru_conv_w, lru_conv_b, lru_wa, lru_ba, lru_wx, lru_bx, lru_lam, ret_theta,
           cm_ln_g, cm_ln_b, cm_ws, cm_bs, ffn_w1, ffn_w3, ffn_w2, router_w, moe_w1, moe_w3, moe_w2):
    b, l, d = x.shape
    lc = ctx.shape[1]
    s = lc + l
    depth = w_mod.shape[0]
    lru_w = lru_conv_w.shape[2]
    cm_w = cm_ln_g.shape[1] * cm_ln_g.shape[2]
    ret_w = (w_in.shape[2] - 2 * lru_w - 2 * cm_w) // 4
    head_dim = ret_w // RET_HEADS
    group_dim = cm_ln_g.shape[2]
    n_exp = router_w.shape[2]
    assert s % ROW_BLOCK == 0 and lc < ROW_BLOCK and lc % RET_CHUNK == 0 and l % RET_CHUNK == 0
    assert ret_w % V7X_LANES == 0 and cm_w % V7X_LANES == 0 and 2 * head_dim == V7X_LANES
    assert b % MOE_SPLITS == 0 and l % lc == 0 and l >= ROW_BLOCK

    n_rows = l // GRID_W
    rows = jnp.repeat(jnp.arange(n_rows, dtype=F32), GRID_W)
    cols = jnp.tile(jnp.arange(GRID_W, dtype=F32), n_rows)
    pairs = head_dim // 4
    freqs = ROPE_BASE ** (-jnp.arange(pairs, dtype=F32) / pairs)
    ang = jnp.concatenate([rows[:, None] * freqs, cols[:, None] * freqs], axis=-1)
    cos = jnp.tile(jnp.repeat(jnp.cos(ang), 2, axis=1), (1, V7X_LANES // head_dim))
    sin = jnp.repeat(jnp.sin(ang), 2, axis=1) * jnp.tile(jnp.array([-1.0, 1.0], F32), head_dim // 2)
    sin_signed = jnp.tile(sin, (1, V7X_LANES // head_dim))

    w_in_b, w_out_b = w_in.astype(BF16), w_out.astype(BF16)
    ffn_b1, ffn_b3, ffn_b2 = ffn_w1.astype(BF16), ffn_w3.astype(BF16), ffn_w2.astype(BF16)
    moe_b1, moe_b3, moe_b2 = moe_w1.astype(BF16), moe_w3.astype(BF16), moe_w2.astype(BF16)
    wg = (0.5 * jnp.concatenate([_block_diag(lru_wa[:, 0]), _block_diag(lru_wx[:, 0]),
                                 _block_diag(lru_wa[:, 1]), _block_diag(lru_wx[:, 1])], axis=2)).astype(BF16)
    bg = 0.5 * jnp.concatenate([lru_ba[:, 0].reshape(depth, 1, lru_w), lru_bx[:, 0].reshape(depth, 1, lru_w),
                                lru_ba[:, 1].reshape(depth, 1, lru_w), lru_bx[:, 1].reshape(depth, 1, lru_w)],
                               axis=2)
    conv_b = lru_conv_b.reshape(depth, 1, lru_w)
    lam = lru_lam.reshape(depth, 2, lru_w)
    lg = jax.nn.log_sigmoid(ret_theta.astype(F32))
    lg_lane = jnp.repeat(lg, head_dim, axis=2)
    ln_g = cm_ln_g.reshape(depth, 1, cm_w)
    ln_b = cm_ln_b.reshape(depth, 1, cm_w)
    ws = cm_ws.reshape(depth, CM_GROUPS * CM_CHUNK, CM_CHUNK).astype(BF16)
    bs = jnp.repeat(jnp.swapaxes(cm_bs, 1, 2), group_dim, axis=2)
    wr_pad = jnp.zeros((router_w.shape[0], d, V7X_LANES), F32).at[:, :, :n_exp].set(router_w)
    gains = [g.reshape(depth, 1, d) for g in (g_mix_pre, g_mix_post, g_ffn_pre, g_ffn_post)]
    widths = (2 * lru_w, 4 * ret_w, 2 * cm_w)

    n_cond = ((b + 1 + V7X_SUBLANES - 1) // V7X_SUBLANES) * V7X_SUBLANES
    cpad = jnp.zeros((n_cond, d), F32).at[:b].set(c).at[b].set(c_ctx)
    mod_all = _modulation(cpad, w_mod, b_mod).reshape(depth, n_cond, N_MOD, d)

    rows = _Rows(b, s, lc)
    src = (ctx.reshape(b * lc, d), x.reshape(b * l, d))
    nxt = None
    for layer in range(depth):
        p_lru, p_ret, p_cm = nxt or _inproj(src, rows, gains[0], mod_all, w_in_b, layer, widths)
        nxt = None
        o_lru = _lru(p_lru.reshape(b, s, -1), lru_conv_w, conv_b, wg, bg, lam, layer, lc)
        o_ret = _retention(p_ret.reshape(b, s, -1), lg, lg_lane, cos, sin_signed, layer, lc, head_dim)
        o_cm = _chunk_mlp(p_cm.reshape(b, s, -1), ln_g, ln_b, ws, bs, layer, group_dim)
        parts = tuple(o.reshape(b * s, -1) for o in (o_lru, o_ret, o_cm))

        j = layer // 2
        last = layer == depth - 1
        if layer % 2 == 0:
            xs2 = _mix_dense_ffn(src, rows, gains, mod_all, parts, w_out_b, ffn_b1, ffn_b3, ffn_b2, layer, j)
        else:
            xs2, nxt = _mix_moe_ffn(src[0], (b, s, lc), gains, mod_all, parts, w_out_b, wr_pad,
                                    moe_b1, moe_b3, moe_b2, layer, j, n_exp, last,
                                    proj=None if last else (gains[0], w_in_b, widths))
            if last:
                return xs2.reshape(b, l, d)
        src = (xs2,)
    return xs2.reshape(b, s, d)[:, lc:, :]
```

```python
import functools
import math

import jax
import jax.numpy as jnp
from jax import lax
from jax.experimental import pallas as pl
from jax.experimental.pallas import tpu as pltpu

F32 = jnp.float32
BF16 = jnp.bfloat16

EPS = 1e-6
N_MOD = 6
LRU_CONV = 4
LRU_C = 8.0
RET_HEADS = 6
ROPE_BASE = 100.0
GRID_W = 64
CM_GROUPS = 4
CM_CHUNK = 128
TOP_K = 2

V7X_LANES = 128
V7X_SUBLANES = 8
V7X_VMEM_LIMIT = 56 * 1024 * 1024

ROW_BLOCK = 768
RET_CHUNK = 256
FF_CHUNK = 256
MOE_TILE = 512
LAT_BLOCK = 1024
MOE_SPLITS = 2
LOG2_E = 1.4426950408889634
TINY = 1e-30


def _cparams(*sem):
    return pltpu.CompilerParams(dimension_semantics=sem, vmem_limit_bytes=V7X_VMEM_LIMIT)


def _layer_spec(arr, layer):
    nd = arr.ndim - 1
    return pl.BlockSpec((None,) + arr.shape[1:], lambda *_: (layer,) + (0,) * nd,
                        pipeline_mode=pl.Buffered(1))


def _const_spec(shape):
    nd = len(shape)
    return pl.BlockSpec(shape, lambda *_: (0,) * nd, pipeline_mode=pl.Buffered(1))


def _rms(x):
    return x * lax.rsqrt(jnp.mean(x * x, axis=-1, keepdims=True) + EPS)


def _gelu_tanh(x):
    return 0.5 * x * (1.0 + jnp.tanh(0.7978845608028654 * (x + 0.044715 * (x * x * x))))


def _sigmoid(x):
    return 1.0 / (1.0 + jnp.exp(-x))


def _silu(x):
    return x * _sigmoid(x)


def _mod_kernel(c_ref, w_ref, b_ref, o_ref):
    s = _silu(c_ref[...])
    o_ref[...] = jnp.dot(s, w_ref[...], preferred_element_type=F32) + b_ref[...]


def _modulation(cpad, w_mod, b_mod):
    depth, d, nd = w_mod.shape
    r = cpad.shape[0]
    return pl.pallas_call(
        _mod_kernel,
        grid=(depth, nd // d),
        in_specs=[
            pl.BlockSpec((r, d), lambda l, n: (0, 0)),
            pl.BlockSpec((None, d, d), lambda l, n: (l, 0, n)),
            pl.BlockSpec((None, 1, d), lambda l, n: (l, 0, n)),
        ],
        out_specs=pl.BlockSpec((None, r, d), lambda l, n: (l, 0, n)),
        out_shape=jax.ShapeDtypeStruct((depth, r, nd), F32),
        compiler_params=_cparams("parallel", "parallel"),
        name="modulation",
    )(cpad, w_mod, b_mod.reshape(depth, 1, nd))


class _Rows:
    def __init__(self, batch, seq, ctx_len, latent_only=False, batch0=0, n_batch=None):
        self.batch, self.seq, self.ctx_len = batch, seq, ctx_len
        self.lat_len = seq - ctx_len
        self.batch0 = batch0
        self.n_batch = batch if n_batch is None else n_batch
        self.latent_only = latent_only
        if latent_only:
            self.rb = LAT_BLOCK if self.lat_len % LAT_BLOCK == 0 else ctx_len
            self.n_blk = self.lat_len // self.rb
            self.ctx_rows = 0
        else:
            self.rb = ROW_BLOCK
            self.n_blk = seq // ROW_BLOCK
            self.ctx_rows = ctx_len
        self.rows = self.rb * self.n_blk
        self.align = math.gcd(self.rb, ctx_len)

    @property
    def grid(self):
        return (self.n_batch, self.n_blk)

    def slab(self, width):
        if not self.latent_only:
            return pl.BlockSpec((self.rb, width), lambda b, j: ((b + self.batch0) * self.n_blk + j, 0))

        def index(b, j):
            row = (b + self.batch0) * self.seq + self.ctx_len + j * self.rb
            return (pl.multiple_of(row, self.align), 0)
        return pl.BlockSpec((pl.Element(self.rb), pl.Element(width)), index)

    def walked(self, width):
        return pl.BlockSpec((self.rb, width), lambda b, j: ((b + self.batch0) * self.n_blk + j, 0))

    def local(self, width):
        return pl.BlockSpec((self.rb, width), lambda b, j: (b * self.n_blk + j, 0))

    def split_source(self, width):
        assert not self.latent_only and self.lat_len >= self.rb
        ctx = pl.BlockSpec((self.ctx_len, width), lambda b, j: (b + self.batch0, 0))

        def index(b, j):
            row = (b + self.batch0) * self.lat_len + jnp.maximum(j * self.rb - self.ctx_len, 0)
            return (pl.multiple_of(row, self.align), 0)
        return ctx, pl.BlockSpec((pl.Element(self.rb), pl.Element(width)), index)

    def mod_specs(self, mod_all, layer):
        d = mod_all.shape[-1]
        lat = pl.BlockSpec((None, 1, N_MOD, d), lambda b, j: (layer, b + self.batch0, 0, 0))
        ctx = pl.BlockSpec((None, 1, N_MOD, d), lambda b, j: (layer, self.batch, 0, 0))
        return lat, ctx


def _row_mod(ml_ref, mc_ref, idx, first_block):
    lat = ml_ref[0, idx:idx + 1, :]
    top = jnp.where(first_block, mc_ref[0, idx:idx + 1, :], lat)
    return top, lat


def _prenorm_modulate(x, g_ref, ml_ref, mc_ref, h_ref, ctx_rows, shift_idx):
    first = pl.program_id(1) == 0
    sh_top, sh_lat = _row_mod(ml_ref, mc_ref, shift_idx, first)
    sc_top, sc_lat = _row_mod(ml_ref, mc_ref, shift_idx + 1, first)
    y = _rms(x) * g_ref[...]
    if ctx_rows:
        h_ref[:ctx_rows, :] = (y[:ctx_rows] * (1.0 + sc_top) + sh_top).astype(h_ref.dtype)
    h_ref[ctx_rows:, :] = (y[ctx_rows:] * (1.0 + sc_lat) + sh_lat).astype(h_ref.dtype)


def _gated_residual(x, y, g_ref, ml_ref, mc_ref, o_ref, ctx_rows, gate_idx):
    first = pl.program_id(1) == 0
    gt_top, gt_lat = _row_mod(ml_ref, mc_ref, gate_idx, first)
    r = _rms(y) * g_ref[...]
    if ctx_rows:
        o_ref[:ctx_rows, :] = x[:ctx_rows] + gt_top * r[:ctx_rows]
    o_ref[ctx_rows:, :] = x[ctx_rows:] + gt_lat * r[ctx_rows:]


def _residual_rows(src_refs, xbuf_ref, ctx_rows):
    if len(src_refs) == 1:
        return src_refs[0]
    ctx_ref, lat_ref = src_refs
    first = pl.program_id(1) == 0

    @pl.when(first)
    def _():
        xbuf_ref[:ctx_rows, :] = ctx_ref[...]
        xbuf_ref[ctx_rows:, :] = lat_ref[:xbuf_ref.shape[0] - ctx_rows, :]

    @pl.when(jnp.logical_not(first))
    def _():
        xbuf_ref[...] = lat_ref[...]

    return xbuf_ref


def _source_specs(rows, src, d):
    if len(src) == 1:
        return [rows.slab(d)], []
    return list(rows.split_source(d)), [pltpu.VMEM((rows.rb, d), F32)]


def _mix_out(part_refs, w_ref, cat_ref):
    c0 = 0
    for part in part_refs:
        n = part.shape[1]
        cat_ref[:, c0:c0 + n] = part[...]
        c0 += n
    return jnp.dot(cat_ref[...], w_ref[...], preferred_element_type=F32)


def _inproj_kernel(*refs, n_src, ctx_rows):
    src, (g_ref, ml_ref, mc_ref, w_ref, h_ref, pr_ref) = refs[:n_src], refs[n_src:n_src + 6]
    x_ref = _residual_rows(src, refs[-1], ctx_rows)
    _prenorm_modulate(x_ref[...], g_ref, ml_ref, mc_ref, h_ref, ctx_rows, 0)
    pr_ref[...] = jnp.dot(h_ref[...], w_ref[...], preferred_element_type=F32).astype(pr_ref.dtype)


def _inproj(src, rows, g_all, mod_all, w_ret_b, layer):
    d = src[0].shape[1]
    ml_spec, mc_spec = rows.mod_specs(mod_all, layer)
    src_specs, src_scratch = _source_specs(rows, src, d)
    n_total = rows.batch * rows.seq
    widths = (d, w_ret_b.shape[2])
    return pl.pallas_call(
        functools.partial(_inproj_kernel, n_src=len(src), ctx_rows=rows.ctx_rows),
        grid=rows.grid,
        in_specs=src_specs + [_layer_spec(g_all, layer), ml_spec, mc_spec, _layer_spec(w_ret_b, layer)],
        out_specs=[rows.slab(n) for n in widths],
        out_shape=[jax.ShapeDtypeStruct((n_total, n), BF16) for n in widths],
        scratch_shapes=src_scratch,
        compiler_params=_cparams("parallel", "parallel"),
        name="mixer_inproj",
    )(*src, g_all, mod_all, mod_all, w_ret_b)


def _tile_scan(a, u, carry, row, reverse):
    n = V7X_SUBLANES
    for dist in (1, 2, 4):
        if reverse:
            keep = row < (n - dist)
            shift = n - dist
        else:
            keep = row >= dist
            shift = dist
        a_s = jnp.where(keep, pltpu.roll(a, shift, 0), 1.0)
        u_s = jnp.where(keep, pltpu.roll(u, shift, 0), 0.0)
        u = a * u_s + u
        a = a * a_s
    h = a * carry + u
    new_carry = h[0:1] if reverse else h[n - 1:n]
    return h, new_carry


def _lru_kernel(hin_ref, wl_ref, cw_ref, cb_ref, wg_ref, bg_ref, lam_ref, o_ref,
                xpad_ref, a_ref, u_ref, h_ref, ly_ref, *, ctx_len, seq):
    w = cw_ref.shape[1]
    s = seq
    pad = V7X_SUBLANES
    chunk = ctx_len
    lanes = V7X_LANES
    n_slabs = w // lanes
    zeros = jnp.zeros((pad, lanes), F32)
    for j in range(n_slabs):
        xpad_ref[j, 0:pad, :] = zeros
        xpad_ref[j, pad + ctx_len:2 * pad + ctx_len, :] = zeros
        xpad_ref[j, 2 * pad + s:3 * pad + s, :] = zeros
    proj_rows = ROW_BLOCK if s % ROW_BLOCK == 0 else chunk
    chunks_per_proj = proj_rows // chunk

    def project(blk):
        p0 = blk * proj_rows
        pr = jnp.dot(hin_ref[0, p0:p0 + proj_rows, :], wl_ref[...], preferred_element_type=F32)
        for q in range(chunks_per_proj):
            r0 = p0 + q * chunk
            base = r0 + (pad if r0 < ctx_len else 2 * pad)
            for j in range(n_slabs):
                xpad_ref[j, base:base + chunk, :] = pr[q * chunk:(q + 1) * chunk, j * lanes:(j + 1) * lanes]
            ly_ref[r0:r0 + chunk, :] = pr[q * chunk:(q + 1) * chunk, w:2 * w]

    neg = -lam_ref[...]
    softplus = jnp.maximum(neg, 0.0) + jnp.log(1.0 + jnp.exp(-jnp.abs(neg)))
    c2 = (-0.5 * LRU_C * LOG2_E) * softplus

    pitch = a_ref.shape[2] // V7X_SUBLANES
    lat_len = s - ctx_len

    def phys(d, r0):
        if d == 0:
            return r0
        return lat_len + r0 if r0 < ctx_len else r0 - ctx_len

    for d in range(2):
        for j in range(n_slabs):
            a_ref[d, j, s:, :] = jnp.ones((a_ref.shape[2] - s, lanes), F32)
            u_ref[d, j, s:, :] = jnp.zeros((a_ref.shape[2] - s, lanes), F32)

    project(0)
    for c in range(s // chunk):
        if c % chunks_per_proj == 0 and (c // chunks_per_proj + 1) * proj_rows < s:
            project(c // chunks_per_proj + 1)
        r0 = c * chunk
        base = r0 + (pad if r0 < ctx_len else 2 * pad)
        xc = []
        for j in range(n_slabs):
            lsl = slice(j * lanes, (j + 1) * lanes)
            acc = jnp.zeros((chunk, lanes), F32) + cb_ref[:, lsl]
            for k in range(LRU_CONV):
                off = k - LRU_CONV // 2
                acc = acc + xpad_ref[j, base + off:base + off + chunk, :] * cw_ref[k:k + 1, lsl]
            xc.append(acc)
        xc = jnp.concatenate(xc, axis=-1)
        th = jnp.tanh(jnp.dot(xc.astype(BF16), wg_ref[...], preferred_element_type=F32) + bg_ref[...])
        hx = 0.5 * xc
        for d in range(2):
            th_r = th[:, (2 * d) * w:(2 * d + 1) * w]
            th_i = th[:, (2 * d + 1) * w:(2 * d + 2) * w]
            cd = c2[d:d + 1, :]
            a = jnp.exp2(cd * th_r + cd)
            y = 1.0 - a * a
            u = (y * lax.rsqrt(jnp.maximum(y, TINY))) * (hx * th_i + hx)
            pr = phys(d, r0)
            for j in range(n_slabs):
                a_ref[d, j, pr:pr + chunk, :] = a[:, j * lanes:(j + 1) * lanes]
                u_ref[d, j, pr:pr + chunk, :] = u[:, j * lanes:(j + 1) * lanes]

    row = lax.broadcasted_iota(jnp.int32, (V7X_SUBLANES, lanes), 0)
    chains = [(d, j) for d in range(2) for j in range(n_slabs)]

    def seg_rows(d, t):
        tt = t if d == 0 else pitch - 1 - t
        return pl.ds(tt, V7X_SUBLANES, stride=pitch)

    def pass1(t, carry):
        out = []
        for (d, j), (h, pprod) in zip(chains, carry):
            a = a_ref[d, j, seg_rows(d, t), :]
            u = u_ref[d, j, seg_rows(d, t), :]
            out.append((a * h + u, a * pprod))
        return tuple(out)

    zero = jnp.zeros((V7X_SUBLANES, lanes), F32)
    ends = lax.fori_loop(0, pitch, pass1, tuple((zero, zero + 1.0) for _ in chains), unroll=4)

    starts = []
    for (d, j), (h_end, p_tot) in zip(chains, ends):
        incl, _ = _tile_scan(p_tot, h_end, jnp.zeros((1, lanes), F32), row, d == 1)
        if d == 0:
            starts.append(jnp.where(row >= 1, pltpu.roll(incl, 1, 0), 0.0))
        else:
            starts.append(jnp.where(row < V7X_SUBLANES - 1, pltpu.roll(incl, V7X_SUBLANES - 1, 0), 0.0))

    def pass2(t, carry):
        out = []
        for (d, j), h in zip(chains, carry):
            a = a_ref[d, j, seg_rows(d, t), :]
            u = u_ref[d, j, seg_rows(d, t), :]
            h = a * h + u
            h_ref[d, j, seg_rows(d, t), :] = h
            out.append(h)
        return tuple(out)

    lax.fori_loop(0, pitch, pass2, tuple(starts), unroll=4)

    for c in range(s // chunk):
        r0 = c * chunk
        rb = phys(1, r0)
        for j in range(n_slabs):
            ly = ly_ref[r0:r0 + chunk, j * lanes:(j + 1) * lanes]
            h = h_ref[0, j, r0:r0 + chunk, :] + h_ref[1, j, rb:rb + chunk, :]
            o_ref[0, r0:r0 + chunk, j * lanes:(j + 1) * lanes] = (_gelu_tanh(ly) * h).astype(o_ref.dtype)


def _lru(h3, w_lru_b, conv_w, conv_b, wg, bg, lam, layer, ctx_len):
    b, s, d = h3.shape
    w = w_lru_b.shape[2] // 2
    assert s % (V7X_SUBLANES * V7X_SUBLANES) == 0
    scan_rows = V7X_SUBLANES * (s // V7X_SUBLANES + 4)
    scan_buf = pltpu.VMEM((2, w // V7X_LANES, scan_rows, V7X_LANES), F32)
    return pl.pallas_call(
        functools.partial(_lru_kernel, ctx_len=ctx_len, seq=s),
        grid=(b,),
        in_specs=[pl.BlockSpec((1, s, d), lambda i: (i, 0, 0))]
        + [_layer_spec(a, layer) for a in (w_lru_b, conv_w, conv_b, wg, bg, lam)],
        out_specs=pl.BlockSpec((1, s, w), lambda i: (i, 0, 0)),
        out_shape=jax.ShapeDtypeStruct((b, s, w), BF16),
        scratch_shapes=[pltpu.VMEM((w // V7X_LANES, s + 3 * V7X_SUBLANES, V7X_LANES), F32),
                        scan_buf, scan_buf, scan_buf, pltpu.VMEM((s, w), F32)],
        compiler_params=_cparams("parallel"),
        name="rglru",
    )(h3, w_lru_b, conv_w, conv_b, wg, bg, lam)


def _rope(x, cos, sin_signed, lane_even):
    partner = jnp.where(lane_even, pltpu.roll(x, V7X_LANES - 1, 1), pltpu.roll(x, 1, 1))
    return x * cos + partner * sin_signed


def _ret_kernel(lg_ref, p_ref, cos_ref, sin_ref, lgl_ref, o_ref,
                q_ref, k_ref, dlt_ref, st_ref, *, layer, ctx_len, seq, head_dim):
    s = seq
    c = RET_CHUNK
    lanes = V7X_LANES
    w = o_ref.shape[2]
    n_pairs = w // lanes
    n_chunks = s // c
    n_ctx = ctx_len // c
    scale = head_dim ** -0.5

    rowi = lax.broadcasted_iota(jnp.int32, (c, lanes), 0).astype(F32)
    lane = lax.broadcasted_iota(jnp.int32, (c, lanes), 1)
    lane_even = (lane % 2) == 0
    low_half = lane < head_dim
    dif = (lax.broadcasted_iota(jnp.int32, (c, c), 0)
           - lax.broadcasted_iota(jnp.int32, (c, c), 1)).astype(F32)
    blk_row_low = lax.broadcasted_iota(jnp.int32, (lanes, lanes), 0) < head_dim
    blk_col_low = lax.broadcasted_iota(jnp.int32, (lanes, lanes), 1) < head_dim
    blk_diag = blk_row_low == blk_col_low

    order_b = list(range(n_ctx - 1, -1, -1)) + list(range(n_chunks - 1, n_ctx - 1, -1))

    for p in range(n_pairs):
        lsl = slice(p * lanes, (p + 1) * lanes)
        lg_f = lgl_ref[0:1, lsl]
        lg_b = lgl_ref[1:2, lsl]
        kd_f = jnp.exp(lg_f * (c - 1.0 - rowi))
        kd_b = jnp.exp(lg_b * rowi)
        qd_f = jnp.exp(lg_f * (rowi + 1.0))
        qd_b = jnp.exp(lg_b * (c - rowi))

        for ch in range(n_chunks):
            r0 = ch * c
            q = p_ref[0, r0:r0 + c, p * lanes:(p + 1) * lanes].astype(F32)
            k = p_ref[0, r0:r0 + c, w + p * lanes:w + (p + 1) * lanes].astype(F32)
            v = p_ref[0, r0:r0 + c, 2 * w + p * lanes:2 * w + (p + 1) * lanes]
            if ch >= n_ctx:
                t0 = r0 - ctx_len
                cos = cos_ref[t0:t0 + c, :]
                sin = sin_ref[t0:t0 + c, :]
                q = _rope(q, cos, sin, lane_even)
                k = _rope(k, cos, sin, lane_even)
            q_ref[r0:r0 + c, lsl] = (q * scale).astype(BF16)
            k_ref[r0:r0 + c, lsl] = k.astype(BF16)
            d_f = pl.dot((k * kd_f).astype(BF16), v, trans_a=True)
            d_b = pl.dot((k * kd_b).astype(BF16), v, trans_a=True)
            dlt_ref[0, ch] = jnp.where(blk_diag, d_f, 0.0)
            dlt_ref[1, ch] = jnp.where(blk_diag, d_b, 0.0)

        gc_f = jnp.exp(jnp.where(blk_row_low, lg_ref[layer, 0, 2 * p], lg_ref[layer, 0, 2 * p + 1]) * float(c))
        gc_b = jnp.exp(jnp.where(blk_row_low, lg_ref[layer, 1, 2 * p], lg_ref[layer, 1, 2 * p + 1]) * float(c))
        state = jnp.zeros((lanes, lanes), F32)
        for ch in range(n_chunks):
            st_ref[0, ch] = state.astype(BF16)
            state = gc_f * state + dlt_ref[0, ch]
        state = jnp.zeros((lanes, lanes), F32)
        for ch in order_b:
            st_ref[1, ch] = state.astype(BF16)
            state = gc_b * state + dlt_ref[1, ch]

        dmats = []
        for h in range(2):
            lgf_s = lg_ref[layer, 0, 2 * p + h]
            lgb_s = lg_ref[layer, 1, 2 * p + h]
            dmats.append(jnp.where(dif > 0, jnp.exp(lgf_s * jnp.maximum(dif, 0.0)),
                                   jnp.where(dif < 0, jnp.exp(lgb_s * jnp.maximum(-dif, 0.0)), 2.0)))

        def out_chunk(ch, _):
            r0 = pl.multiple_of(ch * c, c)
            rows = pl.ds(r0, c)
            q = q_ref[rows, lsl]
            k = k_ref[rows, lsl]
            v = p_ref[0, rows, 2 * w + p * lanes:2 * w + (p + 1) * lanes]
            g = p_ref[0, rows, 3 * w + p * lanes:3 * w + (p + 1) * lanes].astype(F32)
            zero = jnp.zeros_like(q)
            s0 = pl.dot(jnp.where(low_half, q, zero), k, trans_b=True)
            s1 = pl.dot(jnp.where(low_half, zero, q), k, trans_b=True)
            o0 = jnp.dot((s0 * dmats[0]).astype(BF16), v, preferred_element_type=F32)
            o1 = jnp.dot((s1 * dmats[1]).astype(BF16), v, preferred_element_type=F32)
            o = jnp.where(low_half, o0, o1)
            qf = q.astype(F32)
            o = o + jnp.dot((qf * qd_f).astype(BF16), st_ref[0, ch], preferred_element_type=F32)
            o = o + jnp.dot((qf * qd_b).astype(BF16), st_ref[1, ch], preferred_element_type=F32)
            oo = o * o
            ss0 = jnp.sum(jnp.where(low_half, oo, 0.0), axis=-1, keepdims=True)
            ss1 = jnp.sum(jnp.where(low_half, 0.0, oo), axis=-1, keepdims=True)
            inv = jnp.where(low_half, lax.rsqrt(ss0 / head_dim + EPS), lax.rsqrt(ss1 / head_dim + EPS))
            o_ref[0, rows, lsl] = (_silu(g) * (o * inv)).astype(o_ref.dtype)
            return 0

        lax.fori_loop(0, n_chunks, out_chunk, 0, unroll=True)


def _retention(p_ret, lg, lg_lane, cos, sin_signed, layer, ctx_len, head_dim):
    b, s, w4 = p_ret.shape
    w = w4 // 4
    n_chunks = s // RET_CHUNK
    return pl.pallas_call(
        functools.partial(_ret_kernel, layer=layer, ctx_len=ctx_len, seq=s, head_dim=head_dim),
        grid=(b,),
        in_specs=[
            pl.BlockSpec(memory_space=pltpu.SMEM),
            pl.BlockSpec((1, s, w4), lambda i: (i, 0, 0)),
            _const_spec(cos.shape), _const_spec(sin_signed.shape), _layer_spec(lg_lane, layer),
        ],
        out_specs=pl.BlockSpec((1, s, w), lambda i: (i, 0, 0)),
        out_shape=jax.ShapeDtypeStruct((b, s, w), BF16),
        scratch_shapes=[
            pltpu.VMEM((s, w), BF16),
            pltpu.VMEM((s, w), BF16),
            pltpu.VMEM((2, n_chunks, V7X_LANES, V7X_LANES), F32),
            pltpu.VMEM((2, n_chunks, V7X_LANES, V7X_LANES), BF16),
        ],
        compiler_params=_cparams("parallel"),
        name="retention",
    )(lg, p_ret, cos, sin_signed, lg_lane)


def _cm_kernel(hin_ref, wz_ref, lng_ref, lnb_ref, ws_ref, bs_ref, o_ref, z_ref, *, seq, group_dim):
    c = CM_CHUNK
    lanes = V7X_LANES
    w = o_ref.shape[2]
    n_slabs = w // lanes
    low_half = lax.broadcasted_iota(jnp.int32, (c, lanes), 1) < group_dim
    lane_w = lax.broadcasted_iota(jnp.int32, (c, w), 1)
    proj_rows = ROW_BLOCK if seq % ROW_BLOCK == 0 else c
    chunks_per_proj = proj_rows // c

    def project(blk):
        rows = slice(blk * proj_rows, (blk + 1) * proj_rows)
        z_ref[rows, :] = jnp.dot(hin_ref[0, rows, :], wz_ref[...], preferred_element_type=F32)

    def body(ch):
        rows = slice(ch * c, (ch + 1) * c)
        z = _gelu_tanh(z_ref[rows, :])
        u = z[:, :w]
        vn = []
        for sl in range(n_slabs):
            v = z[:, w + sl * lanes:w + (sl + 1) * lanes]
            m0 = jnp.sum(jnp.where(low_half, v, 0.0), axis=-1, keepdims=True) / group_dim
            m1 = jnp.sum(jnp.where(low_half, 0.0, v), axis=-1, keepdims=True) / group_dim
            xc = v - jnp.where(low_half, m0, m1)
            xx = xc * xc
            v0 = jnp.sum(jnp.where(low_half, xx, 0.0), axis=-1, keepdims=True) / group_dim
            v1 = jnp.sum(jnp.where(low_half, 0.0, xx), axis=-1, keepdims=True) / group_dim
            vn.append(xc * jnp.where(low_half, lax.rsqrt(v0 + EPS), lax.rsqrt(v1 + EPS)))
        vn = jnp.concatenate(vn, axis=-1) * lng_ref[...] + lnb_ref[...]
        s_all = jnp.dot(ws_ref[...], vn.astype(BF16), preferred_element_type=F32)
        sp = s_all[0:c]
        for gi in range(1, CM_GROUPS):
            sp = jnp.where(lane_w >= gi * group_dim, s_all[gi * c:(gi + 1) * c], sp)
        o_ref[0, rows, :] = (u * (sp + bs_ref[...])).astype(o_ref.dtype)

    project(0)
    for ch in range(seq // c):
        if ch % chunks_per_proj == 0 and (ch // chunks_per_proj + 1) * proj_rows < seq:
            project(ch // chunks_per_proj + 1)
        body(ch)


def _chunk_mlp(h3, w_cm_b, ln_g, ln_b, ws, bs, layer, group_dim):
    b, s, d = h3.shape
    w = w_cm_b.shape[2] // 2
    return pl.pallas_call(
        functools.partial(_cm_kernel, seq=s, group_dim=group_dim),
        grid=(b,),
        in_specs=[pl.BlockSpec((1, s, d), lambda i: (i, 0, 0))]
        + [_layer_spec(a, layer) for a in (w_cm_b, ln_g, ln_b, ws, bs)],
        out_specs=pl.BlockSpec((1, s, w), lambda i: (i, 0, 0)),
        out_shape=jax.ShapeDtypeStruct((b, s, w), BF16),
        scratch_shapes=[pltpu.VMEM((s, 2 * w), F32)],
        compiler_params=_cparams("parallel"),
        name="chunk_gmlp",
    )(h3, w_cm_b, ln_g, ln_b, ws, bs)


def _swiglu_rows(h, w1_ref, w3_ref, w2_ref):
    f = w1_ref.shape[1]
    acc = None
    for c0 in range(0, f, FF_CHUNK):
        a = jnp.dot(h, w1_ref[:, c0:c0 + FF_CHUNK], preferred_element_type=F32)
        g = jnp.dot(h, w3_ref[:, c0:c0 + FF_CHUNK], preferred_element_type=F32)
        act = (_silu(a) * g).astype(BF16)
        part = jnp.dot(act, w2_ref[c0:c0 + FF_CHUNK, :], preferred_element_type=F32)
        acc = part if acc is None else acc + part
    return acc


def _ffn_kernel(*refs, n_src, ctx_rows):
    src = refs[:n_src]
    (gmix_ref, gpre_ref, gpost_ref, ml_ref, mc_ref, pa_ref, pb_ref, pc_ref, wo_ref,
     w1_ref, w3_ref, w2_ref, o_ref, xm_ref, h_ref) = refs[n_src:n_src + 15]
    x_ref = _residual_rows(src, refs[-1], ctx_rows)
    y = _mix_out((pa_ref, pb_ref, pc_ref), wo_ref, h_ref)
    _gated_residual(x_ref[...], y, gmix_ref, ml_ref, mc_ref, xm_ref, ctx_rows, 2)
    _prenorm_modulate(xm_ref[...], gpre_ref, ml_ref, mc_ref, h_ref, ctx_rows, 3)
    y = _swiglu_rows(h_ref[...], w1_ref, w3_ref, w2_ref)
    _gated_residual(xm_ref[...], y, gpost_ref, ml_ref, mc_ref, o_ref, ctx_rows, 5)


def _mix_dense_ffn(src, rows, gains, mod_all, parts, w_out_b, w1, w3, w2, layer, j):
    d = src[0].shape[1]
    ml_spec, mc_spec = rows.mod_specs(mod_all, layer)
    src_specs, src_scratch = _source_specs(rows, src, d)
    return pl.pallas_call(
        functools.partial(_ffn_kernel, n_src=len(src), ctx_rows=rows.ctx_rows),
        grid=rows.grid,
        in_specs=src_specs + [_layer_spec(g, layer) for g in gains[1:]] + [ml_spec, mc_spec]
        + [rows.slab(a.shape[1]) for a in parts] + [_layer_spec(w_out_b, layer)]
        + [_layer_spec(w, j) for w in (w1, w3, w2)],
        out_specs=rows.slab(d),
        out_shape=jax.ShapeDtypeStruct((rows.batch * rows.seq, d), F32),
        scratch_shapes=[pltpu.VMEM((rows.rb, d), F32), pltpu.VMEM((rows.rb, d), BF16)] + src_scratch,
        input_output_aliases={0: 0} if len(src) == 1 else {},
        compiler_params=_cparams("parallel", "parallel"),
        name="mix_dense_swiglu",
    )(*src, *gains[1:], mod_all, mod_all, *parts, w_out_b, w1, w3, w2)


def _router_kernel(x_ref, gmix_ref, g_ref, ml_ref, mc_ref, pa_ref, pb_ref, pc_ref, wo_ref, wr_ref,
                   xo_ref, h_ref, idx_ref, wgt_ref, hf_ref, *, ctx_rows, n_experts):
    y = _mix_out((pa_ref, pb_ref, pc_ref), wo_ref, h_ref)
    _gated_residual(x_ref[...], y, gmix_ref, ml_ref, mc_ref, xo_ref, ctx_rows, 2)
    _prenorm_modulate(xo_ref[...], g_ref, ml_ref, mc_ref, hf_ref, ctx_rows, 3)
    h = hf_ref[...]
    h_ref[...] = h.astype(h_ref.dtype)
    logits = jnp.dot(h, wr_ref[...], preferred_element_type=F32)
    lane = lax.broadcasted_iota(jnp.int32, logits.shape, 1).astype(F32)
    neg_inf = jnp.float32(-jnp.inf)
    logits = jnp.where(lane < n_experts, logits, neg_inf)
    big = jnp.float32(V7X_LANES)
    m1 = jnp.max(logits, axis=-1, keepdims=True)
    i1 = jnp.min(jnp.where(logits == m1, lane, big), axis=-1, keepdims=True)
    rest = jnp.where(lane == i1, neg_inf, logits)
    m2 = jnp.max(rest, axis=-1, keepdims=True)
    i2 = jnp.min(jnp.where(rest == m2, lane, big), axis=-1, keepdims=True)
    e2 = jnp.exp(m2 - m1)
    w1 = 1.0 / (1.0 + e2)
    w2 = e2 / (1.0 + e2)
    out_lane = lax.broadcasted_iota(jnp.int32, idx_ref.shape, 1)
    idx_ref[...] = jnp.where(out_lane == 0, i1, i2).astype(jnp.int32)
    wgt_ref[...] = jnp.where(out_lane == 0, w1, w2)


def _mix_router(xs2, rows, gains, mod_all, parts, w_out_b, wr_pad, layer, j, n_experts):
    d = xs2.shape[1]
    n_walk = rows.batch * rows.rows
    ml_spec, mc_spec = rows.mod_specs(mod_all, layer)
    if rows.latent_only:
        x_spec, x_shape, aliases = rows.walked(d), (n_walk, d), {}
    else:
        x_spec, x_shape, aliases = rows.slab(d), xs2.shape, {0: 0}
    return pl.pallas_call(
        functools.partial(_router_kernel, ctx_rows=rows.ctx_rows, n_experts=n_experts),
        grid=rows.grid,
        in_specs=[rows.slab(d), _layer_spec(gains[1], layer), _layer_spec(gains[2], layer), ml_spec, mc_spec]
        + [rows.slab(a.shape[1]) for a in parts]
        + [_layer_spec(w_out_b, layer), _layer_spec(wr_pad, j)],
        out_specs=[x_spec, rows.walked(d), rows.walked(V7X_SUBLANES), rows.walked(V7X_SUBLANES)],
        out_shape=[jax.ShapeDtypeStruct(x_shape, F32),
                   jax.ShapeDtypeStruct((n_walk, d), BF16),
                   jax.ShapeDtypeStruct((n_walk, V7X_SUBLANES), jnp.int32),
                   jax.ShapeDtypeStruct((n_walk, V7X_SUBLANES), F32)],
        scratch_shapes=[pltpu.VMEM((rows.rb, d), F32)],
        input_output_aliases=aliases,
        compiler_params=_cparams("parallel", "parallel"),
        name="mix_moe_router",
    )(xs2, gains[1], gains[2], mod_all, mod_all, *parts, w_out_b, wr_pad)


def _grouped_kernel(te_ref, tv_ref, h_ref, w1_ref, w3_ref, w2_ref, o_ref):
    i = pl.program_id(0)

    @pl.when(tv_ref[i] != 0)
    def _():
        o_ref[...] = _swiglu_rows(h_ref[...], w1_ref, w3_ref, w2_ref).astype(o_ref.dtype)

    @pl.when(tv_ref[i] == 0)
    def _():
        o_ref[...] = jnp.zeros_like(o_ref)


def _grouped_swiglu(layer_idx, tile_expert, tile_valid, hs, w1, w3, w2):
    p, d = hs.shape
    f = w1.shape[3]
    tm = MOE_TILE
    grid_spec = pltpu.PrefetchScalarGridSpec(
        num_scalar_prefetch=2,
        grid=(p // tm,),
        in_specs=[
            pl.BlockSpec((tm, d), lambda i, te, tv: (i, 0)),
            pl.BlockSpec((None, None, d, f), lambda i, te, tv: (layer_idx, te[i], 0, 0)),
            pl.BlockSpec((None, None, d, f), lambda i, te, tv: (layer_idx, te[i], 0, 0)),
            pl.BlockSpec((None, None, f, d), lambda i, te, tv: (layer_idx, te[i], 0, 0)),
        ],
        out_specs=pl.BlockSpec((tm, d), lambda i, te, tv: (i, 0)),
    )
    return pl.pallas_call(
        _grouped_kernel,
        grid_spec=grid_spec,
        out_shape=jax.ShapeDtypeStruct((p, d), BF16),
        compiler_params=_cparams("arbitrary"),
        name="moe_grouped_swiglu",
    )(tile_expert, tile_valid, hs, w1, w3, w2)


def _combine_kernel(*refs, ctx_rows, n_in, with_proj):
    x_ref, g_ref, ml_ref, mc_ref, y0_ref, y1_ref, wgt_ref = refs[:7]
    o_ref = refs[n_in]
    wgt = wgt_ref[...]
    y = wgt[:, 0:1] * y0_ref[...].astype(F32) + wgt[:, 1:2] * y1_ref[...].astype(F32)
    _gated_residual(x_ref[...], y, g_ref, ml_ref, mc_ref, o_ref, ctx_rows, 5)
    if with_proj:
        gn_ref, mln_ref, mcn_ref, w_ref = refs[7:11]
        h_ref, pr_ref = refs[n_in + 1:n_in + 3]
        _prenorm_modulate(o_ref[...], gn_ref, mln_ref, mcn_ref, h_ref, ctx_rows, 0)
        pr_ref[...] = jnp.dot(h_ref[...], w_ref[...], preferred_element_type=F32).astype(pr_ref.dtype)


def _combine(xres, g_all, mod_all, y0, y1, wgt, layer, rows, dest, proj=None, proj_dest=None):
    d = xres.shape[1]
    ml_spec, mc_spec = rows.mod_specs(mod_all, layer)
    x_spec = rows.slab(d) if dest is None else rows.walked(d)
    in_specs = [x_spec, _layer_spec(g_all, layer), ml_spec, mc_spec,
                rows.local(d), rows.local(d), rows.walked(V7X_SUBLANES)]
    args = [xres, g_all, mod_all, mod_all, y0, y1, wgt]
    if proj is not None:
        gn_all, w_ret_b = proj
        widths = (d, w_ret_b.shape[2])
        mln_spec, mcn_spec = rows.mod_specs(mod_all, layer + 1)
        in_specs += [_layer_spec(gn_all, layer + 1), mln_spec, mcn_spec, _layer_spec(w_ret_b, layer + 1)]
        args += [gn_all, mod_all, mod_all, w_ret_b]
    if dest is None:
        out_specs, out_shapes, aliases = [rows.slab(d)], [jax.ShapeDtypeStruct(xres.shape, F32)], {0: 0}
    else:
        out_specs, out_shapes, aliases = [rows.walked(d)], [jax.ShapeDtypeStruct(dest.shape, F32)], {}
        if not isinstance(dest, jax.ShapeDtypeStruct):
            in_specs.append(pl.BlockSpec(memory_space=pl.ANY))
            args.append(dest)
            aliases = {len(args) - 1: 0}
    if proj is not None:
        n_total = rows.batch * rows.seq
        out_specs += [rows.slab(n) for n in widths]
        out_shapes += [jax.ShapeDtypeStruct((n_total, n), BF16) for n in widths]
        for k, arr in enumerate(proj_dest or ()):
            in_specs.append(pl.BlockSpec(memory_space=pl.ANY))
            args.append(arr)
            aliases[len(args) - 1] = 1 + k
    res = pl.pallas_call(
        functools.partial(_combine_kernel, ctx_rows=rows.ctx_rows, n_in=len(args), with_proj=proj is not None),
        grid=rows.grid,
        in_specs=in_specs,
        out_specs=out_specs,
        out_shape=out_shapes,
        input_output_aliases=aliases,
        compiler_params=_cparams("parallel", "parallel"),
        name="moe_combine",
    )(*args)
    return res if proj is not None else res[0]


def _gather_rows(src, idx):
    return src.at[idx].get(mode="promise_in_bounds")


def _routing_tables(e_flat, tok_flat, n_exp):
    n_assign = e_flat.shape[0]
    tm = MOE_TILE
    onehot = (e_flat[:, None] == jnp.arange(n_exp, dtype=jnp.int32)[None, :]).astype(jnp.int32)
    csum = jnp.cumsum(onehot, axis=0)
    counts = csum[-1]
    rank = jnp.sum((csum - onehot) * onehot, axis=1)
    padded = ((counts + tm - 1) // tm) * tm
    ends = jnp.cumsum(padded)
    starts = ends - padded
    pos = jnp.sum(onehot * starts[None, :], axis=1) + rank
    n_rows = n_assign + n_exp * tm
    n_tiles = n_rows // tm
    src_tok = jnp.zeros((n_rows,), jnp.int32).at[pos].set(
        tok_flat, unique_indices=True, mode="promise_in_bounds")
    tile_start = jnp.arange(n_tiles, dtype=jnp.int32) * tm
    tile_expert = jnp.minimum(jnp.sum((tile_start[:, None] >= ends[None, :]).astype(jnp.int32), axis=1),
                              n_exp - 1)
    tile_valid = (tile_start < ends[-1]).astype(jnp.int32)
    return pos, src_tok, tile_expert, tile_valid


def _mix_moe_ffn(xs2, shape, gains, mod_all, parts, w_out_b, wr_pad, w1, w3, w2, layer, j, n_exp, last,
                 proj=None):
    b, s, ctx_len = shape
    d = xs2.shape[1]
    rows = _Rows(b, s, ctx_len, latent_only=last)
    xres, h, idx, wgt = _mix_router(xs2, rows, gains, mod_all, parts, w_out_b, wr_pad, layer, j, n_exp)
    r = rows.rows
    bh = b // MOE_SPLITS
    out = jax.ShapeDtypeStruct((b * r, d), F32) if last else None
    nxt = None
    for g in range(MOE_SPLITS):
        e_flat = idx[g * bh * r:(g + 1) * bh * r, :TOP_K].reshape(bh * r * TOP_K)
        tok_flat = g * bh * r + jnp.arange(bh * r * TOP_K, dtype=jnp.int32) // TOP_K
        pos, src_tok, tile_expert, tile_valid = _routing_tables(e_flat, tok_flat, n_exp)
        hs = _gather_rows(h, src_tok)
        ys = _grouped_swiglu(j, tile_expert, tile_valid, hs, w1, w3, w2)
        pos = pos.reshape(bh * r, TOP_K)
        y0 = _gather_rows(ys, pos[:, 0])
        y1 = _gather_rows(ys, pos[:, 1])
        rows_g = _Rows(b, s, ctx_len, latent_only=last, batch0=g * bh, n_batch=bh)
        res = _combine(xres, gains[3], mod_all, y0, y1, wgt, layer, rows_g, out, proj, nxt)
        if proj is not None:
            res, nxt = res[0], tuple(res[1:])
        if last:
            out = res
        else:
            xres = res
    return (out if last else xres), nxt


def _block_diag(wh):
    h, d = wh.shape[-3], wh.shape[-1]
    eye = jnp.eye(h, dtype=wh.dtype)
    out = jnp.einsum("...hij,hg->...higj", wh, eye)
    return out.reshape(wh.shape[:-3] + (h * d, h * d))


def kernel(x, c, ctx, c_ctx, w_mod, b_mod, g_mix_pre, g_mix_post, g_ffn_pre, g_ffn_post, w_in, w_out,
           lru_conv_w, lru_conv_b, lru_wa, lru_ba, lru_wx, lru_bx, lru_lam, ret_theta,
           cm_ln_g, cm_ln_b, cm_ws, cm_bs, ffn_w1, ffn_w3, ffn_w2, router_w, moe_w1, moe_w3, moe_w2):
    b, l, d = x.shape
    lc = ctx.shape[1]
    s = lc + l
    depth = w_mod.shape[0]
    lru_w = lru_conv_w.shape[2]
    cm_w = cm_ln_g.shape[1] * cm_ln_g.shape[2]
    ret_w = (w_in.shape[2] - 2 * lru_w - 2 * cm_w) // 4
    head_dim = ret_w // RET_HEADS
    group_dim = cm_ln_g.shape[2]
    n_exp = router_w.shape[2]
    assert s % ROW_BLOCK == 0 and lc < ROW_BLOCK and lc % RET_CHUNK == 0 and l % RET_CHUNK == 0
    assert ret_w % V7X_LANES == 0 and cm_w % V7X_LANES == 0 and 2 * head_dim == V7X_LANES
    assert b % MOE_SPLITS == 0 and l % lc == 0 and l >= ROW_BLOCK

    n_rows = l // GRID_W
    rows = jnp.repeat(jnp.arange(n_rows, dtype=F32), GRID_W)
    cols = jnp.tile(jnp.arange(GRID_W, dtype=F32), n_rows)
    pairs = head_dim // 4
    freqs = ROPE_BASE ** (-jnp.arange(pairs, dtype=F32) / pairs)
    ang = jnp.concatenate([rows[:, None] * freqs, cols[:, None] * freqs], axis=-1)
    cos = jnp.tile(jnp.repeat(jnp.cos(ang), 2, axis=1), (1, V7X_LANES // head_dim))
    sin = jnp.repeat(jnp.sin(ang), 2, axis=1) * jnp.tile(jnp.array([-1.0, 1.0], F32), head_dim // 2)
    sin_signed = jnp.tile(sin, (1, V7X_LANES // head_dim))

    w_out_b = w_out.astype(BF16)
    c_ret, c_cm = 2 * lru_w, 2 * lru_w + 4 * ret_w
    w_lru_b, w_ret_b, w_cm_b = (w_in[:, :, :c_ret].astype(BF16), w_in[:, :, c_ret:c_cm].astype(BF16),
                                w_in[:, :, c_cm:].astype(BF16))
    ffn_b1, ffn_b3, ffn_b2 = ffn_w1.astype(BF16), ffn_w3.astype(BF16), ffn_w2.astype(BF16)
    moe_b1, moe_b3, moe_b2 = moe_w1.astype(BF16), moe_w3.astype(BF16), moe_w2.astype(BF16)
    wg = (0.5 * jnp.concatenate([_block_diag(lru_wa[:, 0]), _block_diag(lru_wx[:, 0]),
                                 _block_diag(lru_wa[:, 1]), _block_diag(lru_wx[:, 1])], axis=2)).astype(BF16)
    bg = 0.5 * jnp.concatenate([lru_ba[:, 0].reshape(depth, 1, lru_w), lru_bx[:, 0].reshape(depth, 1, lru_w),
                                lru_ba[:, 1].reshape(depth, 1, lru_w), lru_bx[:, 1].reshape(depth, 1, lru_w)],
                               axis=2)
    conv_b = lru_conv_b.reshape(depth, 1, lru_w)
    lam = lru_lam.reshape(depth, 2, lru_w)
    lg = jax.nn.log_sigmoid(ret_theta.astype(F32))
    lg_lane = jnp.repeat(lg, head_dim, axis=2)
    ln_g = cm_ln_g.reshape(depth, 1, cm_w)
    ln_b = cm_ln_b.reshape(depth, 1, cm_w)
    ws = cm_ws.reshape(depth, CM_GROUPS * CM_CHUNK, CM_CHUNK).astype(BF16)
    bs = jnp.repeat(jnp.swapaxes(cm_bs, 1, 2), group_dim, axis=2)
    wr_pad = jnp.zeros((router_w.shape[0], d, V7X_LANES), F32).at[:, :, :n_exp].set(router_w)
    gains = [g.reshape(depth, 1, d) for g in (g_mix_pre, g_mix_post, g_ffn_pre, g_ffn_post)]

    n_cond = ((b + 1 + V7X_SUBLANES - 1) // V7X_SUBLANES) * V7X_SUBLANES
    cpad = jnp.zeros((n_cond, d), F32).at[:b].set(c).at[b].set(c_ctx)
    mod_all = _modulation(cpad, w_mod, b_mod).reshape(depth, n_cond, N_MOD, d)

    rows = _Rows(b, s, lc)
    src = (ctx.reshape(b * lc, d), x.reshape(b * l, d))
    nxt = None
    for layer in range(depth):
        h2, p_ret = nxt or _inproj(src, rows, gains[0], mod_all, w_ret_b, layer)
        nxt = None
        h3 = h2.reshape(b, s, d)
        o_lru = _lru(h3, w_lru_b, lru_conv_w, conv_b, wg, bg, lam, layer, lc)
        o_ret = _retention(p_ret.reshape(b, s, -1), lg, lg_lane, cos, sin_signed, layer, lc, head_dim)
        o_cm = _chunk_mlp(h3, w_cm_b, ln_g, ln_b, ws, bs, layer, group_dim)
        parts = tuple(o.reshape(b * s, -1) for o in (o_lru, o_ret, o_cm))

        j = layer // 2
        last = layer == depth - 1
        if layer % 2 == 0:
            xs2 = _mix_dense_ffn(src, rows, gains, mod_all, parts, w_out_b, ffn_b1, ffn_b3, ffn_b2, layer, j)
        else:
            xs2, nxt = _mix_moe_ffn(src[0], (b, s, lc), gains, mod_all, parts, w_out_b, wr_pad,
                                    moe_b1, moe_b3, moe_b2, layer, j, n_exp, last,
                                    proj=None if last else (gains[0], w_ret_b))
            if last:
                return xs2.reshape(b, l, d)
        src = (xs2,)
    return xs2.reshape(b, s, d)[:, lc:, :]
```

```python
import functools
import math

import jax
import jax.numpy as jnp
from jax import lax
from jax.experimental import pallas as pl
from jax.experimental.pallas import tpu as pltpu

F32 = jnp.float32
BF16 = jnp.bfloat16

EPS = 1e-6
N_MOD = 6
LRU_CONV = 4
LRU_C = 8.0
RET_HEADS = 6
ROPE_BASE = 100.0
GRID_W = 64
CM_GROUPS = 4
CM_CHUNK = 128
TOP_K = 2

V7X_LANES = 128
V7X_SUBLANES = 8
V7X_VMEM_LIMIT = 56 * 1024 * 1024

ROW_BLOCK = 768
RET_CHUNK = 256
FF_CHUNK = 256
MOE_TILE = 512
LAT_BLOCK = 1024
MOE_SPLITS = 2
LOG2_E = 1.4426950408889634
TINY = 1e-30


def _cparams(*sem):
    return pltpu.CompilerParams(dimension_semantics=sem, vmem_limit_bytes=V7X_VMEM_LIMIT)


def _layer_spec(arr, layer):
    nd = arr.ndim - 1
    return pl.BlockSpec((None,) + arr.shape[1:], lambda *_: (layer,) + (0,) * nd,
                        pipeline_mode=pl.Buffered(1))


def _const_spec(shape):
    nd = len(shape)
    return pl.BlockSpec(shape, lambda *_: (0,) * nd, pipeline_mode=pl.Buffered(1))


def _rms(x):
    return x * lax.rsqrt(jnp.mean(x * x, axis=-1, keepdims=True) + EPS)


def _gelu_tanh(x):
    return 0.5 * x * (1.0 + jnp.tanh(0.7978845608028654 * (x + 0.044715 * (x * x * x))))


def _sigmoid(x):
    return 1.0 / (1.0 + jnp.exp(-x))


def _silu(x):
    return x * _sigmoid(x)


def _mod_kernel(c_ref, w_ref, b_ref, o_ref):
    s = _silu(c_ref[...])
    o_ref[...] = jnp.dot(s, w_ref[...], preferred_element_type=F32) + b_ref[...]


def _modulation(cpad, w_mod, b_mod):
    depth, d, nd = w_mod.shape
    r = cpad.shape[0]
    return pl.pallas_call(
        _mod_kernel,
        grid=(depth, nd // d),
        in_specs=[
            pl.BlockSpec((r, d), lambda l, n: (0, 0)),
            pl.BlockSpec((None, d, d), lambda l, n: (l, 0, n)),
            pl.BlockSpec((None, 1, d), lambda l, n: (l, 0, n)),
        ],
        out_specs=pl.BlockSpec((None, r, d), lambda l, n: (l, 0, n)),
        out_shape=jax.ShapeDtypeStruct((depth, r, nd), F32),
        compiler_params=_cparams("parallel", "parallel"),
        name="modulation",
    )(cpad, w_mod, b_mod.reshape(depth, 1, nd))


class _Rows:
    def __init__(self, batch, seq, ctx_len, latent_only=False, batch0=0, n_batch=None):
        self.batch, self.seq, self.ctx_len = batch, seq, ctx_len
        self.lat_len = seq - ctx_len
        self.batch0 = batch0
        self.n_batch = batch if n_batch is None else n_batch
        self.latent_only = latent_only
        if latent_only:
            self.rb = LAT_BLOCK if self.lat_len % LAT_BLOCK == 0 else ctx_len
            self.n_blk = self.lat_len // self.rb
            self.ctx_rows = 0
        else:
            self.rb = ROW_BLOCK
            self.n_blk = seq // ROW_BLOCK
            self.ctx_rows = ctx_len
        self.rows = self.rb * self.n_blk
        self.align = math.gcd(self.rb, ctx_len)

    @property
    def grid(self):
        return (self.n_batch, self.n_blk)

    def slab(self, width):
        if not self.latent_only:
            return pl.BlockSpec((self.rb, width), lambda b, j: ((b + self.batch0) * self.n_blk + j, 0))

        def index(b, j):
            row = (b + self.batch0) * self.seq + self.ctx_len + j * self.rb
            return (pl.multiple_of(row, self.align), 0)
        return pl.BlockSpec((pl.Element(self.rb), pl.Element(width)), index)

    def walked(self, width):
        return pl.BlockSpec((self.rb, width), lambda b, j: ((b + self.batch0) * self.n_blk + j, 0))

    def local(self, width):
        return pl.BlockSpec((self.rb, width), lambda b, j: (b * self.n_blk + j, 0))

    def split_source(self, width):
        assert not self.latent_only and self.lat_len >= self.rb
        ctx = pl.BlockSpec((self.ctx_len, width), lambda b, j: (b + self.batch0, 0))

        def index(b, j):
            row = (b + self.batch0) * self.lat_len + jnp.maximum(j * self.rb - self.ctx_len, 0)
            return (pl.multiple_of(row, self.align), 0)
        return ctx, pl.BlockSpec((pl.Element(self.rb), pl.Element(width)), index)

    def mod_specs(self, mod_all, layer):
        d = mod_all.shape[-1]
        lat = pl.BlockSpec((None, 1, N_MOD, d), lambda b, j: (layer, b + self.batch0, 0, 0))
        ctx = pl.BlockSpec((None, 1, N_MOD, d), lambda b, j: (layer, self.batch, 0, 0))
        return lat, ctx


def _row_mod(ml_ref, mc_ref, idx, first_block):
    lat = ml_ref[0, idx:idx + 1, :]
    top = jnp.where(first_block, mc_ref[0, idx:idx + 1, :], lat)
    return top, lat


def _prenorm_modulate(x, g_ref, ml_ref, mc_ref, h_ref, ctx_rows, shift_idx):
    first = pl.program_id(1) == 0
    sh_top, sh_lat = _row_mod(ml_ref, mc_ref, shift_idx, first)
    sc_top, sc_lat = _row_mod(ml_ref, mc_ref, shift_idx + 1, first)
    y = _rms(x) * g_ref[...]
    if ctx_rows:
        h_ref[:ctx_rows, :] = (y[:ctx_rows] * (1.0 + sc_top) + sh_top).astype(h_ref.dtype)
    h_ref[ctx_rows:, :] = (y[ctx_rows:] * (1.0 + sc_lat) + sh_lat).astype(h_ref.dtype)


def _gated_residual(x, y, g_ref, ml_ref, mc_ref, o_ref, ctx_rows, gate_idx):
    first = pl.program_id(1) == 0
    gt_top, gt_lat = _row_mod(ml_ref, mc_ref, gate_idx, first)
    r = _rms(y) * g_ref[...]
    if ctx_rows:
        o_ref[:ctx_rows, :] = x[:ctx_rows] + gt_top * r[:ctx_rows]
    o_ref[ctx_rows:, :] = x[ctx_rows:] + gt_lat * r[ctx_rows:]


def _residual_rows(src_refs, xbuf_ref, ctx_rows):
    if len(src_refs) == 1:
        return src_refs[0]
    ctx_ref, lat_ref = src_refs
    first = pl.program_id(1) == 0

    @pl.when(first)
    def _():
        xbuf_ref[:ctx_rows, :] = ctx_ref[...]
        xbuf_ref[ctx_rows:, :] = lat_ref[:xbuf_ref.shape[0] - ctx_rows, :]

    @pl.when(jnp.logical_not(first))
    def _():
        xbuf_ref[...] = lat_ref[...]

    return xbuf_ref


def _source_specs(rows, src, d):
    if len(src) == 1:
        return [rows.slab(d)], []
    return list(rows.split_source(d)), [pltpu.VMEM((rows.rb, d), F32)]


def _mix_out(part_refs, w_ref, cat_ref):
    c0 = 0
    for part in part_refs:
        n = part.shape[1]
        cat_ref[:, c0:c0 + n] = part[...]
        c0 += n
    return jnp.dot(cat_ref[...], w_ref[...], preferred_element_type=F32)


def _inproj_kernel(*refs, n_src, ctx_rows):
    src, (g_ref, ml_ref, mc_ref, w_ref, h_ref, pr_ref) = refs[:n_src], refs[n_src:n_src + 6]
    x_ref = _residual_rows(src, refs[-1], ctx_rows)
    _prenorm_modulate(x_ref[...], g_ref, ml_ref, mc_ref, h_ref, ctx_rows, 0)
    pr_ref[...] = jnp.dot(h_ref[...], w_ref[...], preferred_element_type=F32).astype(pr_ref.dtype)


def _inproj(src, rows, g_all, mod_all, w_ret_b, layer):
    d = src[0].shape[1]
    ml_spec, mc_spec = rows.mod_specs(mod_all, layer)
    src_specs, src_scratch = _source_specs(rows, src, d)
    n_total = rows.batch * rows.seq
    widths = (d, w_ret_b.shape[2])
    return pl.pallas_call(
        functools.partial(_inproj_kernel, n_src=len(src), ctx_rows=rows.ctx_rows),
        grid=rows.grid,
        in_specs=src_specs + [_layer_spec(g_all, layer), ml_spec, mc_spec, _layer_spec(w_ret_b, layer)],
        out_specs=[rows.slab(n) for n in widths],
        out_shape=[jax.ShapeDtypeStruct((n_total, n), BF16) for n in widths],
        scratch_shapes=src_scratch,
        compiler_params=_cparams("parallel", "parallel"),
        name="mixer_inproj",
    )(*src, g_all, mod_all, mod_all, w_ret_b)


def _tile_scan(a, u, carry, row, reverse):
    n = V7X_SUBLANES
    for dist in (1, 2, 4):
        if reverse:
            keep = row < (n - dist)
            shift = n - dist
        else:
            keep = row >= dist
            shift = dist
        a_s = jnp.where(keep, pltpu.roll(a, shift, 0), 1.0)
        u_s = jnp.where(keep, pltpu.roll(u, shift, 0), 0.0)
        u = a * u_s + u
        a = a * a_s
    h = a * carry + u
    new_carry = h[0:1] if reverse else h[n - 1:n]
    return h, new_carry


def _lru_kernel(hin_ref, wl_ref, cw_ref, cb_ref, wg_ref, bg_ref, lam_ref, o_ref,
                xpad_ref, a_ref, u_ref, h_ref, ly_ref, *, ctx_len, seq):
    w = cw_ref.shape[1]
    s = seq
    pad = V7X_SUBLANES
    chunk = ctx_len
    lanes = V7X_LANES
    n_slabs = w // lanes
    zeros = jnp.zeros((pad, lanes), F32)
    for j in range(n_slabs):
        xpad_ref[j, 0:pad, :] = zeros
        xpad_ref[j, pad + ctx_len:2 * pad + ctx_len, :] = zeros
        xpad_ref[j, 2 * pad + s:3 * pad + s, :] = zeros
    proj_rows = ROW_BLOCK if s % ROW_BLOCK == 0 else chunk
    chunks_per_proj = proj_rows // chunk

    def project(blk):
        p0 = blk * proj_rows
        pr = jnp.dot(hin_ref[0, p0:p0 + proj_rows, :], wl_ref[...], preferred_element_type=F32)
        for q in range(chunks_per_proj):
            r0 = p0 + q * chunk
            base = r0 + (pad if r0 < ctx_len else 2 * pad)
            for j in range(n_slabs):
                xpad_ref[j, base:base + chunk, :] = pr[q * chunk:(q + 1) * chunk, j * lanes:(j + 1) * lanes]
            ly_ref[r0:r0 + chunk, :] = pr[q * chunk:(q + 1) * chunk, w:2 * w]

    neg = -lam_ref[...]
    softplus = jnp.maximum(neg, 0.0) + jnp.log(1.0 + jnp.exp(-jnp.abs(neg)))
    c2 = (-0.5 * LRU_C * LOG2_E) * softplus

    pitch = a_ref.shape[2] // V7X_SUBLANES
    lat_len = s - ctx_len

    def phys(d, r0):
        if d == 0:
            return r0
        return lat_len + r0 if r0 < ctx_len else r0 - ctx_len

    for d in range(2):
        for j in range(n_slabs):
            a_ref[d, j, s:, :] = jnp.ones((a_ref.shape[2] - s, lanes), F32)
            u_ref[d, j, s:, :] = jnp.zeros((a_ref.shape[2] - s, lanes), F32)

    project(0)
    for c in range(s // chunk):
        if c % chunks_per_proj == 0 and (c // chunks_per_proj + 1) * proj_rows < s:
            project(c // chunks_per_proj + 1)
        r0 = c * chunk
        base = r0 + (pad if r0 < ctx_len else 2 * pad)
        xc = []
        for j in range(n_slabs):
            lsl = slice(j * lanes, (j + 1) * lanes)
            acc = jnp.zeros((chunk, lanes), F32) + cb_ref[:, lsl]
            for k in range(LRU_CONV):
                off = k - LRU_CONV // 2
                acc = acc + xpad_ref[j, base + off:base + off + chunk, :] * cw_ref[k:k + 1, lsl]
            xc.append(acc)
        for j in range(n_slabs):
            th = jnp.tanh(jnp.dot(xc[j].astype(BF16), wg_ref[j], preferred_element_type=F32) + bg_ref[j])
            hx = 0.5 * xc[j]
            for d in range(2):
                th_r = th[:, (2 * d) * lanes:(2 * d + 1) * lanes]
                th_i = th[:, (2 * d + 1) * lanes:(2 * d + 2) * lanes]
                cd = c2[d:d + 1, j * lanes:(j + 1) * lanes]
                a = jnp.exp2(cd * th_r + cd)
                y = 1.0 - a * a
                u = (y * lax.rsqrt(jnp.maximum(y, TINY))) * (hx * th_i + hx)
                pr = phys(d, r0)
                a_ref[d, j, pr:pr + chunk, :] = a
                u_ref[d, j, pr:pr + chunk, :] = u

    row = lax.broadcasted_iota(jnp.int32, (V7X_SUBLANES, lanes), 0)
    chains = [(d, j) for d in range(2) for j in range(n_slabs)]

    def seg_rows(d, t):
        tt = t if d == 0 else pitch - 1 - t
        return pl.ds(tt, V7X_SUBLANES, stride=pitch)

    def pass1(t, carry):
        out = []
        for (d, j), (h, pprod) in zip(chains, carry):
            a = a_ref[d, j, seg_rows(d, t), :]
            u = u_ref[d, j, seg_rows(d, t), :]
            out.append((a * h + u, a * pprod))
        return tuple(out)

    zero = jnp.zeros((V7X_SUBLANES, lanes), F32)
    ends = lax.fori_loop(0, pitch, pass1, tuple((zero, zero + 1.0) for _ in chains), unroll=4)

    starts = []
    for (d, j), (h_end, p_tot) in zip(chains, ends):
        incl, _ = _tile_scan(p_tot, h_end, jnp.zeros((1, lanes), F32), row, d == 1)
        if d == 0:
            starts.append(jnp.where(row >= 1, pltpu.roll(incl, 1, 0), 0.0))
        else:
            starts.append(jnp.where(row < V7X_SUBLANES - 1, pltpu.roll(incl, V7X_SUBLANES - 1, 0), 0.0))

    def pass2(t, carry):
        out = []
        for (d, j), h in zip(chains, carry):
            a = a_ref[d, j, seg_rows(d, t), :]
            u = u_ref[d, j, seg_rows(d, t), :]
            h = a * h + u
            h_ref[d, j, seg_rows(d, t), :] = h
            out.append(h)
        return tuple(out)

    lax.fori_loop(0, pitch, pass2, tuple(starts), unroll=4)

    for c in range(s // chunk):
        r0 = c * chunk
        rb = phys(1, r0)
        for j in range(n_slabs):
            ly = ly_ref[r0:r0 + chunk, j * lanes:(j + 1) * lanes]
            h = h_ref[0, j, r0:r0 + chunk, :] + h_ref[1, j, rb:rb + chunk, :]
            o_ref[0, r0:r0 + chunk, j * lanes:(j + 1) * lanes] = (_gelu_tanh(ly) * h).astype(o_ref.dtype)


def _lru(h3, w_lru_b, conv_w, conv_b, wg, bg, lam, layer, ctx_len):
    b, s, d = h3.shape
    w = w_lru_b.shape[2] // 2
    assert s % (V7X_SUBLANES * V7X_SUBLANES) == 0
    scan_rows = V7X_SUBLANES * (s // V7X_SUBLANES + 4)
    scan_buf = pltpu.VMEM((2, w // V7X_LANES, scan_rows, V7X_LANES), F32)
    return pl.pallas_call(
        functools.partial(_lru_kernel, ctx_len=ctx_len, seq=s),
        grid=(b,),
        in_specs=[pl.BlockSpec((1, s, d), lambda i: (i, 0, 0))]
        + [_layer_spec(a, layer) for a in (w_lru_b, conv_w, conv_b, wg, bg, lam)],
        out_specs=pl.BlockSpec((1, s, w), lambda i: (i, 0, 0)),
        out_shape=jax.ShapeDtypeStruct((b, s, w), BF16),
        scratch_shapes=[pltpu.VMEM((w // V7X_LANES, s + 3 * V7X_SUBLANES, V7X_LANES), F32),
                        scan_buf, scan_buf, scan_buf, pltpu.VMEM((s, w), F32)],
        compiler_params=_cparams("parallel"),
        name="rglru",
    )(h3, w_lru_b, conv_w, conv_b, wg, bg, lam)


def _rope(x, cos, sin_signed, lane_even):
    partner = jnp.where(lane_even, pltpu.roll(x, V7X_LANES - 1, 1), pltpu.roll(x, 1, 1))
    return x * cos + partner * sin_signed


def _ret_kernel(lg_ref, p_ref, cos_ref, sin_ref, lgl_ref, o_ref,
                q_ref, k_ref, dlt_ref, st_ref, *, layer, ctx_len, seq, head_dim):
    s = seq
    c = RET_CHUNK
    lanes = V7X_LANES
    w = o_ref.shape[2]
    n_pairs = w // lanes
    n_chunks = s // c
    n_ctx = ctx_len // c
    scale = head_dim ** -0.5

    rowi = lax.broadcasted_iota(jnp.int32, (c, lanes), 0).astype(F32)
    lane = lax.broadcasted_iota(jnp.int32, (c, lanes), 1)
    lane_even = (lane % 2) == 0
    low_half = lane < head_dim
    dif = (lax.broadcasted_iota(jnp.int32, (c, c), 0)
           - lax.broadcasted_iota(jnp.int32, (c, c), 1)).astype(F32)
    blk_row_low = lax.broadcasted_iota(jnp.int32, (lanes, lanes), 0) < head_dim
    blk_col_low = lax.broadcasted_iota(jnp.int32, (lanes, lanes), 1) < head_dim
    blk_diag = blk_row_low == blk_col_low

    order_b = list(range(n_ctx - 1, -1, -1)) + list(range(n_chunks - 1, n_ctx - 1, -1))

    for p in range(n_pairs):
        lsl = slice(p * lanes, (p + 1) * lanes)
        lg_f = lgl_ref[0:1, lsl]
        lg_b = lgl_ref[1:2, lsl]
        kd_f = jnp.exp(lg_f * (c - 1.0 - rowi))
        kd_b = jnp.exp(lg_b * rowi)
        qd_f = jnp.exp(lg_f * (rowi + 1.0))
        qd_b = jnp.exp(lg_b * (c - rowi))

        for ch in range(n_chunks):
            r0 = ch * c
            q = p_ref[0, r0:r0 + c, p * lanes:(p + 1) * lanes].astype(F32)
            k = p_ref[0, r0:r0 + c, w + p * lanes:w + (p + 1) * lanes].astype(F32)
            v = p_ref[0, r0:r0 + c, 2 * w + p * lanes:2 * w + (p + 1) * lanes]
            if ch >= n_ctx:
                t0 = r0 - ctx_len
                cos = cos_ref[t0:t0 + c, :]
                sin = sin_ref[t0:t0 + c, :]
                q = _rope(q, cos, sin, lane_even)
                k = _rope(k, cos, sin, lane_even)
            q_ref[r0:r0 + c, lsl] = (q * scale).astype(BF16)
            k_ref[r0:r0 + c, lsl] = k.astype(BF16)
            d_f = pl.dot((k * kd_f).astype(BF16), v, trans_a=True)
            d_b = pl.dot((k * kd_b).astype(BF16), v, trans_a=True)
            dlt_ref[0, ch] = jnp.where(blk_diag, d_f, 0.0)
            dlt_ref[1, ch] = jnp.where(blk_diag, d_b, 0.0)

        gc_f = jnp.exp(jnp.where(blk_row_low, lg_ref[layer, 0, 2 * p], lg_ref[layer, 0, 2 * p + 1]) * float(c))
        gc_b = jnp.exp(jnp.where(blk_row_low, lg_ref[layer, 1, 2 * p], lg_ref[layer, 1, 2 * p + 1]) * float(c))
        state = jnp.zeros((lanes, lanes), F32)
        for ch in range(n_chunks):
            st_ref[0, ch] = state.astype(BF16)
            state = gc_f * state + dlt_ref[0, ch]
        state = jnp.zeros((lanes, lanes), F32)
        for ch in order_b:
            st_ref[1, ch] = state.astype(BF16)
            state = gc_b * state + dlt_ref[1, ch]

        dmats = []
        for h in range(2):
            lgf_s = lg_ref[layer, 0, 2 * p + h]
            lgb_s = lg_ref[layer, 1, 2 * p + h]
            dmats.append(jnp.where(dif > 0, jnp.exp(lgf_s * jnp.maximum(dif, 0.0)),
                                   jnp.where(dif < 0, jnp.exp(lgb_s * jnp.maximum(-dif, 0.0)), 2.0)))

        def out_chunk(ch, _):
            r0 = pl.multiple_of(ch * c, c)
            rows = pl.ds(r0, c)
            q = q_ref[rows, lsl]
            k = k_ref[rows, lsl]
            v = p_ref[0, rows, 2 * w + p * lanes:2 * w + (p + 1) * lanes]
            g = p_ref[0, rows, 3 * w + p * lanes:3 * w + (p + 1) * lanes].astype(F32)
            zero = jnp.zeros_like(q)
            s0 = pl.dot(jnp.where(low_half, q, zero), k, trans_b=True)
            s1 = pl.dot(jnp.where(low_half, zero, q), k, trans_b=True)
            o0 = jnp.dot((s0 * dmats[0]).astype(BF16), v, preferred_element_type=F32)
            o1 = jnp.dot((s1 * dmats[1]).astype(BF16), v, preferred_element_type=F32)
            o = jnp.where(low_half, o0, o1)
            qf = q.astype(F32)
            o = o + jnp.dot((qf * qd_f).astype(BF16), st_ref[0, ch], preferred_element_type=F32)
            o = o + jnp.dot((qf * qd_b).astype(BF16), st_ref[1, ch], preferred_element_type=F32)
            oo = o * o
            ss0 = jnp.sum(jnp.where(low_half, oo, 0.0), axis=-1, keepdims=True)
            ss1 = jnp.sum(jnp.where(low_half, 0.0, oo), axis=-1, keepdims=True)
            inv = jnp.where(low_half, lax.rsqrt(ss0 / head_dim + EPS), lax.rsqrt(ss1 / head_dim + EPS))
            o_ref[0, rows, lsl] = (_silu(g) * (o * inv)).astype(o_ref.dtype)
            return 0

        lax.fori_loop(0, n_chunks, out_chunk, 0, unroll=True)


def _retention(p_ret, lg, lg_lane, cos, sin_signed, layer, ctx_len, head_dim):
    b, s, w4 = p_ret.shape
    w = w4 // 4
    n_chunks = s // RET_CHUNK
    return pl.pallas_call(
        functools.partial(_ret_kernel, layer=layer, ctx_len=ctx_len, seq=s, head_dim=head_dim),
        grid=(b,),
        in_specs=[
            pl.BlockSpec(memory_space=pltpu.SMEM),
            pl.BlockSpec((1, s, w4), lambda i: (i, 0, 0)),
            _const_spec(cos.shape), _const_spec(sin_signed.shape), _layer_spec(lg_lane, layer),
        ],
        out_specs=pl.BlockSpec((1, s, w), lambda i: (i, 0, 0)),
        out_shape=jax.ShapeDtypeStruct((b, s, w), BF16),
        scratch_shapes=[
            pltpu.VMEM((s, w), BF16),
            pltpu.VMEM((s, w), BF16),
            pltpu.VMEM((2, n_chunks, V7X_LANES, V7X_LANES), F32),
            pltpu.VMEM((2, n_chunks, V7X_LANES, V7X_LANES), BF16),
        ],
        compiler_params=_cparams("parallel"),
        name="retention",
    )(lg, p_ret, cos, sin_signed, lg_lane)


def _cm_kernel(hin_ref, wz_ref, lng_ref, lnb_ref, ws_ref, bs_ref, o_ref, z_ref, *, seq, group_dim):
    c = CM_CHUNK
    lanes = V7X_LANES
    w = o_ref.shape[2]
    n_slabs = w // lanes
    low_half = lax.broadcasted_iota(jnp.int32, (c, lanes), 1) < group_dim
    lane_w = lax.broadcasted_iota(jnp.int32, (c, w), 1)
    proj_rows = ROW_BLOCK if seq % ROW_BLOCK == 0 else c
    chunks_per_proj = proj_rows // c

    def project(blk):
        rows = slice(blk * proj_rows, (blk + 1) * proj_rows)
        z_ref[rows, :] = jnp.dot(hin_ref[0, rows, :], wz_ref[...], preferred_element_type=F32)

    def body(ch):
        rows = slice(ch * c, (ch + 1) * c)
        z = _gelu_tanh(z_ref[rows, :])
        u = z[:, :w]
        vn = []
        for sl in range(n_slabs):
            v = z[:, w + sl * lanes:w + (sl + 1) * lanes]
            m0 = jnp.sum(jnp.where(low_half, v, 0.0), axis=-1, keepdims=True) / group_dim
            m1 = jnp.sum(jnp.where(low_half, 0.0, v), axis=-1, keepdims=True) / group_dim
            xc = v - jnp.where(low_half, m0, m1)
            xx = xc * xc
            v0 = jnp.sum(jnp.where(low_half, xx, 0.0), axis=-1, keepdims=True) / group_dim
            v1 = jnp.sum(jnp.where(low_half, 0.0, xx), axis=-1, keepdims=True) / group_dim
            vn.append(xc * jnp.where(low_half, lax.rsqrt(v0 + EPS), lax.rsqrt(v1 + EPS)))
        vn = jnp.concatenate(vn, axis=-1) * lng_ref[...] + lnb_ref[...]
        s_all = jnp.dot(ws_ref[...], vn.astype(BF16), preferred_element_type=F32)
        sp = s_all[0:c]
        for gi in range(1, CM_GROUPS):
            sp = jnp.where(lane_w >= gi * group_dim, s_all[gi * c:(gi + 1) * c], sp)
        o_ref[0, rows, :] = (u * (sp + bs_ref[...])).astype(o_ref.dtype)

    project(0)
    for ch in range(seq // c):
        if ch % chunks_per_proj == 0 and (ch // chunks_per_proj + 1) * proj_rows < seq:
            project(ch // chunks_per_proj + 1)
        body(ch)


def _chunk_mlp(h3, w_cm_b, ln_g, ln_b, ws, bs, layer, group_dim):
    b, s, d = h3.shape
    w = w_cm_b.shape[2] // 2
    return pl.pallas_call(
        functools.partial(_cm_kernel, seq=s, group_dim=group_dim),
        grid=(b,),
        in_specs=[pl.BlockSpec((1, s, d), lambda i: (i, 0, 0))]
        + [_layer_spec(a, layer) for a in (w_cm_b, ln_g, ln_b, ws, bs)],
        out_specs=pl.BlockSpec((1, s, w), lambda i: (i, 0, 0)),
        out_shape=jax.ShapeDtypeStruct((b, s, w), BF16),
        scratch_shapes=[pltpu.VMEM((s, 2 * w), F32)],
        compiler_params=_cparams("parallel"),
        name="chunk_gmlp",
    )(h3, w_cm_b, ln_g, ln_b, ws, bs)


def _swiglu_rows(h, w1_ref, w3_ref, w2_ref):
    f = w1_ref.shape[1]
    acc = None
    for c0 in range(0, f, FF_CHUNK):
        a = jnp.dot(h, w1_ref[:, c0:c0 + FF_CHUNK], preferred_element_type=F32)
        g = jnp.dot(h, w3_ref[:, c0:c0 + FF_CHUNK], preferred_element_type=F32)
        act = (_silu(a) * g).astype(BF16)
        part = jnp.dot(act, w2_ref[c0:c0 + FF_CHUNK, :], preferred_element_type=F32)
        acc = part if acc is None else acc + part
    return acc


def _ffn_kernel(*refs, n_src, ctx_rows):
    src = refs[:n_src]
    (gmix_ref, gpre_ref, gpost_ref, ml_ref, mc_ref, pa_ref, pb_ref, pc_ref, wo_ref,
     w1_ref, w3_ref, w2_ref, o_ref, xm_ref, h_ref) = refs[n_src:n_src + 15]
    x_ref = _residual_rows(src, refs[-1], ctx_rows)
    y = _mix_out((pa_ref, pb_ref, pc_ref), wo_ref, h_ref)
    _gated_residual(x_ref[...], y, gmix_ref, ml_ref, mc_ref, xm_ref, ctx_rows, 2)
    _prenorm_modulate(xm_ref[...], gpre_ref, ml_ref, mc_ref, h_ref, ctx_rows, 3)
    y = _swiglu_rows(h_ref[...], w1_ref, w3_ref, w2_ref)
    _gated_residual(xm_ref[...], y, gpost_ref, ml_ref, mc_ref, o_ref, ctx_rows, 5)


def _mix_dense_ffn(src, rows, gains, mod_all, parts, w_out_b, w1, w3, w2, layer, j):
    d = src[0].shape[1]
    ml_spec, mc_spec = rows.mod_specs(mod_all, layer)
    src_specs, src_scratch = _source_specs(rows, src, d)
    return pl.pallas_call(
        functools.partial(_ffn_kernel, n_src=len(src), ctx_rows=rows.ctx_rows),
        grid=rows.grid,
        in_specs=src_specs + [_layer_spec(g, layer) for g in gains[1:]] + [ml_spec, mc_spec]
        + [rows.slab(a.shape[1]) for a in parts] + [_layer_spec(w_out_b, layer)]
        + [_layer_spec(w, j) for w in (w1, w3, w2)],
        out_specs=rows.slab(d),
        out_shape=jax.ShapeDtypeStruct((rows.batch * rows.seq, d), F32),
        scratch_shapes=[pltpu.VMEM((rows.rb, d), F32), pltpu.VMEM((rows.rb, d), BF16)] + src_scratch,
        input_output_aliases={0: 0} if len(src) == 1 else {},
        compiler_params=_cparams("parallel", "parallel"),
        name="mix_dense_swiglu",
    )(*src, *gains[1:], mod_all, mod_all, *parts, w_out_b, w1, w3, w2)


def _router_kernel(x_ref, gmix_ref, g_ref, ml_ref, mc_ref, pa_ref, pb_ref, pc_ref, wo_ref, wr_ref,
                   xo_ref, h_ref, idx_ref, wgt_ref, cnt_ref, hf_ref, tri_ref, run_ref,
                   *, ctx_rows, n_experts, group_batches):
    first_step = (pl.program_id(0) == 0) & (pl.program_id(1) == 0)

    @pl.when(first_step)
    def _():
        r = lax.broadcasted_iota(jnp.int32, tri_ref.shape, 0)
        c = lax.broadcasted_iota(jnp.int32, tri_ref.shape, 1)
        tri_ref[...] = jnp.where(c < r, 1.0, 0.0).astype(tri_ref.dtype)

    @pl.when((pl.program_id(0) % group_batches == 0) & (pl.program_id(1) == 0))
    def _():
        run_ref[...] = jnp.zeros_like(run_ref)

    y = _mix_out((pa_ref, pb_ref, pc_ref), wo_ref, h_ref)
    _gated_residual(x_ref[...], y, gmix_ref, ml_ref, mc_ref, xo_ref, ctx_rows, 2)
    _prenorm_modulate(xo_ref[...], g_ref, ml_ref, mc_ref, hf_ref, ctx_rows, 3)
    h = hf_ref[...]
    h_ref[...] = h.astype(h_ref.dtype)
    logits = jnp.dot(h, wr_ref[...], preferred_element_type=F32)
    lane = lax.broadcasted_iota(jnp.int32, logits.shape, 1).astype(F32)
    neg_inf = jnp.float32(-jnp.inf)
    logits = jnp.where(lane < n_experts, logits, neg_inf)
    big = jnp.float32(V7X_LANES)
    m1 = jnp.max(logits, axis=-1, keepdims=True)
    i1 = jnp.min(jnp.where(logits == m1, lane, big), axis=-1, keepdims=True)
    rest = jnp.where(lane == i1, neg_inf, logits)
    m2 = jnp.max(rest, axis=-1, keepdims=True)
    i2 = jnp.min(jnp.where(rest == m2, lane, big), axis=-1, keepdims=True)
    e2 = jnp.exp(m2 - m1)
    w1 = 1.0 / (1.0 + e2)
    w2 = e2 / (1.0 + e2)
    oh1 = jnp.where(lane == i1, 1.0, 0.0)
    oh2 = jnp.where(lane == i2, 1.0, 0.0)
    both = oh1 + oh2
    before = jnp.dot(tri_ref[...], both.astype(BF16), preferred_element_type=F32) + run_ref[0:1, :]
    r1 = jnp.sum(before * oh1, axis=-1, keepdims=True)
    r2 = jnp.sum(before * oh2, axis=-1, keepdims=True)
    run_ref[...] = run_ref[...] + jnp.sum(both, axis=0, keepdims=True)
    cnt_ref[0] = run_ref[...]
    out_lane = lax.broadcasted_iota(jnp.int32, idx_ref.shape, 1)
    idx_ref[...] = jnp.where(out_lane == 0, i1, jnp.where(out_lane == 1, i2,
                             jnp.where(out_lane == 2, r1, r2))).astype(jnp.int32)
    wgt_ref[...] = jnp.where(out_lane == 0, w1, w2)


def _mix_router(xs2, rows, gains, mod_all, parts, w_out_b, wr_pad, layer, j, n_experts):
    d = xs2.shape[1]
    n_walk = rows.batch * rows.rows
    ml_spec, mc_spec = rows.mod_specs(mod_all, layer)
    if rows.latent_only:
        x_spec, x_shape, aliases = rows.walked(d), (n_walk, d), {}
    else:
        x_spec, x_shape, aliases = rows.slab(d), xs2.shape, {0: 0}
    return pl.pallas_call(
        functools.partial(_router_kernel, ctx_rows=rows.ctx_rows, n_experts=n_experts,
                          group_batches=rows.batch // MOE_SPLITS),
        grid=rows.grid,
        in_specs=[rows.slab(d), _layer_spec(gains[1], layer), _layer_spec(gains[2], layer), ml_spec, mc_spec]
        + [rows.slab(a.shape[1]) for a in parts]
        + [_layer_spec(w_out_b, layer), _layer_spec(wr_pad, j)],
        out_specs=[x_spec, rows.walked(d), rows.walked(V7X_SUBLANES), rows.walked(V7X_SUBLANES),
                   pl.BlockSpec((1, V7X_SUBLANES, V7X_LANES), lambda b, j_: (b, 0, 0))],
        out_shape=[jax.ShapeDtypeStruct(x_shape, F32),
                   jax.ShapeDtypeStruct((n_walk, d), BF16),
                   jax.ShapeDtypeStruct((n_walk, V7X_SUBLANES), jnp.int32),
                   jax.ShapeDtypeStruct((n_walk, V7X_SUBLANES), F32),
                   jax.ShapeDtypeStruct((rows.batch, V7X_SUBLANES, V7X_LANES), F32)],
        scratch_shapes=[pltpu.VMEM((rows.rb, d), F32), pltpu.VMEM((rows.rb, rows.rb), BF16),
                        pltpu.VMEM((V7X_SUBLANES, V7X_LANES), F32)],
        input_output_aliases=aliases,
        compiler_params=_cparams("arbitrary", "arbitrary"),
        name="mix_moe_router",
    )(xs2, gains[1], gains[2], mod_all, mod_all, *parts, w_out_b, wr_pad)


def _grouped_kernel(te_ref, tv_ref, h_ref, w1_ref, w3_ref, w2_ref, o_ref):
    i = pl.program_id(0)

    @pl.when(tv_ref[i] != 0)
    def _():
        o_ref[...] = _swiglu_rows(h_ref[...], w1_ref, w3_ref, w2_ref).astype(o_ref.dtype)

    @pl.when(tv_ref[i] == 0)
    def _():
        o_ref[...] = jnp.zeros_like(o_ref)


def _grouped_swiglu(layer_idx, tile_expert, tile_valid, hs, w1, w3, w2):
    p, d = hs.shape
    f = w1.shape[3]
    tm = MOE_TILE
    grid_spec = pltpu.PrefetchScalarGridSpec(
        num_scalar_prefetch=2,
        grid=(p // tm,),
        in_specs=[
            pl.BlockSpec((tm, d), lambda i, te, tv: (i, 0)),
            pl.BlockSpec((None, None, d, f), lambda i, te, tv: (layer_idx, te[i], 0, 0)),
            pl.BlockSpec((None, None, d, f), lambda i, te, tv: (layer_idx, te[i], 0, 0)),
            pl.BlockSpec((None, None, f, d), lambda i, te, tv: (layer_idx, te[i], 0, 0)),
        ],
        out_specs=pl.BlockSpec((tm, d), lambda i, te, tv: (i, 0)),
    )
    return pl.pallas_call(
        _grouped_kernel,
        grid_spec=grid_spec,
        out_shape=jax.ShapeDtypeStruct((p, d), BF16),
        compiler_params=_cparams("arbitrary"),
        name="moe_grouped_swiglu",
    )(tile_expert, tile_valid, hs, w1, w3, w2)


def _combine_kernel(*refs, ctx_rows, n_in, with_proj):
    x_ref, g_ref, ml_ref, mc_ref, y0_ref, y1_ref, wgt_ref = refs[:7]
    o_ref = refs[n_in]
    wgt = wgt_ref[...]
    y = wgt[:, 0:1] * y0_ref[...].astype(F32) + wgt[:, 1:2] * y1_ref[...].astype(F32)
    _gated_residual(x_ref[...], y, g_ref, ml_ref, mc_ref, o_ref, ctx_rows, 5)
    if with_proj:
        gn_ref, mln_ref, mcn_ref, w_ref = refs[7:11]
        h_ref, pr_ref = refs[n_in + 1:n_in + 3]
        _prenorm_modulate(o_ref[...], gn_ref, mln_ref, mcn_ref, h_ref, ctx_rows, 0)
        pr_ref[...] = jnp.dot(h_ref[...], w_ref[...], preferred_element_type=F32).astype(pr_ref.dtype)


def _combine(xres, g_all, mod_all, y0, y1, wgt, layer, rows, dest, proj=None, proj_dest=None):
    d = xres.shape[1]
    ml_spec, mc_spec = rows.mod_specs(mod_all, layer)
    x_spec = rows.slab(d) if dest is None else rows.walked(d)
    in_specs = [x_spec, _layer_spec(g_all, layer), ml_spec, mc_spec,
                rows.local(d), rows.local(d), rows.walked(V7X_SUBLANES)]
    args = [xres, g_all, mod_all, mod_all, y0, y1, wgt]
    if proj is not None:
        gn_all, w_ret_b = proj
        widths = (d, w_ret_b.shape[2])
        mln_spec, mcn_spec = rows.mod_specs(mod_all, layer + 1)
        in_specs += [_layer_spec(gn_all, layer + 1), mln_spec, mcn_spec, _layer_spec(w_ret_b, layer + 1)]
        args += [gn_all, mod_all, mod_all, w_ret_b]
    if dest is None:
        out_specs, out_shapes, aliases = [rows.slab(d)], [jax.ShapeDtypeStruct(xres.shape, F32)], {0: 0}
    else:
        out_specs, out_shapes, aliases = [rows.walked(d)], [jax.ShapeDtypeStruct(dest.shape, F32)], {}
        if not isinstance(dest, jax.ShapeDtypeStruct):
            in_specs.append(pl.BlockSpec(memory_space=pl.ANY))
            args.append(dest)
            aliases = {len(args) - 1: 0}
    if proj is not None:
        n_total = rows.batch * rows.seq
        out_specs += [rows.slab(n) for n in widths]
        out_shapes += [jax.ShapeDtypeStruct((n_total, n), BF16) for n in widths]
        for k, arr in enumerate(proj_dest or ()):
            in_specs.append(pl.BlockSpec(memory_space=pl.ANY))
            args.append(arr)
            aliases[len(args) - 1] = 1 + k
    res = pl.pallas_call(
        functools.partial(_combine_kernel, ctx_rows=rows.ctx_rows, n_in=len(args), with_proj=proj is not None),
        grid=rows.grid,
        in_specs=in_specs,
        out_specs=out_specs,
        out_shape=out_shapes,
        input_output_aliases=aliases,
        compiler_params=_cparams("parallel", "parallel"),
        name="moe_combine",
    )(*args)
    return res if proj is not None else res[0]


def _gather_rows(src, idx):
    return src.at[idx].get(mode="promise_in_bounds")


def _routing_tables(e_flat, rank, counts, tok_flat, n_exp):
    n_assign = e_flat.shape[0]
    tm = MOE_TILE
    onehot = (e_flat[:, None] == jnp.arange(n_exp, dtype=jnp.int32)[None, :]).astype(jnp.int32)
    padded = ((counts + tm - 1) // tm) * tm
    ends = jnp.cumsum(padded)
    starts = ends - padded
    pos = jnp.sum(onehot * starts[None, :], axis=1) + rank
    n_rows = n_assign + n_exp * tm
    n_tiles = n_rows // tm
    src_tok = jnp.zeros((n_rows,), jnp.int32).at[pos].set(
        tok_flat, unique_indices=True, mode="promise_in_bounds")
    tile_start = jnp.arange(n_tiles, dtype=jnp.int32) * tm
    tile_expert = jnp.minimum(jnp.sum((tile_start[:, None] >= ends[None, :]).astype(jnp.int32), axis=1),
                              n_exp - 1)
    tile_valid = (tile_start < ends[-1]).astype(jnp.int32)
    return pos, src_tok, tile_expert, tile_valid


def _mix_moe_ffn(xs2, shape, gains, mod_all, parts, w_out_b, wr_pad, w1, w3, w2, layer, j, n_exp, last,
                 proj=None):
    b, s, ctx_len = shape
    d = xs2.shape[1]
    rows = _Rows(b, s, ctx_len, latent_only=last)
    xres, h, idx, wgt, cnt = _mix_router(xs2, rows, gains, mod_all, parts, w_out_b, wr_pad, layer, j, n_exp)
    r = rows.rows
    bh = b // MOE_SPLITS
    out = jax.ShapeDtypeStruct((b * r, d), F32) if last else None
    nxt = None
    for g in range(MOE_SPLITS):
        idx_g = idx[g * bh * r:(g + 1) * bh * r]
        e_flat = idx_g[:, :TOP_K].reshape(bh * r * TOP_K)
        rank = idx_g[:, TOP_K:2 * TOP_K].reshape(bh * r * TOP_K)
        counts = cnt[(g + 1) * bh - 1, 0, :n_exp].astype(jnp.int32)
        tok_flat = g * bh * r + jnp.arange(bh * r * TOP_K, dtype=jnp.int32) // TOP_K
        pos, src_tok, tile_expert, tile_valid = _routing_tables(e_flat, rank, counts, tok_flat, n_exp)
        hs = _gather_rows(h, src_tok)
        ys = _grouped_swiglu(j, tile_expert, tile_valid, hs, w1, w3, w2)
        pos = pos.reshape(bh * r, TOP_K)
        y0 = _gather_rows(ys, pos[:, 0])
        y1 = _gather_rows(ys, pos[:, 1])
        rows_g = _Rows(b, s, ctx_len, latent_only=last, batch0=g * bh, n_batch=bh)
        res = _combine(xres, gains[3], mod_all, y0, y1, wgt, layer, rows_g, out, proj, nxt)
        if proj is not None:
            res, nxt = res[0], tuple(res[1:])
        if last:
            out = res
        else:
            xres = res
    return (out if last else xres), nxt


def _block_diag(wh):
    h, d = wh.shape[-3], wh.shape[-1]
    eye = jnp.eye(h, dtype=wh.dtype)
    out = jnp.einsum("...hij,hg->...higj", wh, eye)
    return out.reshape(wh.shape[:-3] + (h * d, h * d))


def kernel(x, c, ctx, c_ctx, w_mod, b_mod, g_mix_pre, g_mix_post, g_ffn_pre, g_ffn_post, w_in, w_out,
           lru_conv_w, lru_conv_b, lru_wa, lru_ba, lru_wx, lru_bx, lru_lam, ret_theta,
           cm_ln_g, cm_ln_b, cm_ws, cm_bs, ffn_w1, ffn_w3, ffn_w2, router_w, moe_w1, moe_w3, moe_w2):
    b, l, d = x.shape
    lc = ctx.shape[1]
    s = lc + l
    depth = w_mod.shape[0]
    lru_w = lru_conv_w.shape[2]
    cm_w = cm_ln_g.shape[1] * cm_ln_g.shape[2]
    ret_w = (w_in.shape[2] - 2 * lru_w - 2 * cm_w) // 4
    head_dim = ret_w // RET_HEADS
    group_dim = cm_ln_g.shape[2]
    n_exp = router_w.shape[2]
    assert s % ROW_BLOCK == 0 and lc < ROW_BLOCK and lc % RET_CHUNK == 0 and l % RET_CHUNK == 0
    assert ret_w % V7X_LANES == 0 and cm_w % V7X_LANES == 0 and 2 * head_dim == V7X_LANES
    assert b % MOE_SPLITS == 0 and l % lc == 0 and l >= ROW_BLOCK

    n_rows = l // GRID_W
    rows = jnp.repeat(jnp.arange(n_rows, dtype=F32), GRID_W)
    cols = jnp.tile(jnp.arange(GRID_W, dtype=F32), n_rows)
    pairs = head_dim // 4
    freqs = ROPE_BASE ** (-jnp.arange(pairs, dtype=F32) / pairs)
    ang = jnp.concatenate([rows[:, None] * freqs, cols[:, None] * freqs], axis=-1)
    cos = jnp.tile(jnp.repeat(jnp.cos(ang), 2, axis=1), (1, V7X_LANES // head_dim))
    sin = jnp.repeat(jnp.sin(ang), 2, axis=1) * jnp.tile(jnp.array([-1.0, 1.0], F32), head_dim // 2)
    sin_signed = jnp.tile(sin, (1, V7X_LANES // head_dim))

    w_out_b = w_out.astype(BF16)
    c_ret, c_cm = 2 * lru_w, 2 * lru_w + 4 * ret_w
    w_lru_b, w_ret_b, w_cm_b = (w_in[:, :, :c_ret].astype(BF16), w_in[:, :, c_ret:c_cm].astype(BF16),
                                w_in[:, :, c_cm:].astype(BF16))
    ffn_b1, ffn_b3, ffn_b2 = ffn_w1.astype(BF16), ffn_w3.astype(BF16), ffn_w2.astype(BF16)
    moe_b1, moe_b3, moe_b2 = moe_w1.astype(BF16), moe_w3.astype(BF16), moe_w2.astype(BF16)
    n_slab = lru_w // V7X_LANES
    hd = lru_wa.shape[-1]
    per_slab = V7X_LANES // hd

    def slab_blocks(wh):
        return _block_diag(wh.reshape(depth, n_slab, per_slab, hd, hd))

    wg = (0.5 * jnp.concatenate([slab_blocks(lru_wa[:, 0]), slab_blocks(lru_wx[:, 0]),
                                 slab_blocks(lru_wa[:, 1]), slab_blocks(lru_wx[:, 1])], axis=3)).astype(BF16)
    bg = 0.5 * jnp.concatenate([t.reshape(depth, n_slab, 1, V7X_LANES)
                                for t in (lru_ba[:, 0], lru_bx[:, 0], lru_ba[:, 1], lru_bx[:, 1])], axis=3)
    conv_b = lru_conv_b.reshape(depth, 1, lru_w)
    lam = lru_lam.reshape(depth, 2, lru_w)
    lg = jax.nn.log_sigmoid(ret_theta.astype(F32))
    lg_lane = jnp.repeat(lg, head_dim, axis=2)
    ln_g = cm_ln_g.reshape(depth, 1, cm_w)
    ln_b = cm_ln_b.reshape(depth, 1, cm_w)
    ws = cm_ws.reshape(depth, CM_GROUPS * CM_CHUNK, CM_CHUNK).astype(BF16)
    bs = jnp.repeat(jnp.swapaxes(cm_bs, 1, 2), group_dim, axis=2)
    wr_pad = jnp.zeros((router_w.shape[0], d, V7X_LANES), F32).at[:, :, :n_exp].set(router_w)
    gains = [g.reshape(depth, 1, d) for g in (g_mix_pre, g_mix_post, g_ffn_pre, g_ffn_post)]

    n_cond = ((b + 1 + V7X_SUBLANES - 1) // V7X_SUBLANES) * V7X_SUBLANES
    cpad = jnp.zeros((n_cond, d), F32).at[:b].set(c).at[b].set(c_ctx)
    mod_all = _modulation(cpad, w_mod, b_mod).reshape(depth, n_cond, N_MOD, d)

    rows = _Rows(b, s, lc)
    src = (ctx.reshape(b * lc, d), x.reshape(b * l, d))
    nxt = None
    for layer in range(depth):
        h2, p_ret = nxt or _inproj(src, rows, gains[0], mod_all, w_ret_b, layer)
        nxt = None
        h3 = h2.reshape(b, s, d)
        o_lru = _lru(h3, w_lru_b, lru_conv_w, conv_b, wg, bg, lam, layer, lc)
        o_ret = _retention(p_ret.reshape(b, s, -1), lg, lg_lane, cos, sin_signed, layer, lc, head_dim)
        o_cm = _chunk_mlp(h3, w_cm_b, ln_g, ln_b, ws, bs, layer, group_dim)
        parts = tuple(o.reshape(b * s, -1) for o in (o_lru, o_ret, o_cm))

        j = layer // 2
        last = layer == depth - 1
        if layer % 2 == 0:
            xs2 = _mix_dense_ffn(src, rows, gains, mod_all, parts, w_out_b, ffn_b1, ffn_b3, ffn_b2, layer, j)
        else:
            xs2, nxt = _mix_moe_ffn(src[0], (b, s, lc), gains, mod_all, parts, w_out_b, wr_pad,
                                    moe_b1, moe_b3, moe_b2, layer, j, n_exp, last,
                                    proj=None if last else (gains[0], w_ret_b))
            if last:
                return xs2.reshape(b, l, d)
        src = (xs2,)
    return xs2.reshape(b, s, d)[:, lc:, :]
```

```python
import functools
import math

import jax
import jax.numpy as jnp
from jax import lax
from jax.experimental import pallas as pl
from jax.experimental.pallas import tpu as pltpu

F32 = jnp.float32
BF16 = jnp.bfloat16

EPS = 1e-6
N_MOD = 6
LRU_CONV = 4
LRU_C = 8.0
RET_HEADS = 6
ROPE_BASE = 100.0
GRID_W = 64
CM_GROUPS = 4
CM_CHUNK = 128
TOP_K = 2

V7X_LANES = 128
V7X_SUBLANES = 8
V7X_VMEM_LIMIT = 56 * 1024 * 1024

ROW_BLOCK = 768
RET_CHUNK = 256
FF_CHUNK = 256
MOE_TILE = 512
LAT_BLOCK = 1024
MOE_SPLITS = 2
LOG2_E = 1.4426950408889634
TINY = 1e-30


def _cparams(*sem):
    return pltpu.CompilerParams(dimension_semantics=sem, vmem_limit_bytes=V7X_VMEM_LIMIT)


def _layer_spec(arr, layer):
    nd = arr.ndim - 1
    return pl.BlockSpec((None,) + arr.shape[1:], lambda *_: (layer,) + (0,) * nd,
                        pipeline_mode=pl.Buffered(1))


def _const_spec(shape):
    nd = len(shape)
    return pl.BlockSpec(shape, lambda *_: (0,) * nd, pipeline_mode=pl.Buffered(1))


def _rms(x):
    return x * lax.rsqrt(jnp.mean(x * x, axis=-1, keepdims=True) + EPS)


def _gelu_tanh(x):
    return 0.5 * x * (1.0 + jnp.tanh(0.7978845608028654 * (x + 0.044715 * (x * x * x))))


def _sigmoid(x):
    return 1.0 / (1.0 + jnp.exp(-x))


def _silu(x):
    return x * _sigmoid(x)


def _mod_kernel(c_ref, w_ref, b_ref, o_ref):
    s = _silu(c_ref[...])
    o_ref[...] = jnp.dot(s, w_ref[...], preferred_element_type=F32) + b_ref[...]


def _modulation(cpad, w_mod, b_mod):
    depth, d, nd = w_mod.shape
    r = cpad.shape[0]
    return pl.pallas_call(
        _mod_kernel,
        grid=(depth, nd // d),
        in_specs=[
            pl.BlockSpec((r, d), lambda l, n: (0, 0)),
            pl.BlockSpec((None, d, d), lambda l, n: (l, 0, n)),
            pl.BlockSpec((None, 1, d), lambda l, n: (l, 0, n)),
        ],
        out_specs=pl.BlockSpec((None, r, d), lambda l, n: (l, 0, n)),
        out_shape=jax.ShapeDtypeStruct((depth, r, nd), F32),
        compiler_params=_cparams("parallel", "parallel"),
        name="modulation",
    )(cpad, w_mod, b_mod.reshape(depth, 1, nd))


class _Rows:
    def __init__(self, batch, seq, ctx_len, latent_only=False, batch0=0, n_batch=None):
        self.batch, self.seq, self.ctx_len = batch, seq, ctx_len
        self.lat_len = seq - ctx_len
        self.batch0 = batch0
        self.n_batch = batch if n_batch is None else n_batch
        self.latent_only = latent_only
        if latent_only:
            self.rb = LAT_BLOCK if self.lat_len % LAT_BLOCK == 0 else ctx_len
            self.n_blk = self.lat_len // self.rb
            self.ctx_rows = 0
        else:
            self.rb = ROW_BLOCK
            self.n_blk = seq // ROW_BLOCK
            self.ctx_rows = ctx_len
        self.rows = self.rb * self.n_blk
        self.align = math.gcd(self.rb, ctx_len)

    @property
    def grid(self):
        return (self.n_batch, self.n_blk)

    def slab(self, width):
        if not self.latent_only:
            return pl.BlockSpec((self.rb, width), lambda b, j: ((b + self.batch0) * self.n_blk + j, 0))

        def index(b, j):
            row = (b + self.batch0) * self.seq + self.ctx_len + j * self.rb
            return (pl.multiple_of(row, self.align), 0)
        return pl.BlockSpec((pl.Element(self.rb), pl.Element(width)), index)

    def walked(self, width):
        return pl.BlockSpec((self.rb, width), lambda b, j: ((b + self.batch0) * self.n_blk + j, 0))

    def local(self, width):
        return pl.BlockSpec((self.rb, width), lambda b, j: (b * self.n_blk + j, 0))

    def split_source(self, width):
        assert not self.latent_only and self.lat_len >= self.rb
        ctx = pl.BlockSpec((self.ctx_len, width), lambda b, j: (b + self.batch0, 0))

        def index(b, j):
            row = (b + self.batch0) * self.lat_len + jnp.maximum(j * self.rb - self.ctx_len, 0)
            return (pl.multiple_of(row, self.align), 0)
        return ctx, pl.BlockSpec((pl.Element(self.rb), pl.Element(width)), index)

    def mod_specs(self, mod_all, layer):
        d = mod_all.shape[-1]
        lat = pl.BlockSpec((None, 1, N_MOD, d), lambda b, j: (layer, b + self.batch0, 0, 0))
        ctx = pl.BlockSpec((None, 1, N_MOD, d), lambda b, j: (layer, self.batch, 0, 0))
        return lat, ctx


def _row_mod(ml_ref, mc_ref, idx, first_block):
    lat = ml_ref[0, idx:idx + 1, :]
    top = jnp.where(first_block, mc_ref[0, idx:idx + 1, :], lat)
    return top, lat


def _prenorm_modulate(x, g_ref, ml_ref, mc_ref, h_ref, ctx_rows, shift_idx):
    first = pl.program_id(1) == 0
    sh_top, sh_lat = _row_mod(ml_ref, mc_ref, shift_idx, first)
    sc_top, sc_lat = _row_mod(ml_ref, mc_ref, shift_idx + 1, first)
    y = _rms(x) * g_ref[...]
    if ctx_rows:
        h_ref[:ctx_rows, :] = (y[:ctx_rows] * (1.0 + sc_top) + sh_top).astype(h_ref.dtype)
    h_ref[ctx_rows:, :] = (y[ctx_rows:] * (1.0 + sc_lat) + sh_lat).astype(h_ref.dtype)


def _gated_residual(x, y, g_ref, ml_ref, mc_ref, o_ref, ctx_rows, gate_idx):
    first = pl.program_id(1) == 0
    gt_top, gt_lat = _row_mod(ml_ref, mc_ref, gate_idx, first)
    r = _rms(y) * g_ref[...]
    if ctx_rows:
        o_ref[:ctx_rows, :] = x[:ctx_rows] + gt_top * r[:ctx_rows]
    o_ref[ctx_rows:, :] = x[ctx_rows:] + gt_lat * r[ctx_rows:]


def _residual_rows(src_refs, xbuf_ref, ctx_rows):
    if len(src_refs) == 1:
        return src_refs[0]
    ctx_ref, lat_ref = src_refs
    first = pl.program_id(1) == 0

    @pl.when(first)
    def _():
        xbuf_ref[:ctx_rows, :] = ctx_ref[...]
        xbuf_ref[ctx_rows:, :] = lat_ref[:xbuf_ref.shape[0] - ctx_rows, :]

    @pl.when(jnp.logical_not(first))
    def _():
        xbuf_ref[...] = lat_ref[...]

    return xbuf_ref


def _source_specs(rows, src, d):
    if len(src) == 1:
        return [rows.slab(d)], []
    return list(rows.split_source(d)), [pltpu.VMEM((rows.rb, d), F32)]


def _mix_out(part_refs, w_ref, cat_ref):
    c0 = 0
    for part in part_refs:
        n = part.shape[1]
        cat_ref[:, c0:c0 + n] = part[...]
        c0 += n
    return jnp.dot(cat_ref[...], w_ref[...], preferred_element_type=F32)


def _inproj_kernel(*refs, n_src, ctx_rows):
    src, (g_ref, ml_ref, mc_ref, w_ref, h_ref, pr_ref) = refs[:n_src], refs[n_src:n_src + 6]
    x_ref = _residual_rows(src, refs[-1], ctx_rows)
    _prenorm_modulate(x_ref[...], g_ref, ml_ref, mc_ref, h_ref, ctx_rows, 0)
    pr_ref[...] = jnp.dot(h_ref[...], w_ref[...], preferred_element_type=F32).astype(pr_ref.dtype)


def _inproj(src, rows, g_all, mod_all, w_ret_b, layer):
    d = src[0].shape[1]
    ml_spec, mc_spec = rows.mod_specs(mod_all, layer)
    src_specs, src_scratch = _source_specs(rows, src, d)
    n_total = rows.batch * rows.seq
    widths = (d, w_ret_b.shape[2])
    return pl.pallas_call(
        functools.partial(_inproj_kernel, n_src=len(src), ctx_rows=rows.ctx_rows),
        grid=rows.grid,
        in_specs=src_specs + [_layer_spec(g_all, layer), ml_spec, mc_spec, _layer_spec(w_ret_b, layer)],
        out_specs=[rows.slab(n) for n in widths],
        out_shape=[jax.ShapeDtypeStruct((n_total, n), BF16) for n in widths],
        scratch_shapes=src_scratch,
        compiler_params=_cparams("parallel", "parallel"),
        name="mixer_inproj",
    )(*src, g_all, mod_all, mod_all, w_ret_b)


def _tile_scan(a, u, carry, row, reverse):
    n = V7X_SUBLANES
    for dist in (1, 2, 4):
        if reverse:
            keep = row < (n - dist)
            shift = n - dist
        else:
            keep = row >= dist
            shift = dist
        a_s = jnp.where(keep, pltpu.roll(a, shift, 0), 1.0)
        u_s = jnp.where(keep, pltpu.roll(u, shift, 0), 0.0)
        u = a * u_s + u
        a = a * a_s
    h = a * carry + u
    new_carry = h[0:1] if reverse else h[n - 1:n]
    return h, new_carry


def _lru_kernel(hin_ref, wl_ref, cw_ref, cb_ref, wg_ref, bg_ref, lam_ref, o_ref,
                xpad_ref, a_ref, u_ref, h_ref, ly_ref, *, ctx_len, seq):
    w = cw_ref.shape[1]
    s = seq
    pad = V7X_SUBLANES
    chunk = ctx_len
    lanes = V7X_LANES
    n_slabs = w // lanes
    zeros = jnp.zeros((pad, lanes), F32)
    for j in range(n_slabs):
        xpad_ref[j, 0:pad, :] = zeros
        xpad_ref[j, pad + ctx_len:2 * pad + ctx_len, :] = zeros
        xpad_ref[j, 2 * pad + s:3 * pad + s, :] = zeros
    proj_rows = ROW_BLOCK if s % ROW_BLOCK == 0 else chunk
    chunks_per_proj = proj_rows // chunk

    def project(blk):
        p0 = blk * proj_rows
        pr = jnp.dot(hin_ref[0, p0:p0 + proj_rows, :], wl_ref[...], preferred_element_type=F32)
        for q in range(chunks_per_proj):
            r0 = p0 + q * chunk
            base = r0 + (pad if r0 < ctx_len else 2 * pad)
            for j in range(n_slabs):
                xpad_ref[j, base:base + chunk, :] = pr[q * chunk:(q + 1) * chunk, j * lanes:(j + 1) * lanes]
            ly_ref[r0:r0 + chunk, :] = pr[q * chunk:(q + 1) * chunk, w:2 * w]

    neg = -lam_ref[...]
    softplus = jnp.maximum(neg, 0.0) + jnp.log(1.0 + jnp.exp(-jnp.abs(neg)))
    c2 = (-0.5 * LRU_C * LOG2_E) * softplus

    pitch = a_ref.shape[2] // V7X_SUBLANES
    lat_len = s - ctx_len

    def phys(d, r0):
        if d == 0:
            return r0
        return lat_len + r0 if r0 < ctx_len else r0 - ctx_len

    for d in range(2):
        for j in range(n_slabs):
            a_ref[d, j, s:, :] = jnp.ones((a_ref.shape[2] - s, lanes), F32)
            u_ref[d, j, s:, :] = jnp.zeros((a_ref.shape[2] - s, lanes), F32)

    project(0)
    for c in range(s // chunk):
        if c % chunks_per_proj == 0 and (c // chunks_per_proj + 1) * proj_rows < s:
            project(c // chunks_per_proj + 1)
        r0 = c * chunk
        base = r0 + (pad if r0 < ctx_len else 2 * pad)
        xc = []
        for j in range(n_slabs):
            lsl = slice(j * lanes, (j + 1) * lanes)
            acc = jnp.zeros((chunk, lanes), F32) + cb_ref[:, lsl]
            for k in range(LRU_CONV):
                off = k - LRU_CONV // 2
                acc = acc + xpad_ref[j, base + off:base + off + chunk, :] * cw_ref[k:k + 1, lsl]
            xc.append(acc)
        for j in range(n_slabs):
            th = jnp.tanh(jnp.dot(xc[j].astype(BF16), wg_ref[j], preferred_element_type=F32) + bg_ref[j])
            hx = 0.5 * xc[j]
            for d in range(2):
                th_r = th[:, (2 * d) * lanes:(2 * d + 1) * lanes]
                th_i = th[:, (2 * d + 1) * lanes:(2 * d + 2) * lanes]
                cd = c2[d:d + 1, j * lanes:(j + 1) * lanes]
                a = jnp.exp2(cd * th_r + cd)
                y = 1.0 - a * a
                u = (y * lax.rsqrt(jnp.maximum(y, TINY))) * (hx * th_i + hx)
                pr = phys(d, r0)
                a_ref[d, j, pr:pr + chunk, :] = a
                u_ref[d, j, pr:pr + chunk, :] = u

    row = lax.broadcasted_iota(jnp.int32, (V7X_SUBLANES, lanes), 0)
    chains = [(d, j) for d in range(2) for j in range(n_slabs)]

    def seg_rows(d, t):
        tt = t if d == 0 else pitch - 1 - t
        return pl.ds(tt, V7X_SUBLANES, stride=pitch)

    def pass1(t, carry):
        out = []
        for (d, j), (h, pprod) in zip(chains, carry):
            a = a_ref[d, j, seg_rows(d, t), :]
            u = u_ref[d, j, seg_rows(d, t), :]
            out.append((a * h + u, a * pprod))
        return tuple(out)

    zero = jnp.zeros((V7X_SUBLANES, lanes), F32)
    ends = lax.fori_loop(0, pitch, pass1, tuple((zero, zero + 1.0) for _ in chains), unroll=4)

    starts = []
    for (d, j), (h_end, p_tot) in zip(chains, ends):
        incl, _ = _tile_scan(p_tot, h_end, jnp.zeros((1, lanes), F32), row, d == 1)
        if d == 0:
            starts.append(jnp.where(row >= 1, pltpu.roll(incl, 1, 0), 0.0))
        else:
            starts.append(jnp.where(row < V7X_SUBLANES - 1, pltpu.roll(incl, V7X_SUBLANES - 1, 0), 0.0))

    def pass2(t, carry):
        out = []
        for (d, j), h in zip(chains, carry):
            a = a_ref[d, j, seg_rows(d, t), :]
            u = u_ref[d, j, seg_rows(d, t), :]
            h = a * h + u
            h_ref[d, j, seg_rows(d, t), :] = h
            out.append(h)
        return tuple(out)

    lax.fori_loop(0, pitch, pass2, tuple(starts), unroll=4)

    for c in range(s // chunk):
        r0 = c * chunk
        rb = phys(1, r0)
        for j in range(n_slabs):
            ly = ly_ref[r0:r0 + chunk, j * lanes:(j + 1) * lanes]
            h = h_ref[0, j, r0:r0 + chunk, :] + h_ref[1, j, rb:rb + chunk, :]
            o_ref[0, r0:r0 + chunk, j * lanes:(j + 1) * lanes] = (_gelu_tanh(ly) * h).astype(o_ref.dtype)


def _lru(h3, w_lru_b, conv_w, conv_b, wg, bg, lam, layer, ctx_len):
    b, s, d = h3.shape
    w = w_lru_b.shape[2] // 2
    assert s % (V7X_SUBLANES * V7X_SUBLANES) == 0
    scan_rows = V7X_SUBLANES * (s // V7X_SUBLANES + 4)
    scan_buf = pltpu.VMEM((2, w // V7X_LANES, scan_rows, V7X_LANES), F32)
    return pl.pallas_call(
        functools.partial(_lru_kernel, ctx_len=ctx_len, seq=s),
        grid=(b,),
        in_specs=[pl.BlockSpec((1, s, d), lambda i: (i, 0, 0))]
        + [_layer_spec(a, layer) for a in (w_lru_b, conv_w, conv_b, wg, bg, lam)],
        out_specs=pl.BlockSpec((1, s, w), lambda i: (i, 0, 0)),
        out_shape=jax.ShapeDtypeStruct((b, s, w), BF16),
        scratch_shapes=[pltpu.VMEM((w // V7X_LANES, s + 3 * V7X_SUBLANES, V7X_LANES), F32),
                        scan_buf, scan_buf, scan_buf, pltpu.VMEM((s, w), F32)],
        compiler_params=_cparams("parallel"),
        name="rglru",
    )(h3, w_lru_b, conv_w, conv_b, wg, bg, lam)


def _rope(x, cos, sin_signed, lane_even):
    partner = jnp.where(lane_even, pltpu.roll(x, V7X_LANES - 1, 1), pltpu.roll(x, 1, 1))
    return x * cos + partner * sin_signed


def _ret_kernel(lg_ref, p_ref, cos_ref, sin_ref, lgl_ref, o_ref,
                q_ref, k_ref, dlt_ref, st_ref, *, layer, ctx_len, seq, head_dim):
    s = seq
    c = RET_CHUNK
    lanes = V7X_LANES
    w = o_ref.shape[2]
    n_pairs = w // lanes
    n_chunks = s // c
    n_ctx = ctx_len // c
    scale = head_dim ** -0.5

    rowi = lax.broadcasted_iota(jnp.int32, (c, lanes), 0).astype(F32)
    lane = lax.broadcasted_iota(jnp.int32, (c, lanes), 1)
    lane_even = (lane % 2) == 0
    low_half = lane < head_dim
    dif = (lax.broadcasted_iota(jnp.int32, (c, c), 0)
           - lax.broadcasted_iota(jnp.int32, (c, c), 1)).astype(F32)
    blk_row_low = lax.broadcasted_iota(jnp.int32, (lanes, lanes), 0) < head_dim
    blk_col_low = lax.broadcasted_iota(jnp.int32, (lanes, lanes), 1) < head_dim
    blk_diag = blk_row_low == blk_col_low

    order_b = list(range(n_ctx - 1, -1, -1)) + list(range(n_chunks - 1, n_ctx - 1, -1))

    for p in range(n_pairs):
        lsl = slice(p * lanes, (p + 1) * lanes)
        lg_f = lgl_ref[0:1, lsl]
        lg_b = lgl_ref[1:2, lsl]
        kd_f = jnp.exp(lg_f * (c - 1.0 - rowi))
        kd_b = jnp.exp(lg_b * rowi)
        qd_f = jnp.exp(lg_f * (rowi + 1.0))
        qd_b = jnp.exp(lg_b * (c - rowi))

        for ch in range(n_chunks):
            r0 = ch * c
            q = p_ref[0, r0:r0 + c, p * lanes:(p + 1) * lanes].astype(F32)
            k = p_ref[0, r0:r0 + c, w + p * lanes:w + (p + 1) * lanes].astype(F32)
            v = p_ref[0, r0:r0 + c, 2 * w + p * lanes:2 * w + (p + 1) * lanes]
            if ch >= n_ctx:
                t0 = r0 - ctx_len
                cos = cos_ref[t0:t0 + c, :]
                sin = sin_ref[t0:t0 + c, :]
                q = _rope(q, cos, sin, lane_even)
                k = _rope(k, cos, sin, lane_even)
            q_ref[r0:r0 + c, lsl] = (q * scale).astype(BF16)
            k_ref[r0:r0 + c, lsl] = k.astype(BF16)
            d_f = pl.dot((k * kd_f).astype(BF16), v, trans_a=True)
            d_b = pl.dot((k * kd_b).astype(BF16), v, trans_a=True)
            dlt_ref[0, ch] = jnp.where(blk_diag, d_f, 0.0)
            dlt_ref[1, ch] = jnp.where(blk_diag, d_b, 0.0)

        gc_f = jnp.exp(jnp.where(blk_row_low, lg_ref[layer, 0, 2 * p], lg_ref[layer, 0, 2 * p + 1]) * float(c))
        gc_b = jnp.exp(jnp.where(blk_row_low, lg_ref[layer, 1, 2 * p], lg_ref[layer, 1, 2 * p + 1]) * float(c))
        state = jnp.zeros((lanes, lanes), F32)
        for ch in range(n_chunks):
            st_ref[0, ch] = state.astype(BF16)
            state = gc_f * state + dlt_ref[0, ch]
        state = jnp.zeros((lanes, lanes), F32)
        for ch in order_b:
            st_ref[1, ch] = state.astype(BF16)
            state = gc_b * state + dlt_ref[1, ch]

        dmats = []
        for h in range(2):
            lgf_s = lg_ref[layer, 0, 2 * p + h]
            lgb_s = lg_ref[layer, 1, 2 * p + h]
            dmats.append(jnp.where(dif > 0, jnp.exp(lgf_s * jnp.maximum(dif, 0.0)),
                                   jnp.where(dif < 0, jnp.exp(lgb_s * jnp.maximum(-dif, 0.0)), 2.0)))

        def out_chunk(ch, _):
            r0 = pl.multiple_of(ch * c, c)
            rows = pl.ds(r0, c)
            q = q_ref[rows, lsl]
            k = k_ref[rows, lsl]
            v = p_ref[0, rows, 2 * w + p * lanes:2 * w + (p + 1) * lanes]
            g = p_ref[0, rows, 3 * w + p * lanes:3 * w + (p + 1) * lanes].astype(F32)
            zero = jnp.zeros_like(q)
            s0 = pl.dot(jnp.where(low_half, q, zero), k, trans_b=True)
            s1 = pl.dot(jnp.where(low_half, zero, q), k, trans_b=True)
            o0 = jnp.dot((s0 * dmats[0]).astype(BF16), v, preferred_element_type=F32)
            o1 = jnp.dot((s1 * dmats[1]).astype(BF16), v, preferred_element_type=F32)
            o = jnp.where(low_half, o0, o1)
            qf = q.astype(F32)
            o = o + jnp.dot((qf * qd_f).astype(BF16), st_ref[0, ch], preferred_element_type=F32)
            o = o + jnp.dot((qf * qd_b).astype(BF16), st_ref[1, ch], preferred_element_type=F32)
            oo = o * o
            ss0 = jnp.sum(jnp.where(low_half, oo, 0.0), axis=-1, keepdims=True)
            ss1 = jnp.sum(jnp.where(low_half, 0.0, oo), axis=-1, keepdims=True)
            inv = jnp.where(low_half, lax.rsqrt(ss0 / head_dim + EPS), lax.rsqrt(ss1 / head_dim + EPS))
            o_ref[0, rows, lsl] = (_silu(g) * (o * inv)).astype(o_ref.dtype)
            return 0

        lax.fori_loop(0, n_chunks, out_chunk, 0, unroll=True)


def _retention(p_ret, lg, lg_lane, cos, sin_signed, layer, ctx_len, head_dim):
    b, s, w4 = p_ret.shape
    w = w4 // 4
    n_chunks = s // RET_CHUNK
    return pl.pallas_call(
        functools.partial(_ret_kernel, layer=layer, ctx_len=ctx_len, seq=s, head_dim=head_dim),
        grid=(b,),
        in_specs=[
            pl.BlockSpec(memory_space=pltpu.SMEM),
            pl.BlockSpec((1, s, w4), lambda i: (i, 0, 0)),
            _const_spec(cos.shape), _const_spec(sin_signed.shape), _layer_spec(lg_lane, layer),
        ],
        out_specs=pl.BlockSpec((1, s, w), lambda i: (i, 0, 0)),
        out_shape=jax.ShapeDtypeStruct((b, s, w), BF16),
        scratch_shapes=[
            pltpu.VMEM((s, w), BF16),
            pltpu.VMEM((s, w), BF16),
            pltpu.VMEM((2, n_chunks, V7X_LANES, V7X_LANES), F32),
            pltpu.VMEM((2, n_chunks, V7X_LANES, V7X_LANES), BF16),
        ],
        compiler_params=_cparams("parallel"),
        name="retention",
    )(lg, p_ret, cos, sin_signed, lg_lane)


def _cm_kernel(hin_ref, wz_ref, lng_ref, lnb_ref, ws_ref, bs_ref, o_ref, z_ref, *, seq, group_dim):
    c = CM_CHUNK
    lanes = V7X_LANES
    w = o_ref.shape[2]
    n_slabs = w // lanes
    low_half = lax.broadcasted_iota(jnp.int32, (c, lanes), 1) < group_dim
    lane_w = lax.broadcasted_iota(jnp.int32, (c, w), 1)
    proj_rows = ROW_BLOCK if seq % ROW_BLOCK == 0 else c
    chunks_per_proj = proj_rows // c

    def project(blk):
        rows = slice(blk * proj_rows, (blk + 1) * proj_rows)
        z_ref[rows, :] = jnp.dot(hin_ref[0, rows, :], wz_ref[...], preferred_element_type=F32)

    def body(ch):
        rows = slice(ch * c, (ch + 1) * c)
        z = _gelu_tanh(z_ref[rows, :])
        u = z[:, :w]
        vn = []
        for sl in range(n_slabs):
            v = z[:, w + sl * lanes:w + (sl + 1) * lanes]
            m0 = jnp.sum(jnp.where(low_half, v, 0.0), axis=-1, keepdims=True) / group_dim
            m1 = jnp.sum(jnp.where(low_half, 0.0, v), axis=-1, keepdims=True) / group_dim
            xc = v - jnp.where(low_half, m0, m1)
            xx = xc * xc
            v0 = jnp.sum(jnp.where(low_half, xx, 0.0), axis=-1, keepdims=True) / group_dim
            v1 = jnp.sum(jnp.where(low_half, 0.0, xx), axis=-1, keepdims=True) / group_dim
            vn.append(xc * jnp.where(low_half, lax.rsqrt(v0 + EPS), lax.rsqrt(v1 + EPS)))
        vn = jnp.concatenate(vn, axis=-1) * lng_ref[...] + lnb_ref[...]
        s_all = jnp.dot(ws_ref[...], vn.astype(BF16), preferred_element_type=F32)
        sp = s_all[0:c]
        for gi in range(1, CM_GROUPS):
            sp = jnp.where(lane_w >= gi * group_dim, s_all[gi * c:(gi + 1) * c], sp)
        o_ref[0, rows, :] = (u * (sp + bs_ref[...])).astype(o_ref.dtype)

    project(0)
    for ch in range(seq // c):
        if ch % chunks_per_proj == 0 and (ch // chunks_per_proj + 1) * proj_rows < seq:
            project(ch // chunks_per_proj + 1)
        body(ch)


def _chunk_mlp(h3, w_cm_b, ln_g, ln_b, ws, bs, layer, group_dim):
    b, s, d = h3.shape
    w = w_cm_b.shape[2] // 2
    return pl.pallas_call(
        functools.partial(_cm_kernel, seq=s, group_dim=group_dim),
        grid=(b,),
        in_specs=[pl.BlockSpec((1, s, d), lambda i: (i, 0, 0))]
        + [_layer_spec(a, layer) for a in (w_cm_b, ln_g, ln_b, ws, bs)],
        out_specs=pl.BlockSpec((1, s, w), lambda i: (i, 0, 0)),
        out_shape=jax.ShapeDtypeStruct((b, s, w), BF16),
        scratch_shapes=[pltpu.VMEM((s, 2 * w), F32)],
        compiler_params=_cparams("parallel"),
        name="chunk_gmlp",
    )(h3, w_cm_b, ln_g, ln_b, ws, bs)


def _swiglu_rows(h, w1_ref, w3_ref, w2_ref):
    f = w1_ref.shape[1]
    acc = None
    for c0 in range(0, f, FF_CHUNK):
        a = jnp.dot(h, w1_ref[:, c0:c0 + FF_CHUNK], preferred_element_type=F32)
        g = jnp.dot(h, w3_ref[:, c0:c0 + FF_CHUNK], preferred_element_type=F32)
        act = (_silu(a) * g).astype(BF16)
        part = jnp.dot(act, w2_ref[c0:c0 + FF_CHUNK, :], preferred_element_type=F32)
        acc = part if acc is None else acc + part
    return acc


def _ffn_kernel(*refs, n_src, ctx_rows):
    src = refs[:n_src]
    (gmix_ref, gpre_ref, gpost_ref, ml_ref, mc_ref, pa_ref, pb_ref, pc_ref, wo_ref,
     w1_ref, w3_ref, w2_ref, o_ref, xm_ref, h_ref) = refs[n_src:n_src + 15]
    x_ref = _residual_rows(src, refs[-1], ctx_rows)
    y = _mix_out((pa_ref, pb_ref, pc_ref), wo_ref, h_ref)
    _gated_residual(x_ref[...], y, gmix_ref, ml_ref, mc_ref, xm_ref, ctx_rows, 2)
    _prenorm_modulate(xm_ref[...], gpre_ref, ml_ref, mc_ref, h_ref, ctx_rows, 3)
    y = _swiglu_rows(h_ref[...], w1_ref, w3_ref, w2_ref)
    _gated_residual(xm_ref[...], y, gpost_ref, ml_ref, mc_ref, o_ref, ctx_rows, 5)


def _mix_dense_ffn(src, rows, gains, mod_all, parts, w_out_b, w1, w3, w2, layer, j):
    d = src[0].shape[1]
    ml_spec, mc_spec = rows.mod_specs(mod_all, layer)
    src_specs, src_scratch = _source_specs(rows, src, d)
    return pl.pallas_call(
        functools.partial(_ffn_kernel, n_src=len(src), ctx_rows=rows.ctx_rows),
        grid=rows.grid,
        in_specs=src_specs + [_layer_spec(g, layer) for g in gains[1:]] + [ml_spec, mc_spec]
        + [rows.slab(a.shape[1]) for a in parts] + [_layer_spec(w_out_b, layer)]
        + [_layer_spec(w, j) for w in (w1, w3, w2)],
        out_specs=rows.slab(d),
        out_shape=jax.ShapeDtypeStruct((rows.batch * rows.seq, d), F32),
        scratch_shapes=[pltpu.VMEM((rows.rb, d), F32), pltpu.VMEM((rows.rb, d), BF16)] + src_scratch,
        input_output_aliases={0: 0} if len(src) == 1 else {},
        compiler_params=_cparams("parallel", "parallel"),
        name="mix_dense_swiglu",
    )(*src, *gains[1:], mod_all, mod_all, *parts, w_out_b, w1, w3, w2)


def _router_kernel(x_ref, gmix_ref, g_ref, ml_ref, mc_ref, pa_ref, pb_ref, pc_ref, wo_ref, wr_ref,
                   xo_ref, h_ref, idx_ref, wgt_ref, cnt_ref, hf_ref, tri_ref, run_ref,
                   *, ctx_rows, n_experts, group_batches):
    first_step = (pl.program_id(0) == 0) & (pl.program_id(1) == 0)

    @pl.when(first_step)
    def _():
        r = lax.broadcasted_iota(jnp.int32, tri_ref.shape, 0)
        c = lax.broadcasted_iota(jnp.int32, tri_ref.shape, 1)
        tri_ref[...] = jnp.where(c < r, 1.0, 0.0).astype(tri_ref.dtype)

    @pl.when((pl.program_id(0) % group_batches == 0) & (pl.program_id(1) == 0))
    def _():
        run_ref[...] = jnp.zeros_like(run_ref)

    y = _mix_out((pa_ref, pb_ref, pc_ref), wo_ref, h_ref)
    _gated_residual(x_ref[...], y, gmix_ref, ml_ref, mc_ref, xo_ref, ctx_rows, 2)
    _prenorm_modulate(xo_ref[...], g_ref, ml_ref, mc_ref, hf_ref, ctx_rows, 3)
    h = hf_ref[...]
    h_ref[...] = h.astype(h_ref.dtype)
    logits = jnp.dot(h, wr_ref[...], preferred_element_type=F32)
    lane = lax.broadcasted_iota(jnp.int32, logits.shape, 1).astype(F32)
    neg_inf = jnp.float32(-jnp.inf)
    logits = jnp.where(lane < n_experts, logits, neg_inf)
    big = jnp.float32(V7X_LANES)
    m1 = jnp.max(logits, axis=-1, keepdims=True)
    i1 = jnp.min(jnp.where(logits == m1, lane, big), axis=-1, keepdims=True)
    rest = jnp.where(lane == i1, neg_inf, logits)
    m2 = jnp.max(rest, axis=-1, keepdims=True)
    i2 = jnp.min(jnp.where(rest == m2, lane, big), axis=-1, keepdims=True)
    e2 = jnp.exp(m2 - m1)
    w1 = 1.0 / (1.0 + e2)
    w2 = e2 / (1.0 + e2)
    oh1 = jnp.where(lane == i1, 1.0, 0.0)
    oh2 = jnp.where(lane == i2, 1.0, 0.0)
    both = oh1 + oh2
    before = jnp.dot(tri_ref[...], both.astype(BF16), preferred_element_type=F32) + run_ref[0:1, :]
    r1 = jnp.sum(before * oh1, axis=-1, keepdims=True)
    r2 = jnp.sum(before * oh2, axis=-1, keepdims=True)
    run_ref[...] = run_ref[...] + jnp.sum(both, axis=0, keepdims=True)
    cnt_ref[0] = run_ref[...]
    packed = jnp.where(lane == 0, i1, jnp.where(lane == 1, i2, jnp.where(lane == 2, r1,
                       jnp.where(lane == 3, r2, 0.0))))
    idx_ref[...] = packed.T[:idx_ref.shape[0], :].astype(jnp.int32)
    out_lane = lax.broadcasted_iota(jnp.int32, wgt_ref.shape, 1)
    wgt_ref[...] = jnp.where(out_lane == 0, w1, w2)


def _mix_router(xs2, rows, gains, mod_all, parts, w_out_b, wr_pad, layer, j, n_experts):
    d = xs2.shape[1]
    n_walk = rows.batch * rows.rows
    ml_spec, mc_spec = rows.mod_specs(mod_all, layer)
    if rows.latent_only:
        x_spec, x_shape, aliases = rows.walked(d), (n_walk, d), {}
    else:
        x_spec, x_shape, aliases = rows.slab(d), xs2.shape, {0: 0}
    return pl.pallas_call(
        functools.partial(_router_kernel, ctx_rows=rows.ctx_rows, n_experts=n_experts,
                          group_batches=rows.batch // MOE_SPLITS),
        grid=rows.grid,
        in_specs=[rows.slab(d), _layer_spec(gains[1], layer), _layer_spec(gains[2], layer), ml_spec, mc_spec]
        + [rows.slab(a.shape[1]) for a in parts]
        + [_layer_spec(w_out_b, layer), _layer_spec(wr_pad, j)],
        out_specs=[x_spec, rows.walked(d),
                   pl.BlockSpec((V7X_SUBLANES, rows.rb), lambda b, j_: (0, (b + rows.batch0) * rows.n_blk + j_)),
                   rows.walked(V7X_SUBLANES),
                   pl.BlockSpec((1, V7X_SUBLANES, V7X_LANES), lambda b, j_: (b, 0, 0))],
        out_shape=[jax.ShapeDtypeStruct(x_shape, F32),
                   jax.ShapeDtypeStruct((n_walk, d), BF16),
                   jax.ShapeDtypeStruct((V7X_SUBLANES, n_walk), jnp.int32),
                   jax.ShapeDtypeStruct((n_walk, V7X_SUBLANES), F32),
                   jax.ShapeDtypeStruct((rows.batch, V7X_SUBLANES, V7X_LANES), F32)],
        scratch_shapes=[pltpu.VMEM((rows.rb, d), F32), pltpu.VMEM((rows.rb, rows.rb), BF16),
                        pltpu.VMEM((V7X_SUBLANES, V7X_LANES), F32)],
        input_output_aliases=aliases,
        compiler_params=_cparams("arbitrary", "arbitrary"),
        name="mix_moe_router",
    )(xs2, gains[1], gains[2], mod_all, mod_all, *parts, w_out_b, wr_pad)


def _grouped_kernel(te_ref, tv_ref, h_ref, w1_ref, w3_ref, w2_ref, o_ref):
    i = pl.program_id(0)

    @pl.when(tv_ref[i] != 0)
    def _():
        o_ref[...] = _swiglu_rows(h_ref[...], w1_ref, w3_ref, w2_ref).astype(o_ref.dtype)

    @pl.when(tv_ref[i] == 0)
    def _():
        o_ref[...] = jnp.zeros_like(o_ref)


def _grouped_swiglu(layer_idx, tile_expert, tile_valid, hs, w1, w3, w2):
    p, d = hs.shape
    f = w1.shape[3]
    tm = MOE_TILE
    grid_spec = pltpu.PrefetchScalarGridSpec(
        num_scalar_prefetch=2,
        grid=(p // tm,),
        in_specs=[
            pl.BlockSpec((tm, d), lambda i, te, tv: (i, 0)),
            pl.BlockSpec((None, None, d, f), lambda i, te, tv: (layer_idx, te[i], 0, 0)),
            pl.BlockSpec((None, None, d, f), lambda i, te, tv: (layer_idx, te[i], 0, 0)),
            pl.BlockSpec((None, None, f, d), lambda i, te, tv: (layer_idx, te[i], 0, 0)),
        ],
        out_specs=pl.BlockSpec((tm, d), lambda i, te, tv: (i, 0)),
    )
    return pl.pallas_call(
        _grouped_kernel,
        grid_spec=grid_spec,
        out_shape=jax.ShapeDtypeStruct((p, d), BF16),
        compiler_params=_cparams("arbitrary"),
        name="moe_grouped_swiglu",
    )(tile_expert, tile_valid, hs, w1, w3, w2)


def _combine_kernel(*refs, ctx_rows, n_in, with_proj):
    x_ref, g_ref, ml_ref, mc_ref, y0_ref, y1_ref, wgt_ref = refs[:7]
    o_ref = refs[n_in]
    wgt = wgt_ref[...]
    y = wgt[:, 0:1] * y0_ref[...].astype(F32) + wgt[:, 1:2] * y1_ref[...].astype(F32)
    _gated_residual(x_ref[...], y, g_ref, ml_ref, mc_ref, o_ref, ctx_rows, 5)
    if with_proj:
        gn_ref, mln_ref, mcn_ref, w_ref = refs[7:11]
        h_ref, pr_ref = refs[n_in + 1:n_in + 3]
        _prenorm_modulate(o_ref[...], gn_ref, mln_ref, mcn_ref, h_ref, ctx_rows, 0)
        pr_ref[...] = jnp.dot(h_ref[...], w_ref[...], preferred_element_type=F32).astype(pr_ref.dtype)


def _combine(xres, g_all, mod_all, y0, y1, wgt, layer, rows, dest, proj=None, proj_dest=None):
    d = xres.shape[1]
    ml_spec, mc_spec = rows.mod_specs(mod_all, layer)
    x_spec = rows.slab(d) if dest is None else rows.walked(d)
    in_specs = [x_spec, _layer_spec(g_all, layer), ml_spec, mc_spec,
                rows.local(d), rows.local(d), rows.walked(V7X_SUBLANES)]
    args = [xres, g_all, mod_all, mod_all, y0, y1, wgt]
    if proj is not None:
        gn_all, w_ret_b = proj
        widths = (d, w_ret_b.shape[2])
        mln_spec, mcn_spec = rows.mod_specs(mod_all, layer + 1)
        in_specs += [_layer_spec(gn_all, layer + 1), mln_spec, mcn_spec, _layer_spec(w_ret_b, layer + 1)]
        args += [gn_all, mod_all, mod_all, w_ret_b]
    if dest is None:
        out_specs, out_shapes, aliases = [rows.slab(d)], [jax.ShapeDtypeStruct(xres.shape, F32)], {0: 0}
    else:
        out_specs, out_shapes, aliases = [rows.walked(d)], [jax.ShapeDtypeStruct(dest.shape, F32)], {}
        if not isinstance(dest, jax.ShapeDtypeStruct):
            in_specs.append(pl.BlockSpec(memory_space=pl.ANY))
            args.append(dest)
            aliases = {len(args) - 1: 0}
    if proj is not None:
        n_total = rows.batch * rows.seq
        out_specs += [rows.slab(n) for n in widths]
        out_shapes += [jax.ShapeDtypeStruct((n_total, n), BF16) for n in widths]
        for k, arr in enumerate(proj_dest or ()):
            in_specs.append(pl.BlockSpec(memory_space=pl.ANY))
            args.append(arr)
            aliases[len(args) - 1] = 1 + k
    res = pl.pallas_call(
        functools.partial(_combine_kernel, ctx_rows=rows.ctx_rows, n_in=len(args), with_proj=proj is not None),
        grid=rows.grid,
        in_specs=in_specs,
        out_specs=out_specs,
        out_shape=out_shapes,
        input_output_aliases=aliases,
        compiler_params=_cparams("parallel", "parallel"),
        name="moe_combine",
    )(*args)
    return res if proj is not None else res[0]


def _gather_rows(src, idx):
    return src.at[idx].get(mode="promise_in_bounds")


def _routing_tables(experts, ranks, counts, tok0, n_exp):
    n_tok = experts.shape[1]
    tm = MOE_TILE
    padded = ((counts + tm - 1) // tm) * tm
    ends = jnp.cumsum(padded)
    starts = ends - padded
    start_of = jnp.zeros_like(experts)
    for e in range(n_exp):
        start_of = jnp.where(experts == e, starts[e], start_of)
    pos = start_of + ranks
    n_rows = TOP_K * n_tok + n_exp * tm
    n_tiles = n_rows // tm
    tok = jnp.broadcast_to(tok0 + jnp.arange(n_tok, dtype=jnp.int32)[None, :], pos.shape)
    src_tok = jnp.zeros((n_rows,), jnp.int32).at[pos.reshape(-1)].set(
        tok.reshape(-1), unique_indices=True, mode="promise_in_bounds")
    tile_start = jnp.arange(n_tiles, dtype=jnp.int32) * tm
    tile_expert = jnp.minimum(jnp.sum((tile_start[:, None] >= ends[None, :]).astype(jnp.int32), axis=1),
                              n_exp - 1)
    tile_valid = (tile_start < ends[-1]).astype(jnp.int32)
    return pos, src_tok, tile_expert, tile_valid


def _mix_moe_ffn(xs2, shape, gains, mod_all, parts, w_out_b, wr_pad, w1, w3, w2, layer, j, n_exp, last,
                 proj=None):
    b, s, ctx_len = shape
    d = xs2.shape[1]
    rows = _Rows(b, s, ctx_len, latent_only=last)
    xres, h, idx, wgt, cnt = _mix_router(xs2, rows, gains, mod_all, parts, w_out_b, wr_pad, layer, j, n_exp)
    r = rows.rows
    bh = b // MOE_SPLITS
    out = jax.ShapeDtypeStruct((b * r, d), F32) if last else None
    nxt = None
    for g in range(MOE_SPLITS):
        idx_g = idx[:, g * bh * r:(g + 1) * bh * r]
        counts = cnt[(g + 1) * bh - 1, 0, :n_exp].astype(jnp.int32)
        pos, src_tok, tile_expert, tile_valid = _routing_tables(
            idx_g[:TOP_K], idx_g[TOP_K:2 * TOP_K], counts, g * bh * r, n_exp)
        hs = _gather_rows(h, src_tok)
        ys = _grouped_swiglu(j, tile_expert, tile_valid, hs, w1, w3, w2)
        y0 = _gather_rows(ys, pos[0])
        y1 = _gather_rows(ys, pos[1])
        rows_g = _Rows(b, s, ctx_len, latent_only=last, batch0=g * bh, n_batch=bh)
        res = _combine(xres, gains[3], mod_all, y0, y1, wgt, layer, rows_g, out, proj, nxt)
        if proj is not None:
            res, nxt = res[0], tuple(res[1:])
        if last:
            out = res
        else:
            xres = res
    return (out if last else xres), nxt


def _block_diag(wh):
    h, d = wh.shape[-3], wh.shape[-1]
    eye = jnp.eye(h, dtype=wh.dtype)
    out = jnp.einsum("...hij,hg->...higj", wh, eye)
    return out.reshape(wh.shape[:-3] + (h * d, h * d))


def kernel(x, c, ctx, c_ctx, w_mod, b_mod, g_mix_pre, g_mix_post, g_ffn_pre, g_ffn_post, w_in, w_out,
           lru_conv_w, lru_conv_b, lru_wa, lru_ba, lru_wx, lru_bx, lru_lam, ret_theta,
           cm_ln_g, cm_ln_b, cm_ws, cm_bs, ffn_w1, ffn_w3, ffn_w2, router_w, moe_w1, moe_w3, moe_w2):
    b, l, d = x.shape
    lc = ctx.shape[1]
    s = lc + l
    depth = w_mod.shape[0]
    lru_w = lru_conv_w.shape[2]
    cm_w = cm_ln_g.shape[1] * cm_ln_g.shape[2]
    ret_w = (w_in.shape[2] - 2 * lru_w - 2 * cm_w) // 4
    head_dim = ret_w // RET_HEADS
    group_dim = cm_ln_g.shape[2]
    n_exp = router_w.shape[2]
    assert s % ROW_BLOCK == 0 and lc < ROW_BLOCK and lc % RET_CHUNK == 0 and l % RET_CHUNK == 0
    assert ret_w % V7X_LANES == 0 and cm_w % V7X_LANES == 0 and 2 * head_dim == V7X_LANES
    assert b % MOE_SPLITS == 0 and l % lc == 0 and l >= ROW_BLOCK

    n_rows = l // GRID_W
    rows = jnp.repeat(jnp.arange(n_rows, dtype=F32), GRID_W)
    cols = jnp.tile(jnp.arange(GRID_W, dtype=F32), n_rows)
    pairs = head_dim // 4
    freqs = ROPE_BASE ** (-jnp.arange(pairs, dtype=F32) / pairs)
    ang = jnp.concatenate([rows[:, None] * freqs, cols[:, None] * freqs], axis=-1)
    cos = jnp.tile(jnp.repeat(jnp.cos(ang), 2, axis=1), (1, V7X_LANES // head_dim))
    sin = jnp.repeat(jnp.sin(ang), 2, axis=1) * jnp.tile(jnp.array([-1.0, 1.0], F32), head_dim // 2)
    sin_signed = jnp.tile(sin, (1, V7X_LANES // head_dim))

    w_out_b = w_out.astype(BF16)
    c_ret, c_cm = 2 * lru_w, 2 * lru_w + 4 * ret_w
    w_lru_b, w_ret_b, w_cm_b = (w_in[:, :, :c_ret].astype(BF16), w_in[:, :, c_ret:c_cm].astype(BF16),
                                w_in[:, :, c_cm:].astype(BF16))
    ffn_b1, ffn_b3, ffn_b2 = ffn_w1.astype(BF16), ffn_w3.astype(BF16), ffn_w2.astype(BF16)
    moe_b1, moe_b3, moe_b2 = moe_w1.astype(BF16), moe_w3.astype(BF16), moe_w2.astype(BF16)
    n_slab = lru_w // V7X_LANES
    hd = lru_wa.shape[-1]
    per_slab = V7X_LANES // hd

    def slab_blocks(wh):
        return _block_diag(wh.reshape(depth, n_slab, per_slab, hd, hd))

    wg = (0.5 * jnp.concatenate([slab_blocks(lru_wa[:, 0]), slab_blocks(lru_wx[:, 0]),
                                 slab_blocks(lru_wa[:, 1]), slab_blocks(lru_wx[:, 1])], axis=3)).astype(BF16)
    bg = 0.5 * jnp.concatenate([t.reshape(depth, n_slab, 1, V7X_LANES)
                                for t in (lru_ba[:, 0], lru_bx[:, 0], lru_ba[:, 1], lru_bx[:, 1])], axis=3)
    conv_b = lru_conv_b.reshape(depth, 1, lru_w)
    lam = lru_lam.reshape(depth, 2, lru_w)
    lg = jax.nn.log_sigmoid(ret_theta.astype(F32))
    lg_lane = jnp.repeat(lg, head_dim, axis=2)
    ln_g = cm_ln_g.reshape(depth, 1, cm_w)
    ln_b = cm_ln_b.reshape(depth, 1, cm_w)
    ws = cm_ws.reshape(depth, CM_GROUPS * CM_CHUNK, CM_CHUNK).astype(BF16)
    bs = jnp.repeat(jnp.swapaxes(cm_bs, 1, 2), group_dim, axis=2)
    wr_pad = jnp.zeros((router_w.shape[0], d, V7X_LANES), F32).at[:, :, :n_exp].set(router_w)
    gains = [g.reshape(depth, 1, d) for g in (g_mix_pre, g_mix_post, g_ffn_pre, g_ffn_post)]

    n_cond = ((b + 1 + V7X_SUBLANES - 1) // V7X_SUBLANES) * V7X_SUBLANES
    cpad = jnp.zeros((n_cond, d), F32).at[:b].set(c).at[b].set(c_ctx)
    mod_all = _modulation(cpad, w_mod, b_mod).reshape(depth, n_cond, N_MOD, d)

    rows = _Rows(b, s, lc)
    src = (ctx.reshape(b * lc, d), x.reshape(b * l, d))
    nxt = None
    for layer in range(depth):
        h2, p_ret = nxt or _inproj(src, rows, gains[0], mod_all, w_ret_b, layer)
        nxt = None
        h3 = h2.reshape(b, s, d)
        o_lru = _lru(h3, w_lru_b, lru_conv_w, conv_b, wg, bg, lam, layer, lc)
        o_ret = _retention(p_ret.reshape(b, s, -1), lg, lg_lane, cos, sin_signed, layer, lc, head_dim)
        o_cm = _chunk_mlp(h3, w_cm_b, ln_g, ln_b, ws, bs, layer, group_dim)
        parts = tuple(o.reshape(b * s, -1) for o in (o_lru, o_ret, o_cm))

        j = layer // 2
        last = layer == depth - 1
        if layer % 2 == 0:
            xs2 = _mix_dense_ffn(src, rows, gains, mod_all, parts, w_out_b, ffn_b1, ffn_b3, ffn_b2, layer, j)
        else:
            xs2, nxt = _mix_moe_ffn(src[0], (b, s, lc), gains, mod_all, parts, w_out_b, wr_pad,
                                    moe_b1, moe_b3, moe_b2, layer, j, n_exp, last,
                                    proj=None if last else (gains[0], w_ret_b))
            if last:
                return xs2.reshape(b, l, d)
        src = (xs2,)
    return xs2.reshape(b, s, d)[:, lc:, :]
```

```python
import functools
import math

import jax
import jax.numpy as jnp
from jax import lax
from jax.experimental import pallas as pl
from jax.experimental.pallas import tpu as pltpu

F32 = jnp.float32
BF16 = jnp.bfloat16

EPS = 1e-6
N_MOD = 6
LRU_CONV = 4
LRU_C = 8.0
RET_HEADS = 6
ROPE_BASE = 100.0
GRID_W = 64
CM_GROUPS = 4
CM_CHUNK = 128
TOP_K = 2

V7X_LANES = 128
V7X_SUBLANES = 8
V7X_VMEM_LIMIT = 56 * 1024 * 1024

ROW_BLOCK = 768
RET_CHUNK = 256
FF_CHUNK = 256
MOE_TILE = 512
LAT_BLOCK = 1024
MOE_SPLITS = 2
LOG2_E = 1.4426950408889634
TINY = 1e-30


def _cparams(*sem):
    return pltpu.CompilerParams(dimension_semantics=sem, vmem_limit_bytes=V7X_VMEM_LIMIT)


def _layer_spec(arr, layer):
    nd = arr.ndim - 1
    return pl.BlockSpec((None,) + arr.shape[1:], lambda *_: (layer,) + (0,) * nd,
                        pipeline_mode=pl.Buffered(1))


def _const_spec(shape):
    nd = len(shape)
    return pl.BlockSpec(shape, lambda *_: (0,) * nd, pipeline_mode=pl.Buffered(1))


def _rms(x):
    return x * lax.rsqrt(jnp.mean(x * x, axis=-1, keepdims=True) + EPS)


def _gelu_tanh(x):
    return 0.5 * x * (1.0 + jnp.tanh(0.7978845608028654 * (x + 0.044715 * (x * x * x))))


def _sigmoid(x):
    return 1.0 / (1.0 + jnp.exp(-x))


def _silu(x):
    return x * _sigmoid(x)


def _mod_kernel(c_ref, w_ref, b_ref, o_ref):
    s = _silu(c_ref[...])
    o_ref[...] = jnp.dot(s, w_ref[...], preferred_element_type=F32) + b_ref[...]


def _modulation(cpad, w_mod, b_mod):
    depth, d, nd = w_mod.shape
    r = cpad.shape[0]
    return pl.pallas_call(
        _mod_kernel,
        grid=(depth, nd // d),
        in_specs=[
            pl.BlockSpec((r, d), lambda l, n: (0, 0)),
            pl.BlockSpec((None, d, d), lambda l, n: (l, 0, n)),
            pl.BlockSpec((None, 1, d), lambda l, n: (l, 0, n)),
        ],
        out_specs=pl.BlockSpec((None, r, d), lambda l, n: (l, 0, n)),
        out_shape=jax.ShapeDtypeStruct((depth, r, nd), F32),
        compiler_params=_cparams("parallel", "parallel"),
        name="modulation",
    )(cpad, w_mod, b_mod.reshape(depth, 1, nd))


class _Rows:
    def __init__(self, batch, seq, ctx_len, latent_only=False, batch0=0, n_batch=None):
        self.batch, self.seq, self.ctx_len = batch, seq, ctx_len
        self.lat_len = seq - ctx_len
        self.batch0 = batch0
        self.n_batch = batch if n_batch is None else n_batch
        self.latent_only = latent_only
        if latent_only:
            self.rb = LAT_BLOCK if self.lat_len % LAT_BLOCK == 0 else ctx_len
            self.n_blk = self.lat_len // self.rb
            self.ctx_rows = 0
        else:
            self.rb = ROW_BLOCK
            self.n_blk = seq // ROW_BLOCK
            self.ctx_rows = ctx_len
        self.rows = self.rb * self.n_blk
        self.align = math.gcd(self.rb, ctx_len)

    @property
    def grid(self):
        return (self.n_batch, self.n_blk)

    def slab(self, width):
        if not self.latent_only:
            return pl.BlockSpec((self.rb, width), lambda b, j: ((b + self.batch0) * self.n_blk + j, 0))

        def index(b, j):
            row = (b + self.batch0) * self.seq + self.ctx_len + j * self.rb
            return (pl.multiple_of(row, self.align), 0)
        return pl.BlockSpec((pl.Element(self.rb), pl.Element(width)), index)

    def walked(self, width):
        return pl.BlockSpec((self.rb, width), lambda b, j: ((b + self.batch0) * self.n_blk + j, 0))

    def local(self, width):
        return pl.BlockSpec((self.rb, width), lambda b, j: (b * self.n_blk + j, 0))

    def split_source(self, width):
        assert not self.latent_only and self.lat_len >= self.rb
        ctx = pl.BlockSpec((self.ctx_len, width), lambda b, j: (b + self.batch0, 0))

        def index(b, j):
            row = (b + self.batch0) * self.lat_len + jnp.maximum(j * self.rb - self.ctx_len, 0)
            return (pl.multiple_of(row, self.align), 0)
        return ctx, pl.BlockSpec((pl.Element(self.rb), pl.Element(width)), index)

    def mod_specs(self, mod_all, layer):
        d = mod_all.shape[-1]
        lat = pl.BlockSpec((None, 1, N_MOD, d), lambda b, j: (layer, b + self.batch0, 0, 0))
        ctx = pl.BlockSpec((None, 1, N_MOD, d), lambda b, j: (layer, self.batch, 0, 0))
        return lat, ctx


def _row_mod(ml_ref, mc_ref, idx, first_block):
    lat = ml_ref[0, idx:idx + 1, :]
    top = jnp.where(first_block, mc_ref[0, idx:idx + 1, :], lat)
    return top, lat


def _prenorm_modulate(x, g_ref, ml_ref, mc_ref, h_ref, ctx_rows, shift_idx):
    first = pl.program_id(1) == 0
    sh_top, sh_lat = _row_mod(ml_ref, mc_ref, shift_idx, first)
    sc_top, sc_lat = _row_mod(ml_ref, mc_ref, shift_idx + 1, first)
    y = _rms(x) * g_ref[...]
    if ctx_rows:
        h_ref[:ctx_rows, :] = (y[:ctx_rows] * (1.0 + sc_top) + sh_top).astype(h_ref.dtype)
    h_ref[ctx_rows:, :] = (y[ctx_rows:] * (1.0 + sc_lat) + sh_lat).astype(h_ref.dtype)


def _gated_residual(x, y, g_ref, ml_ref, mc_ref, o_ref, ctx_rows, gate_idx):
    first = pl.program_id(1) == 0
    gt_top, gt_lat = _row_mod(ml_ref, mc_ref, gate_idx, first)
    r = _rms(y) * g_ref[...]
    if ctx_rows:
        o_ref[:ctx_rows, :] = x[:ctx_rows] + gt_top * r[:ctx_rows]
    o_ref[ctx_rows:, :] = x[ctx_rows:] + gt_lat * r[ctx_rows:]


def _residual_rows(src_refs, xbuf_ref, ctx_rows):
    if len(src_refs) == 1:
        return src_refs[0]
    ctx_ref, lat_ref = src_refs
    first = pl.program_id(1) == 0

    @pl.when(first)
    def _():
        xbuf_ref[:ctx_rows, :] = ctx_ref[...]
        xbuf_ref[ctx_rows:, :] = lat_ref[:xbuf_ref.shape[0] - ctx_rows, :]

    @pl.when(jnp.logical_not(first))
    def _():
        xbuf_ref[...] = lat_ref[...]

    return xbuf_ref


def _source_specs(rows, src, d):
    if len(src) == 1:
        return [rows.slab(d)], []
    return list(rows.split_source(d)), [pltpu.VMEM((rows.rb, d), F32)]


def _mix_out(part_refs, w_ref, cat_ref):
    c0 = 0
    for part in part_refs:
        n = part.shape[1]
        cat_ref[:, c0:c0 + n] = part[...]
        c0 += n
    return jnp.dot(cat_ref[...], w_ref[...], preferred_element_type=F32)


def _inproj_kernel(*refs, n_src, ctx_rows):
    src, (g_ref, ml_ref, mc_ref, w_ref, h_ref, pr_ref) = refs[:n_src], refs[n_src:n_src + 6]
    x_ref = _residual_rows(src, refs[-1], ctx_rows)
    _prenorm_modulate(x_ref[...], g_ref, ml_ref, mc_ref, h_ref, ctx_rows, 0)
    pr_ref[...] = jnp.dot(h_ref[...], w_ref[...], preferred_element_type=F32).astype(pr_ref.dtype)


def _inproj(src, rows, g_all, mod_all, w_ret_b, layer):
    d = src[0].shape[1]
    ml_spec, mc_spec = rows.mod_specs(mod_all, layer)
    src_specs, src_scratch = _source_specs(rows, src, d)
    n_total = rows.batch * rows.seq
    widths = (d, w_ret_b.shape[2])
    return pl.pallas_call(
        functools.partial(_inproj_kernel, n_src=len(src), ctx_rows=rows.ctx_rows),
        grid=rows.grid,
        in_specs=src_specs + [_layer_spec(g_all, layer), ml_spec, mc_spec, _layer_spec(w_ret_b, layer)],
        out_specs=[rows.slab(n) for n in widths],
        out_shape=[jax.ShapeDtypeStruct((n_total, n), BF16) for n in widths],
        scratch_shapes=src_scratch,
        compiler_params=_cparams("parallel", "parallel"),
        name="mixer_inproj",
    )(*src, g_all, mod_all, mod_all, w_ret_b)


def _tile_scan(a, u, carry, row, reverse):
    n = V7X_SUBLANES
    for dist in (1, 2, 4):
        if reverse:
            keep = row < (n - dist)
            shift = n - dist
        else:
            keep = row >= dist
            shift = dist
        a_s = jnp.where(keep, pltpu.roll(a, shift, 0), 1.0)
        u_s = jnp.where(keep, pltpu.roll(u, shift, 0), 0.0)
        u = a * u_s + u
        a = a * a_s
    h = a * carry + u
    new_carry = h[0:1] if reverse else h[n - 1:n]
    return h, new_carry


def _lru_kernel(hin_ref, wl_ref, cw_ref, cb_ref, wg_ref, bg_ref, lam_ref, o_ref,
                xpad_ref, a_ref, u_ref, h_ref, ly_ref, *, ctx_len, seq):
    w = cw_ref.shape[1]
    s = seq
    pad = V7X_SUBLANES
    chunk = ctx_len
    lanes = V7X_LANES
    n_slabs = w // lanes
    zeros = jnp.zeros((pad, lanes), F32)
    for j in range(n_slabs):
        xpad_ref[j, 0:pad, :] = zeros
        xpad_ref[j, pad + ctx_len:2 * pad + ctx_len, :] = zeros
        xpad_ref[j, 2 * pad + s:3 * pad + s, :] = zeros
    proj_rows = ROW_BLOCK if s % ROW_BLOCK == 0 else chunk
    chunks_per_proj = proj_rows // chunk

    def project(blk):
        p0 = blk * proj_rows
        pr = jnp.dot(hin_ref[0, p0:p0 + proj_rows, :], wl_ref[...], preferred_element_type=F32)
        for q in range(chunks_per_proj):
            r0 = p0 + q * chunk
            base = r0 + (pad if r0 < ctx_len else 2 * pad)
            for j in range(n_slabs):
                xpad_ref[j, base:base + chunk, :] = pr[q * chunk:(q + 1) * chunk, j * lanes:(j + 1) * lanes]
            ly_ref[r0:r0 + chunk, :] = pr[q * chunk:(q + 1) * chunk, w:2 * w]

    neg = -lam_ref[...]
    softplus = jnp.maximum(neg, 0.0) + jnp.log(1.0 + jnp.exp(-jnp.abs(neg)))
    c2 = (-0.5 * LRU_C * LOG2_E) * softplus

    pitch = a_ref.shape[2] // V7X_SUBLANES
    lat_len = s - ctx_len

    def phys(d, r0):
        if d == 0:
            return r0
        return lat_len + r0 if r0 < ctx_len else r0 - ctx_len

    for d in range(2):
        for j in range(n_slabs):
            a_ref[d, j, s:, :] = jnp.ones((a_ref.shape[2] - s, lanes), F32)
            u_ref[d, j, s:, :] = jnp.zeros((a_ref.shape[2] - s, lanes), F32)

    project(0)
    for c in range(s // chunk):
        if c % chunks_per_proj == 0 and (c // chunks_per_proj + 1) * proj_rows < s:
            project(c // chunks_per_proj + 1)
        r0 = c * chunk
        base = r0 + (pad if r0 < ctx_len else 2 * pad)
        xc = []
        for j in range(n_slabs):
            lsl = slice(j * lanes, (j + 1) * lanes)
            acc = jnp.zeros((chunk, lanes), F32) + cb_ref[:, lsl]
            for k in range(LRU_CONV):
                off = k - LRU_CONV // 2
                acc = acc + xpad_ref[j, base + off:base + off + chunk, :] * cw_ref[k:k + 1, lsl]
            xc.append(acc)
        for j in range(n_slabs):
            th = jnp.tanh(jnp.dot(xc[j].astype(BF16), wg_ref[j], preferred_element_type=F32) + bg_ref[j])
            hx = 0.5 * xc[j]
            for d in range(2):
                th_r = th[:, (2 * d) * lanes:(2 * d + 1) * lanes]
                th_i = th[:, (2 * d + 1) * lanes:(2 * d + 2) * lanes]
                cd = c2[d:d + 1, j * lanes:(j + 1) * lanes]
                a = jnp.exp2(cd * th_r + cd)
                y = 1.0 - a * a
                u = (y * lax.rsqrt(jnp.maximum(y, TINY))) * (hx * th_i + hx)
                pr = phys(d, r0)
                a_ref[d, j, pr:pr + chunk, :] = a
                u_ref[d, j, pr:pr + chunk, :] = u

    row = lax.broadcasted_iota(jnp.int32, (V7X_SUBLANES, lanes), 0)
    chains = [(d, j) for d in range(2) for j in range(n_slabs)]

    def seg_rows(d, t):
        tt = t if d == 0 else pitch - 1 - t
        return pl.ds(tt, V7X_SUBLANES, stride=pitch)

    def pass1(t, carry):
        out = []
        for (d, j), (h, pprod) in zip(chains, carry):
            a = a_ref[d, j, seg_rows(d, t), :]
            u = u_ref[d, j, seg_rows(d, t), :]
            out.append((a * h + u, a * pprod))
        return tuple(out)

    zero = jnp.zeros((V7X_SUBLANES, lanes), F32)
    ends = lax.fori_loop(0, pitch, pass1, tuple((zero, zero + 1.0) for _ in chains), unroll=4)

    starts = []
    for (d, j), (h_end, p_tot) in zip(chains, ends):
        incl, _ = _tile_scan(p_tot, h_end, jnp.zeros((1, lanes), F32), row, d == 1)
        if d == 0:
            starts.append(jnp.where(row >= 1, pltpu.roll(incl, 1, 0), 0.0))
        else:
            starts.append(jnp.where(row < V7X_SUBLANES - 1, pltpu.roll(incl, V7X_SUBLANES - 1, 0), 0.0))

    def pass2(t, carry):
        out = []
        for (d, j), h in zip(chains, carry):
            a = a_ref[d, j, seg_rows(d, t), :]
            u = u_ref[d, j, seg_rows(d, t), :]
            h = a * h + u
            h_ref[d, j, seg_rows(d, t), :] = h
            out.append(h)
        return tuple(out)

    lax.fori_loop(0, pitch, pass2, tuple(starts), unroll=4)

    for c in range(s // chunk):
        r0 = c * chunk
        rb = phys(1, r0)
        for j in range(n_slabs):
            ly = ly_ref[r0:r0 + chunk, j * lanes:(j + 1) * lanes]
            h = h_ref[0, j, r0:r0 + chunk, :] + h_ref[1, j, rb:rb + chunk, :]
            o_ref[0, r0:r0 + chunk, j * lanes:(j + 1) * lanes] = (_gelu_tanh(ly) * h).astype(o_ref.dtype)


def _lru(h3, w_lru_b, conv_w, conv_b, wg, bg, lam, layer, ctx_len):
    b, s, d = h3.shape
    w = w_lru_b.shape[2] // 2
    assert s % (V7X_SUBLANES * V7X_SUBLANES) == 0
    scan_rows = V7X_SUBLANES * (s // V7X_SUBLANES + 4)
    scan_buf = pltpu.VMEM((2, w // V7X_LANES, scan_rows, V7X_LANES), F32)
    return pl.pallas_call(
        functools.partial(_lru_kernel, ctx_len=ctx_len, seq=s),
        grid=(b,),
        in_specs=[pl.BlockSpec((1, s, d), lambda i: (i, 0, 0))]
        + [_layer_spec(a, layer) for a in (w_lru_b, conv_w, conv_b, wg, bg, lam)],
        out_specs=pl.BlockSpec((1, s, w), lambda i: (i, 0, 0)),
        out_shape=jax.ShapeDtypeStruct((b, s, w), BF16),
        scratch_shapes=[pltpu.VMEM((w // V7X_LANES, s + 3 * V7X_SUBLANES, V7X_LANES), F32),
                        scan_buf, scan_buf, scan_buf, pltpu.VMEM((s, w), F32)],
        compiler_params=_cparams("parallel"),
        name="rglru",
    )(h3, w_lru_b, conv_w, conv_b, wg, bg, lam)


def _rope(x, cos, sin_signed, lane_even):
    partner = jnp.where(lane_even, pltpu.roll(x, V7X_LANES - 1, 1), pltpu.roll(x, 1, 1))
    return x * cos + partner * sin_signed


def _ret_kernel(lg_ref, p_ref, cos_ref, sin_ref, lgl_ref, o_ref,
                q_ref, k_ref, dlt_ref, st_ref, *, layer, ctx_len, seq, head_dim):
    s = seq
    c = RET_CHUNK
    lanes = V7X_LANES
    w = o_ref.shape[2]
    n_pairs = w // lanes
    n_chunks = s // c
    n_ctx = ctx_len // c
    scale = head_dim ** -0.5

    rowi = lax.broadcasted_iota(jnp.int32, (c, lanes), 0).astype(F32)
    lane = lax.broadcasted_iota(jnp.int32, (c, lanes), 1)
    lane_even = (lane % 2) == 0
    low_half = lane < head_dim
    dif = (lax.broadcasted_iota(jnp.int32, (c, c), 0)
           - lax.broadcasted_iota(jnp.int32, (c, c), 1)).astype(F32)
    blk_row_low = lax.broadcasted_iota(jnp.int32, (lanes, lanes), 0) < head_dim
    blk_col_low = lax.broadcasted_iota(jnp.int32, (lanes, lanes), 1) < head_dim
    blk_diag = blk_row_low == blk_col_low

    order_b = list(range(n_ctx - 1, -1, -1)) + list(range(n_chunks - 1, n_ctx - 1, -1))

    for p in range(n_pairs):
        lsl = slice(p * lanes, (p + 1) * lanes)
        lg_f = lgl_ref[0:1, lsl]
        lg_b = lgl_ref[1:2, lsl]
        kd_f = jnp.exp(lg_f * (c - 1.0 - rowi))
        kd_b = jnp.exp(lg_b * rowi)
        qd_f = jnp.exp(lg_f * (rowi + 1.0))
        qd_b = jnp.exp(lg_b * (c - rowi))

        for ch in range(n_chunks):
            r0 = ch * c
            q = p_ref[0, r0:r0 + c, p * lanes:(p + 1) * lanes].astype(F32)
            k = p_ref[0, r0:r0 + c, w + p * lanes:w + (p + 1) * lanes].astype(F32)
            v = p_ref[0, r0:r0 + c, 2 * w + p * lanes:2 * w + (p + 1) * lanes]
            if ch >= n_ctx:
                t0 = r0 - ctx_len
                cos = cos_ref[t0:t0 + c, :]
                sin = sin_ref[t0:t0 + c, :]
                q = _rope(q, cos, sin, lane_even)
                k = _rope(k, cos, sin, lane_even)
            q_ref[r0:r0 + c, lsl] = (q * scale).astype(BF16)
            k_ref[r0:r0 + c, lsl] = k.astype(BF16)
            d_f = pl.dot((k * kd_f).astype(BF16), v, trans_a=True)
            d_b = pl.dot((k * kd_b).astype(BF16), v, trans_a=True)
            dlt_ref[0, ch] = jnp.where(blk_diag, d_f, 0.0)
            dlt_ref[1, ch] = jnp.where(blk_diag, d_b, 0.0)

        gc_f = jnp.exp(jnp.where(blk_row_low, lg_ref[layer, 0, 2 * p], lg_ref[layer, 0, 2 * p + 1]) * float(c))
        gc_b = jnp.exp(jnp.where(blk_row_low, lg_ref[layer, 1, 2 * p], lg_ref[layer, 1, 2 * p + 1]) * float(c))
        state = jnp.zeros((lanes, lanes), F32)
        for ch in range(n_chunks):
            st_ref[0, ch] = state.astype(BF16)
            state = gc_f * state + dlt_ref[0, ch]
        state = jnp.zeros((lanes, lanes), F32)
        for ch in order_b:
            st_ref[1, ch] = state.astype(BF16)
            state = gc_b * state + dlt_ref[1, ch]

        dmats = []
        for h in range(2):
            lgf_s = lg_ref[layer, 0, 2 * p + h]
            lgb_s = lg_ref[layer, 1, 2 * p + h]
            dmats.append(jnp.where(dif > 0, jnp.exp(lgf_s * jnp.maximum(dif, 0.0)),
                                   jnp.where(dif < 0, jnp.exp(lgb_s * jnp.maximum(-dif, 0.0)), 2.0)))

        def out_chunk(ch, _):
            r0 = pl.multiple_of(ch * c, c)
            rows = pl.ds(r0, c)
            q = q_ref[rows, lsl]
            k = k_ref[rows, lsl]
            v = p_ref[0, rows, 2 * w + p * lanes:2 * w + (p + 1) * lanes]
            g = p_ref[0, rows, 3 * w + p * lanes:3 * w + (p + 1) * lanes].astype(F32)
            zero = jnp.zeros_like(q)
            s0 = pl.dot(jnp.where(low_half, q, zero), k, trans_b=True)
            s1 = pl.dot(jnp.where(low_half, zero, q), k, trans_b=True)
            o0 = jnp.dot((s0 * dmats[0]).astype(BF16), v, preferred_element_type=F32)
            o1 = jnp.dot((s1 * dmats[1]).astype(BF16), v, preferred_element_type=F32)
            o = jnp.where(low_half, o0, o1)
            qf = q.astype(F32)
            o = o + jnp.dot((qf * qd_f).astype(BF16), st_ref[0, ch], preferred_element_type=F32)
            o = o + jnp.dot((qf * qd_b).astype(BF16), st_ref[1, ch], preferred_element_type=F32)
            oo = o * o
            ss0 = jnp.sum(jnp.where(low_half, oo, 0.0), axis=-1, keepdims=True)
            ss1 = jnp.sum(jnp.where(low_half, 0.0, oo), axis=-1, keepdims=True)
            inv = jnp.where(low_half, lax.rsqrt(ss0 / head_dim + EPS), lax.rsqrt(ss1 / head_dim + EPS))
            o_ref[0, rows, lsl] = (_silu(g) * (o * inv)).astype(o_ref.dtype)
            return 0

        lax.fori_loop(0, n_chunks, out_chunk, 0, unroll=True)


def _retention(p_ret, lg, lg_lane, cos, sin_signed, layer, ctx_len, head_dim):
    b, s, w4 = p_ret.shape
    w = w4 // 4
    n_chunks = s // RET_CHUNK
    return pl.pallas_call(
        functools.partial(_ret_kernel, layer=layer, ctx_len=ctx_len, seq=s, head_dim=head_dim),
        grid=(b,),
        in_specs=[
            pl.BlockSpec(memory_space=pltpu.SMEM),
            pl.BlockSpec((1, s, w4), lambda i: (i, 0, 0)),
            _const_spec(cos.shape), _const_spec(sin_signed.shape), _layer_spec(lg_lane, layer),
        ],
        out_specs=pl.BlockSpec((1, s, w), lambda i: (i, 0, 0)),
        out_shape=jax.ShapeDtypeStruct((b, s, w), BF16),
        scratch_shapes=[
            pltpu.VMEM((s, w), BF16),
            pltpu.VMEM((s, w), BF16),
            pltpu.VMEM((2, n_chunks, V7X_LANES, V7X_LANES), F32),
            pltpu.VMEM((2, n_chunks, V7X_LANES, V7X_LANES), BF16),
        ],
        compiler_params=_cparams("parallel"),
        name="retention",
    )(lg, p_ret, cos, sin_signed, lg_lane)


def _cm_kernel(hin_ref, wz_ref, lng_ref, lnb_ref, ws_ref, bs_ref, o_ref, z_ref, *, seq, group_dim):
    c = CM_CHUNK
    lanes = V7X_LANES
    w = o_ref.shape[2]
    n_slabs = w // lanes
    low_half = lax.broadcasted_iota(jnp.int32, (c, lanes), 1) < group_dim
    lane_w = lax.broadcasted_iota(jnp.int32, (c, w), 1)
    proj_rows = ROW_BLOCK if seq % ROW_BLOCK == 0 else c
    chunks_per_proj = proj_rows // c

    def project(blk):
        rows = slice(blk * proj_rows, (blk + 1) * proj_rows)
        z_ref[rows, :] = jnp.dot(hin_ref[0, rows, :], wz_ref[...], preferred_element_type=F32)

    def body(ch):
        rows = slice(ch * c, (ch + 1) * c)
        z = _gelu_tanh(z_ref[rows, :])
        u = z[:, :w]
        vn = []
        for sl in range(n_slabs):
            v = z[:, w + sl * lanes:w + (sl + 1) * lanes]
            m0 = jnp.sum(jnp.where(low_half, v, 0.0), axis=-1, keepdims=True) / group_dim
            m1 = jnp.sum(jnp.where(low_half, 0.0, v), axis=-1, keepdims=True) / group_dim
            xc = v - jnp.where(low_half, m0, m1)
            xx = xc * xc
            v0 = jnp.sum(jnp.where(low_half, xx, 0.0), axis=-1, keepdims=True) / group_dim
            v1 = jnp.sum(jnp.where(low_half, 0.0, xx), axis=-1, keepdims=True) / group_dim
            vn.append(xc * jnp.where(low_half, lax.rsqrt(v0 + EPS), lax.rsqrt(v1 + EPS)))
        vn = jnp.concatenate(vn, axis=-1) * lng_ref[...] + lnb_ref[...]
        s_all = jnp.dot(ws_ref[...], vn.astype(BF16), preferred_element_type=F32)
        sp = s_all[0:c]
        for gi in range(1, CM_GROUPS):
            sp = jnp.where(lane_w >= gi * group_dim, s_all[gi * c:(gi + 1) * c], sp)
        o_ref[0, rows, :] = (u * (sp + bs_ref[...])).astype(o_ref.dtype)

    project(0)
    for ch in range(seq // c):
        if ch % chunks_per_proj == 0 and (ch // chunks_per_proj + 1) * proj_rows < seq:
            project(ch // chunks_per_proj + 1)
        body(ch)


def _chunk_mlp(h3, w_cm_b, ln_g, ln_b, ws, bs, layer, group_dim):
    b, s, d = h3.shape
    w = w_cm_b.shape[2] // 2
    return pl.pallas_call(
        functools.partial(_cm_kernel, seq=s, group_dim=group_dim),
        grid=(b,),
        in_specs=[pl.BlockSpec((1, s, d), lambda i: (i, 0, 0))]
        + [_layer_spec(a, layer) for a in (w_cm_b, ln_g, ln_b, ws, bs)],
        out_specs=pl.BlockSpec((1, s, w), lambda i: (i, 0, 0)),
        out_shape=jax.ShapeDtypeStruct((b, s, w), BF16),
        scratch_shapes=[pltpu.VMEM((s, 2 * w), F32)],
        compiler_params=_cparams("parallel"),
        name="chunk_gmlp",
    )(h3, w_cm_b, ln_g, ln_b, ws, bs)


def _swiglu_rows(h, w1_ref, w3_ref, w2_ref):
    f = w1_ref.shape[1]
    acc = None
    for c0 in range(0, f, FF_CHUNK):
        a = jnp.dot(h, w1_ref[:, c0:c0 + FF_CHUNK], preferred_element_type=F32)
        g = jnp.dot(h, w3_ref[:, c0:c0 + FF_CHUNK], preferred_element_type=F32)
        act = (_silu(a) * g).astype(BF16)
        part = jnp.dot(act, w2_ref[c0:c0 + FF_CHUNK, :], preferred_element_type=F32)
        acc = part if acc is None else acc + part
    return acc


def _ffn_kernel(*refs, n_src, ctx_rows):
    src = refs[:n_src]
    (gmix_ref, gpre_ref, gpost_ref, ml_ref, mc_ref, pa_ref, pb_ref, pc_ref, wo_ref,
     w1_ref, w3_ref, w2_ref, o_ref, xm_ref, h_ref) = refs[n_src:n_src + 15]
    x_ref = _residual_rows(src, refs[-1], ctx_rows)
    y = _mix_out((pa_ref, pb_ref, pc_ref), wo_ref, h_ref)
    _gated_residual(x_ref[...], y, gmix_ref, ml_ref, mc_ref, xm_ref, ctx_rows, 2)
    _prenorm_modulate(xm_ref[...], gpre_ref, ml_ref, mc_ref, h_ref, ctx_rows, 3)
    y = _swiglu_rows(h_ref[...], w1_ref, w3_ref, w2_ref)
    _gated_residual(xm_ref[...], y, gpost_ref, ml_ref, mc_ref, o_ref, ctx_rows, 5)


def _mix_dense_ffn(src, rows, gains, mod_all, parts, w_out_b, w1, w3, w2, layer, j):
    d = src[0].shape[1]
    ml_spec, mc_spec = rows.mod_specs(mod_all, layer)
    src_specs, src_scratch = _source_specs(rows, src, d)
    return pl.pallas_call(
        functools.partial(_ffn_kernel, n_src=len(src), ctx_rows=rows.ctx_rows),
        grid=rows.grid,
        in_specs=src_specs + [_layer_spec(g, layer) for g in gains[1:]] + [ml_spec, mc_spec]
        + [rows.slab(a.shape[1]) for a in parts] + [_layer_spec(w_out_b, layer)]
        + [_layer_spec(w, j) for w in (w1, w3, w2)],
        out_specs=rows.slab(d),
        out_shape=jax.ShapeDtypeStruct((rows.batch * rows.seq, d), F32),
        scratch_shapes=[pltpu.VMEM((rows.rb, d), F32), pltpu.VMEM((rows.rb, d), BF16)] + src_scratch,
        input_output_aliases={0: 0} if len(src) == 1 else {},
        compiler_params=_cparams("parallel", "parallel"),
        name="mix_dense_swiglu",
    )(*src, *gains[1:], mod_all, mod_all, *parts, w_out_b, w1, w3, w2)


def _router_kernel(x_ref, gmix_ref, g_ref, ml_ref, mc_ref, pa_ref, pb_ref, pc_ref, wo_ref, wr_ref,
                   xo_ref, h_ref, idx_ref, wgt_ref, cnt_ref, hf_ref, tri_ref, run_ref,
                   *, ctx_rows, n_experts, group_batches):
    first_step = (pl.program_id(0) == 0) & (pl.program_id(1) == 0)

    @pl.when(first_step)
    def _():
        r = lax.broadcasted_iota(jnp.int32, tri_ref.shape, 0)
        c = lax.broadcasted_iota(jnp.int32, tri_ref.shape, 1)
        tri_ref[...] = jnp.where(c < r, 1.0, 0.0).astype(tri_ref.dtype)

    @pl.when((pl.program_id(0) % group_batches == 0) & (pl.program_id(1) == 0))
    def _():
        run_ref[...] = jnp.zeros_like(run_ref)

    y = _mix_out((pa_ref, pb_ref, pc_ref), wo_ref, h_ref)
    _gated_residual(x_ref[...], y, gmix_ref, ml_ref, mc_ref, xo_ref, ctx_rows, 2)
    _prenorm_modulate(xo_ref[...], g_ref, ml_ref, mc_ref, hf_ref, ctx_rows, 3)
    h = hf_ref[...]
    h_ref[...] = h.astype(h_ref.dtype)
    logits = jnp.dot(h, wr_ref[...], preferred_element_type=F32)
    lane = lax.broadcasted_iota(jnp.int32, logits.shape, 1).astype(F32)
    neg_inf = jnp.float32(-jnp.inf)
    logits = jnp.where(lane < n_experts, logits, neg_inf)
    big = jnp.float32(V7X_LANES)
    m1 = jnp.max(logits, axis=-1, keepdims=True)
    i1 = jnp.min(jnp.where(logits == m1, lane, big), axis=-1, keepdims=True)
    rest = jnp.where(lane == i1, neg_inf, logits)
    m2 = jnp.max(rest, axis=-1, keepdims=True)
    i2 = jnp.min(jnp.where(rest == m2, lane, big), axis=-1, keepdims=True)
    e2 = jnp.exp(m2 - m1)
    w1 = 1.0 / (1.0 + e2)
    w2 = e2 / (1.0 + e2)
    oh1 = jnp.where(lane == i1, 1.0, 0.0)
    oh2 = jnp.where(lane == i2, 1.0, 0.0)
    both = oh1 + oh2
    before = jnp.dot(tri_ref[...], both.astype(BF16), preferred_element_type=F32) + run_ref[0:1, :]
    r1 = jnp.sum(before * oh1, axis=-1, keepdims=True)
    r2 = jnp.sum(before * oh2, axis=-1, keepdims=True)
    run_ref[...] = run_ref[...] + jnp.sum(both, axis=0, keepdims=True)
    cnt_ref[0] = run_ref[...]
    packed = jnp.where(lane == 0, i1, jnp.where(lane == 1, i2, jnp.where(lane == 2, r1,
                       jnp.where(lane == 3, r2, 0.0))))
    idx_ref[...] = packed.T[:idx_ref.shape[0], :].astype(jnp.int32)
    out_lane = lax.broadcasted_iota(jnp.int32, wgt_ref.shape, 1)
    wgt_ref[...] = jnp.where(out_lane == 0, w1, w2)


def _mix_router(xs2, rows, gains, mod_all, parts, w_out_b, wr_pad, layer, j, n_experts):
    d = xs2.shape[1]
    n_walk = rows.n_batch * rows.rows
    ml_spec, mc_spec = rows.mod_specs(mod_all, layer)
    if rows.latent_only:
        x_spec, x_shape, aliases = rows.local(d), (n_walk, d), {}
    else:
        x_spec, x_shape, aliases = rows.slab(d), xs2.shape, {0: 0}
    return pl.pallas_call(
        functools.partial(_router_kernel, ctx_rows=rows.ctx_rows, n_experts=n_experts,
                          group_batches=rows.n_batch),
        grid=rows.grid,
        in_specs=[rows.slab(d), _layer_spec(gains[1], layer), _layer_spec(gains[2], layer), ml_spec, mc_spec]
        + [rows.slab(a.shape[1]) for a in parts]
        + [_layer_spec(w_out_b, layer), _layer_spec(wr_pad, j)],
        out_specs=[x_spec, rows.local(d),
                   pl.BlockSpec((V7X_SUBLANES, rows.rb), lambda b, j_: (0, b * rows.n_blk + j_)),
                   rows.local(V7X_SUBLANES),
                   pl.BlockSpec((1, V7X_SUBLANES, V7X_LANES), lambda b, j_: (b, 0, 0))],
        out_shape=[jax.ShapeDtypeStruct(x_shape, F32),
                   jax.ShapeDtypeStruct((n_walk, d), BF16),
                   jax.ShapeDtypeStruct((V7X_SUBLANES, n_walk), jnp.int32),
                   jax.ShapeDtypeStruct((n_walk, V7X_SUBLANES), F32),
                   jax.ShapeDtypeStruct((rows.n_batch, V7X_SUBLANES, V7X_LANES), F32)],
        scratch_shapes=[pltpu.VMEM((rows.rb, d), F32), pltpu.VMEM((rows.rb, rows.rb), BF16),
                        pltpu.VMEM((V7X_SUBLANES, V7X_LANES), F32)],
        input_output_aliases=aliases,
        compiler_params=_cparams("arbitrary", "arbitrary"),
        name="mix_moe_router",
    )(xs2, gains[1], gains[2], mod_all, mod_all, *parts, w_out_b, wr_pad)


def _grouped_kernel(te_ref, tv_ref, h_ref, w1_ref, w3_ref, w2_ref, o_ref):
    i = pl.program_id(0)

    @pl.when(tv_ref[i] != 0)
    def _():
        o_ref[...] = _swiglu_rows(h_ref[...], w1_ref, w3_ref, w2_ref).astype(o_ref.dtype)

    @pl.when(tv_ref[i] == 0)
    def _():
        o_ref[...] = jnp.zeros_like(o_ref)


def _grouped_swiglu(layer_idx, tile_expert, tile_valid, hs, w1, w3, w2):
    p, d = hs.shape
    f = w1.shape[3]
    tm = MOE_TILE
    grid_spec = pltpu.PrefetchScalarGridSpec(
        num_scalar_prefetch=2,
        grid=(p // tm,),
        in_specs=[
            pl.BlockSpec((tm, d), lambda i, te, tv: (i, 0)),
            pl.BlockSpec((None, None, d, f), lambda i, te, tv: (layer_idx, te[i], 0, 0)),
            pl.BlockSpec((None, None, d, f), lambda i, te, tv: (layer_idx, te[i], 0, 0)),
            pl.BlockSpec((None, None, f, d), lambda i, te, tv: (layer_idx, te[i], 0, 0)),
        ],
        out_specs=pl.BlockSpec((tm, d), lambda i, te, tv: (i, 0)),
    )
    return pl.pallas_call(
        _grouped_kernel,
        grid_spec=grid_spec,
        out_shape=jax.ShapeDtypeStruct((p, d), BF16),
        compiler_params=_cparams("arbitrary"),
        name="moe_grouped_swiglu",
    )(tile_expert, tile_valid, hs, w1, w3, w2)


def _combine_kernel(*refs, ctx_rows, n_in, with_proj):
    x_ref, g_ref, ml_ref, mc_ref, y0_ref, y1_ref, wgt_ref = refs[:7]
    o_ref = refs[n_in]
    wgt = wgt_ref[...]
    y = wgt[:, 0:1] * y0_ref[...].astype(F32) + wgt[:, 1:2] * y1_ref[...].astype(F32)
    _gated_residual(x_ref[...], y, g_ref, ml_ref, mc_ref, o_ref, ctx_rows, 5)
    if with_proj:
        gn_ref, mln_ref, mcn_ref, w_ref = refs[7:11]
        h_ref, pr_ref = refs[n_in + 1:n_in + 3]
        _prenorm_modulate(o_ref[...], gn_ref, mln_ref, mcn_ref, h_ref, ctx_rows, 0)
        pr_ref[...] = jnp.dot(h_ref[...], w_ref[...], preferred_element_type=F32).astype(pr_ref.dtype)


def _combine(xres, g_all, mod_all, y0, y1, wgt, layer, rows, dest, proj=None, proj_dest=None):
    d = xres.shape[1]
    ml_spec, mc_spec = rows.mod_specs(mod_all, layer)
    x_spec = rows.slab(d) if dest is None else rows.local(d)
    in_specs = [x_spec, _layer_spec(g_all, layer), ml_spec, mc_spec,
                rows.local(d), rows.local(d), rows.local(V7X_SUBLANES)]
    args = [xres, g_all, mod_all, mod_all, y0, y1, wgt]
    if proj is not None:
        gn_all, w_ret_b = proj
        widths = (d, w_ret_b.shape[2])
        mln_spec, mcn_spec = rows.mod_specs(mod_all, layer + 1)
        in_specs += [_layer_spec(gn_all, layer + 1), mln_spec, mcn_spec, _layer_spec(w_ret_b, layer + 1)]
        args += [gn_all, mod_all, mod_all, w_ret_b]
    if dest is None:
        out_specs, out_shapes, aliases = [rows.slab(d)], [jax.ShapeDtypeStruct(xres.shape, F32)], {0: 0}
    else:
        out_specs, out_shapes, aliases = [rows.walked(d)], [jax.ShapeDtypeStruct(dest.shape, F32)], {}
        if not isinstance(dest, jax.ShapeDtypeStruct):
            in_specs.append(pl.BlockSpec(memory_space=pl.ANY))
            args.append(dest)
            aliases = {len(args) - 1: 0}
    if proj is not None:
        n_total = rows.batch * rows.seq
        out_specs += [rows.slab(n) for n in widths]
        out_shapes += [jax.ShapeDtypeStruct((n_total, n), BF16) for n in widths]
        for k, arr in enumerate(proj_dest or ()):
            in_specs.append(pl.BlockSpec(memory_space=pl.ANY))
            args.append(arr)
            aliases[len(args) - 1] = 1 + k
    res = pl.pallas_call(
        functools.partial(_combine_kernel, ctx_rows=rows.ctx_rows, n_in=len(args), with_proj=proj is not None),
        grid=rows.grid,
        in_specs=in_specs,
        out_specs=out_specs,
        out_shape=out_shapes,
        input_output_aliases=aliases,
        compiler_params=_cparams("parallel", "parallel"),
        name="moe_combine",
    )(*args)
    return res if proj is not None else res[0]


def _gather_rows(src, idx):
    return src.at[idx].get(mode="promise_in_bounds")


def _routing_tables(experts, ranks, counts, tok0, n_exp):
    n_tok = experts.shape[1]
    tm = MOE_TILE
    padded = ((counts + tm - 1) // tm) * tm
    ends = jnp.cumsum(padded)
    starts = ends - padded
    start_of = jnp.zeros_like(experts)
    for e in range(n_exp):
        start_of = jnp.where(experts == e, starts[e], start_of)
    pos = start_of + ranks
    n_rows = TOP_K * n_tok + n_exp * tm
    n_tiles = n_rows // tm
    tok = jnp.broadcast_to(tok0 + jnp.arange(n_tok, dtype=jnp.int32)[None, :], pos.shape)
    src_tok = jnp.zeros((n_rows,), jnp.int32).at[pos.reshape(-1)].set(
        tok.reshape(-1), unique_indices=True, mode="promise_in_bounds")
    tile_start = jnp.arange(n_tiles, dtype=jnp.int32) * tm
    tile_expert = jnp.minimum(jnp.sum((tile_start[:, None] >= ends[None, :]).astype(jnp.int32), axis=1),
                              n_exp - 1)
    tile_valid = (tile_start < ends[-1]).astype(jnp.int32)
    return pos, src_tok, tile_expert, tile_valid


def _mix_moe_ffn(xs2, shape, gains, mod_all, parts, w_out_b, wr_pad, w1, w3, w2, layer, j, n_exp, last,
                 proj=None):
    b, s, ctx_len = shape
    d = xs2.shape[1]
    bh = b // MOE_SPLITS
    groups = [_Rows(b, s, ctx_len, latent_only=last, batch0=g * bh, n_batch=bh) for g in range(MOE_SPLITS)]
    r = groups[0].rows
    routed = []
    for rows_g in groups:
        xres, h, idx, wgt, cnt = _mix_router(xs2, rows_g, gains, mod_all, parts, w_out_b, wr_pad,
                                             layer, j, n_exp)
        if not last:
            xs2 = xres
        routed.append((xres, h, idx, wgt, cnt))
    out = jax.ShapeDtypeStruct((b * r, d), F32) if last else None
    nxt = None
    for rows_g, (xres, h, idx, wgt, cnt) in zip(groups, routed):
        counts = cnt[bh - 1, 0, :n_exp].astype(jnp.int32)
        pos, src_tok, tile_expert, tile_valid = _routing_tables(
            idx[:TOP_K], idx[TOP_K:2 * TOP_K], counts, 0, n_exp)
        hs = _gather_rows(h, src_tok)
        ys = _grouped_swiglu(j, tile_expert, tile_valid, hs, w1, w3, w2)
        y0 = _gather_rows(ys, pos[0])
        y1 = _gather_rows(ys, pos[1])
        res = _combine(xres if last else xs2, gains[3], mod_all, y0, y1, wgt, layer, rows_g, out, proj, nxt)
        if proj is not None:
            res, nxt = res[0], tuple(res[1:])
        if last:
            out = res
        else:
            xs2 = res
    return (out if last else xs2), nxt


def _block_diag(wh):
    h, d = wh.shape[-3], wh.shape[-1]
    eye = jnp.eye(h, dtype=wh.dtype)
    out = jnp.einsum("...hij,hg->...higj", wh, eye)
    return out.reshape(wh.shape[:-3] + (h * d, h * d))


def kernel(x, c, ctx, c_ctx, w_mod, b_mod, g_mix_pre, g_mix_post, g_ffn_pre, g_ffn_post, w_in, w_out,
           lru_conv_w, lru_conv_b, lru_wa, lru_ba, lru_wx, lru_bx, lru_lam, ret_theta,
           cm_ln_g, cm_ln_b, cm_ws, cm_bs, ffn_w1, ffn_w3, ffn_w2, router_w, moe_w1, moe_w3, moe_w2):
    b, l, d = x.shape
    lc = ctx.shape[1]
    s = lc + l
    depth = w_mod.shape[0]
    lru_w = lru_conv_w.shape[2]
    cm_w = cm_ln_g.shape[1] * cm_ln_g.shape[2]
    ret_w = (w_in.shape[2] - 2 * lru_w - 2 * cm_w) // 4
    head_dim = ret_w // RET_HEADS
    group_dim = cm_ln_g.shape[2]
    n_exp = router_w.shape[2]
    assert s % ROW_BLOCK == 0 and lc < ROW_BLOCK and lc % RET_CHUNK == 0 and l % RET_CHUNK == 0
    assert ret_w % V7X_LANES == 0 and cm_w % V7X_LANES == 0 and 2 * head_dim == V7X_LANES
    assert b % MOE_SPLITS == 0 and l % lc == 0 and l >= ROW_BLOCK

    n_rows = l // GRID_W
    rows = jnp.repeat(jnp.arange(n_rows, dtype=F32), GRID_W)
    cols = jnp.tile(jnp.arange(GRID_W, dtype=F32), n_rows)
    pairs = head_dim // 4
    freqs = ROPE_BASE ** (-jnp.arange(pairs, dtype=F32) / pairs)
    ang = jnp.concatenate([rows[:, None] * freqs, cols[:, None] * freqs], axis=-1)
    cos = jnp.tile(jnp.repeat(jnp.cos(ang), 2, axis=1), (1, V7X_LANES // head_dim))
    sin = jnp.repeat(jnp.sin(ang), 2, axis=1) * jnp.tile(jnp.array([-1.0, 1.0], F32), head_dim // 2)
    sin_signed = jnp.tile(sin, (1, V7X_LANES // head_dim))

    w_out_b = w_out.astype(BF16)
    c_ret, c_cm = 2 * lru_w, 2 * lru_w + 4 * ret_w
    w_lru_b, w_ret_b, w_cm_b = (w_in[:, :, :c_ret].astype(BF16), w_in[:, :, c_ret:c_cm].astype(BF16),
                                w_in[:, :, c_cm:].astype(BF16))
    ffn_b1, ffn_b3, ffn_b2 = ffn_w1.astype(BF16), ffn_w3.astype(BF16), ffn_w2.astype(BF16)
    moe_b1, moe_b3, moe_b2 = moe_w1.astype(BF16), moe_w3.astype(BF16), moe_w2.astype(BF16)
    n_slab = lru_w // V7X_LANES
    hd = lru_wa.shape[-1]
    per_slab = V7X_LANES // hd

    def slab_blocks(wh):
        return _block_diag(wh.reshape(depth, n_slab, per_slab, hd, hd))

    wg = (0.5 * jnp.concatenate([slab_blocks(lru_wa[:, 0]), slab_blocks(lru_wx[:, 0]),
                                 slab_blocks(lru_wa[:, 1]), slab_blocks(lru_wx[:, 1])], axis=3)).astype(BF16)
    bg = 0.5 * jnp.concatenate([t.reshape(depth, n_slab, 1, V7X_LANES)
                                for t in (lru_ba[:, 0], lru_bx[:, 0], lru_ba[:, 1], lru_bx[:, 1])], axis=3)
    conv_b = lru_conv_b.reshape(depth, 1, lru_w)
    lam = lru_lam.reshape(depth, 2, lru_w)
    lg = jax.nn.log_sigmoid(ret_theta.astype(F32))
    lg_lane = jnp.repeat(lg, head_dim, axis=2)
    ln_g = cm_ln_g.reshape(depth, 1, cm_w)
    ln_b = cm_ln_b.reshape(depth, 1, cm_w)
    ws = cm_ws.reshape(depth, CM_GROUPS * CM_CHUNK, CM_CHUNK).astype(BF16)
    bs = jnp.repeat(jnp.swapaxes(cm_bs, 1, 2), group_dim, axis=2)
    wr_pad = jnp.zeros((router_w.shape[0], d, V7X_LANES), F32).at[:, :, :n_exp].set(router_w)
    gains = [g.reshape(depth, 1, d) for g in (g_mix_pre, g_mix_post, g_ffn_pre, g_ffn_post)]

    n_cond = ((b + 1 + V7X_SUBLANES - 1) // V7X_SUBLANES) * V7X_SUBLANES
    cpad = jnp.zeros((n_cond, d), F32).at[:b].set(c).at[b].set(c_ctx)
    mod_all = _modulation(cpad, w_mod, b_mod).reshape(depth, n_cond, N_MOD, d)

    rows = _Rows(b, s, lc)
    src = (ctx.reshape(b * lc, d), x.reshape(b * l, d))
    nxt = None
    for layer in range(depth):
        h2, p_ret = nxt or _inproj(src, rows, gains[0], mod_all, w_ret_b, layer)
        nxt = None
        h3 = h2.reshape(b, s, d)
        o_lru = _lru(h3, w_lru_b, lru_conv_w, conv_b, wg, bg, lam, layer, lc)
        o_ret = _retention(p_ret.reshape(b, s, -1), lg, lg_lane, cos, sin_signed, layer, lc, head_dim)
        o_cm = _chunk_mlp(h3, w_cm_b, ln_g, ln_b, ws, bs, layer, group_dim)
        parts = tuple(o.reshape(b * s, -1) for o in (o_lru, o_ret, o_cm))

        j = layer // 2
        last = layer == depth - 1
        if layer % 2 == 0:
            xs2 = _mix_dense_ffn(src, rows, gains, mod_all, parts, w_out_b, ffn_b1, ffn_b3, ffn_b2, layer, j)
        else:
            xs2, nxt = _mix_moe_ffn(src[0], (b, s, lc), gains, mod_all, parts, w_out_b, wr_pad,
                                    moe_b1, moe_b3, moe_b2, layer, j, n_exp, last,
                                    proj=None if last else (gains[0], w_ret_b))
            if last:
                return xs2.reshape(b, l, d)
        src = (xs2,)
    return xs2.reshape(b, s, d)[:, lc:, :]
```
